```python
import jax
import jax.numpy as jnp
from jax import lax
import numpy as np

D_MODEL = 1024
BATCH = 8
SEQ = 2048
DEPTH = 4
DEC_BATCH = 32
DEC_SEQ = 4
PAST_LEN = 8192
PAGE_SIZE = 128

N_MIXERS = 4
HD = 64
BLK = 128
EPS = 1e-6
ADA_N = 6
F32 = jnp.float32

NSA_H = 16
NSA_KVH = 4
NSA_G = NSA_H // NSA_KVH
NSA_CMP_LEN = 32
NSA_CMP_STRIDE = 16
NSA_CMP_HID = 128
NSA_SEL_LEN = 64
NSA_TOPN = 16
NSA_WIN = 512
NSA_SEL_QBLK = 32
NSA_FORCE_BONUS = 1000.0
NSA_IN = NSA_H * HD + 6 * NSA_KVH * HD + 3 * NSA_H

DIL_PAIRS = ((128, 1), (512, 4), (2048, 16))
DIL_GROUPS = len(DIL_PAIRS)
DIL_SLOTS = 8
DIL_IN = 3 * DIL_GROUPS * DIL_SLOTS * HD

GLA_H = 4
GLA_DK = 128
GLA_DV = 256
GLA_RANK = 16
GLA_TAU = 16.0
GLA_IN = 2 * GLA_H * GLA_DK + 2 * GLA_H * GLA_DV + GLA_RANK

HG_H = 8
HG_DK = 128
HG_DV = 128
HG_IN = 2 * HG_H * HG_DK + 2 * HG_H * HG_DV
REC_CHUNK = 32

MOE_GROUPS = 4
MOE_PER_GROUP = 4
MOE_E = MOE_GROUPS * MOE_PER_GROUP
MOE_TOPK = 2
MOE_FF = 256

kernel_name = 'hybrid_nsa_dilated_gla_hgrn2_hmoe_step'


def rmsnorm(x, g):
    xf = x.astype(F32)
    y = xf * lax.rsqrt(jnp.mean(xf * xf, axis=-1, keepdims=True) + EPS)
    return y.astype(x.dtype) * g


def masked_softmax(s, mask):
    s = jnp.where(mask, s.astype(F32), -jnp.inf)
    m = jnp.max(s, axis=-1, keepdims=True)
    m = jnp.where(jnp.isfinite(m), m, 0.0)
    e = jnp.exp(s - m)
    den = jnp.sum(e, axis=-1, keepdims=True)
    p = e / jnp.where(den > 0, den, 1.0)
    return p, (m + jnp.log(den))[..., 0]


def adaln(c, w, b):
    return (jax.nn.silu(c) @ w + b).reshape(c.shape[0], ADA_N, 1, D_MODEL)


def modulate(x, g, shift, scale):
    return rmsnorm(x, g) * (1 + scale) + shift


def banded_attention(q, k, v, n_back):
    B, L, KH, G, D = q.shape
    nb = -(-n_back // BLK)
    NB = L // BLK
    W = (nb + 1) * BLK
    pad = ((0, 0), (nb * BLK, 0), (0, 0), (0, 0))
    kp, vp = jnp.pad(k, pad), jnp.pad(v, pad)
    idx = jnp.arange(NB)[:, None] * BLK + jnp.arange(W)[None, :]
    kb, vb = kp[:, idx], vp[:, idx]
    qb = q.reshape(B, NB, BLK, KH, G, D)
    s = jnp.einsum('bnqhgd,bnkhd->bnhgqk', qb, kb, preferred_element_type=F32) * (D ** -0.5)
    dist = jnp.arange(BLK)[:, None] + nb * BLK - jnp.arange(W)[None, :]
    band = (dist >= 0) & (dist <= n_back)
    mask = band[None] & (idx >= nb * BLK)[:, None, :]
    p, lse = masked_softmax(s, mask[None, :, None, None])
    o = jnp.einsum('bnhgqk,bnkhd->bnqhgd', p.astype(v.dtype), vb).reshape(B, L, KH, G, D)
    return o, lse.transpose(0, 1, 4, 2, 3).reshape(B, L, KH, G)


def nsa_project(h, w_in):
    B, L, _ = h.shape
    nq, nkv = NSA_H * HD, 6 * NSA_KVH * HD
    z = h @ w_in
    q = z[..., :nq].reshape(B, L, NSA_KVH, NSA_G, HD)
    kv = z[..., nq:nq + nkv].reshape(B, L, 6, NSA_KVH, HD)
    gates = jax.nn.sigmoid(z[..., nq + nkv:].astype(F32)).reshape(B, L, NSA_KVH, NSA_G, 3)
    return q, kv, gates


def nsa_compress(t, pe, w1, w2):
    B, L = t.shape[:2]
    r = NSA_CMP_LEN // NSA_CMP_STRIDE
    n_ch = L // NSA_CMP_STRIDE
    nc = n_ch - r + 1
    ch = t[:, :n_ch * NSA_CMP_STRIDE].reshape(B, n_ch, NSA_CMP_STRIDE, NSA_KVH, HD)
    blocks = jnp.concatenate([ch[:, j:j + nc] for j in range(r)], axis=2) + pe[:, None, :]
    flat = blocks.transpose(0, 1, 3, 2, 4).reshape(B, nc, NSA_KVH, NSA_CMP_LEN * HD)
    return jax.nn.silu(flat @ w1) @ w2


def nsa_cmp_sel(q, k_c, v_c, k_s, v_s, qpos, pe, w1, w2):
    B, Lq = q.shape[:2]
    Lk = k_c.shape[1]
    scale = HD ** -0.5
    kc = nsa_compress(k_c, pe[0], w1[0], w2[0])
    vc = nsa_compress(v_c, pe[1], w1[1], w2[1])
    nc = kc.shape[1]
    c_end = jnp.arange(nc) * NSA_CMP_STRIDE + NSA_CMP_LEN - 1
    s = jnp.einsum('bqhgd,bchd->bhgqc', q, kc, preferred_element_type=F32) * scale
    p_c, _ = masked_softmax(s, (c_end[None, :] <= qpos[:, None])[None, None, None])
    o_cmp = jnp.einsum('bhgqc,bchd->bqhgd', p_c.astype(vc.dtype), vc)
    ns = -(-Lk // NSA_SEL_LEN)
    cst = jnp.arange(nc)[:, None] * NSA_CMP_STRIDE
    jst = jnp.arange(ns)[None, :] * NSA_SEL_LEN
    overlap = ((cst < jst + NSA_SEL_LEN) & (cst + NSA_CMP_LEN > jst)).astype(F32)
    imp = jnp.einsum('bhgqc,cj->bhqj', p_c, overlap)
    tb = (qpos // NSA_SEL_LEN)[:, None]
    j = jnp.arange(ns)[None, :]
    forced = (j == 0) | (j == tb) | (j == tb - 1)
    score = jnp.where(j <= tb, imp + jnp.where(forced, NSA_FORCE_BONUS, 0.0), -jnp.inf)
    n_sel = min(NSA_TOPN, ns)
    _, idx = lax.top_k(score, n_sel)
    pad = ns * NSA_SEL_LEN - Lk

    def to_blocks(t):
        t = jnp.pad(t, ((0, 0), (0, pad), (0, 0), (0, 0)))
        return t.reshape(B, ns, NSA_SEL_LEN, NSA_KVH, HD).transpose(0, 3, 1, 2, 4)

    kb, vb = to_blocks(k_s), to_blocks(v_s)
    bi = jnp.arange(B)[:, None, None, None]
    hi = jnp.arange(NSA_KVH)[None, :, None, None]

    def sel_attend(args):
        qc, ic, pc = args
        Q = pc.shape[0]
        kg = kb[bi, hi, ic].reshape(B, NSA_KVH, Q, n_sel * NSA_SEL_LEN, HD)
        vg = vb[bi, hi, ic].reshape(B, NSA_KVH, Q, n_sel * NSA_SEL_LEN, HD)
        kpos = (ic[..., None] * NSA_SEL_LEN + jnp.arange(NSA_SEL_LEN)).reshape(B, NSA_KVH, Q, n_sel * NSA_SEL_LEN)
        ss = jnp.einsum('bqhgd,bhqkd->bhgqk', qc, kg, preferred_element_type=F32) * scale
        p, _ = masked_softmax(ss, (kpos <= pc[None, None, :, None])[:, :, None])
        return jnp.einsum('bhgqk,bhqkd->bqhgd', p.astype(vg.dtype), vg)

    if Lq > NSA_SEL_QBLK and Lq % NSA_SEL_QBLK == 0:
        nq = Lq // NSA_SEL_QBLK
        o_sel = lax.map(sel_attend, (
            q.reshape(B, nq, NSA_SEL_QBLK, NSA_KVH, NSA_G, HD).swapaxes(0, 1),
            idx.reshape(B, NSA_KVH, nq, NSA_SEL_QBLK, n_sel).transpose(2, 0, 1, 3, 4),
            qpos.reshape(nq, NSA_SEL_QBLK)))
        o_sel = o_sel.swapaxes(0, 1).reshape(B, Lq, NSA_KVH, NSA_G, HD)
    else:
        o_sel = sel_attend((q, idx, qpos))
    return o_cmp, o_sel


def nsa_merge(o_cmp, o_sel, o_win, gates, w_out):
    B, L = o_cmp.shape[:2]
    o = gates[..., 0:1] * o_cmp + gates[..., 1:2] * o_sel + gates[..., 2:3] * o_win
    return o.astype(o_win.dtype).reshape(B, L, NSA_H * HD) @ w_out


def nsa_prompt(h, w_in, pe, w1, w2, w_out):
    S = h.shape[1]
    q, kv, gates = nsa_project(h, w_in)
    o_cmp, o_sel = nsa_cmp_sel(q, kv[:, :, 0], kv[:, :, 1], kv[:, :, 2], kv[:, :, 3], jnp.arange(S), pe, w1, w2)
    o_win, _ = banded_attention(q, kv[:, :, 4], kv[:, :, 5], NSA_WIN)
    return nsa_merge(o_cmp, o_sel, o_win, gates, w_out), kv[:, :, :4], kv[:, S - min(NSA_WIN, S):, 4:]


def nsa_sample(h, cache_kv, win_buf, page_table, w_in, pe, w1, w2, w_out):
    DB, T = h.shape[:2]
    P = page_table.shape[1] * PAGE_SIZE
    q, kv, gates = nsa_project(h, w_in)
    new_rows = kv[:, :, :4].astype(cache_kv.dtype)
    past = cache_kv[page_table].reshape(DB, P, 4, NSA_KVH, HD)
    full = jnp.concatenate([past, new_rows], axis=1)
    qpos = P + jnp.arange(T)
    o_cmp, o_sel = nsa_cmp_sel(q, full[:, :, 0], full[:, :, 1], full[:, :, 2], full[:, :, 3], qpos, pe, w1, w2)
    Lb = win_buf.shape[1]
    kv_w = jnp.concatenate([win_buf, kv[:, :, 4:].astype(win_buf.dtype)], axis=1)
    d = qpos[:, None] - (P - Lb + jnp.arange(Lb + T))[None, :]
    s = jnp.einsum('bqhgd,bkhd->bhgqk', q, kv_w[:, :, 0], preferred_element_type=F32) * (HD ** -0.5)
    p, _ = masked_softmax(s, (d >= 0) & (d <= NSA_WIN))
    o_win = jnp.einsum('bhgqk,bkhd->bqhgd', p.astype(kv_w.dtype), kv_w[:, :, 1])
    keep = min(NSA_WIN, Lb + T)
    return nsa_merge(o_cmp, o_sel, o_win, gates, w_out), new_rows, kv_w[:, Lb + T - keep:]


def dil_project(h, w_in):
    B, L, _ = h.shape
    return (h @ w_in).reshape(B, L, 3, DIL_GROUPS, DIL_SLOTS, HD)


def dilated_group_prompt(q, k, v, window, dil):
    B, S, H, D = q.shape
    Ls = S // dil
    Lp = -(-Ls // BLK) * BLK

    def split(t):
        t = t.reshape(B, Ls, dil, H, D).transpose(0, 2, 1, 3, 4).reshape(B * dil, Ls, H, D)
        return jnp.pad(t, ((0, 0), (0, Lp - Ls), (0, 0), (0, 0)))

    o, lse = banded_attention(split(q)[:, :, :, None], split(k), split(v), window // dil)
    o = o[:, :Ls, :, 0].reshape(B, dil, Ls, H, D).transpose(0, 2, 1, 3, 4).reshape(B, S, H, D)
    lse = lse[:, :Ls, :, 0].reshape(B, dil, Ls, H).transpose(0, 2, 1, 3).reshape(B, S, H)
    return o, lse


def dilated_group_sample(q, k, v, buf, window, dil):
    Lb, T, D = buf.shape[1], q.shape[1], q.shape[-1]
    kv_all = jnp.concatenate([buf, jnp.stack([k, v], axis=2).astype(buf.dtype)], axis=1)
    M = window // dil + 1
    idx = Lb + jnp.arange(T)[:, None] - dil * jnp.arange(M)[None, :]
    g = kv_all[:, jnp.maximum(idx, 0)]
    s = jnp.einsum('bthd,btmhd->bhtm', q, g[:, :, :, 0], preferred_element_type=F32) * (D ** -0.5)
    p, lse = masked_softmax(s, (idx >= 0)[None, None])
    o = jnp.einsum('bhtm,btmhd->bthd', p.astype(g.dtype), g[:, :, :, 1])
    keep = min(window, Lb + T)
    return o, lse.transpose(0, 2, 1), kv_all[:, Lb + T - keep:]


def dil_merge(outs, lses, w_out):
    wts = jax.nn.softmax(jnp.stack(lses), axis=0)
    o = jnp.einsum('gbls,gblsd->blsd', wts, jnp.stack(outs).astype(F32))
    B, L = o.shape[:2]
    return o.astype(outs[0].dtype).reshape(B, L, DIL_SLOTS * HD) @ w_out


def dilated_prompt(h, w_in, w_out):
    S = h.shape[1]
    qkv = dil_project(h, w_in)
    outs, lses, bufs = [], [], []
    for gi, (window, dil) in enumerate(DIL_PAIRS):
        q, k, v = qkv[:, :, 0, gi], qkv[:, :, 1, gi], qkv[:, :, 2, gi]
        o, lse = dilated_group_prompt(q, k, v, window, dil)
        outs.append(o)
        lses.append(lse)
        bufs.append(jnp.stack([k, v], axis=2)[:, S - min(window, S):])
    return dil_merge(outs, lses, w_out), bufs


def dilated_sample(h, bufs, w_in, w_out):
    qkv = dil_project(h, w_in)
    outs, lses, new = [], [], []
    for gi, ((window, dil), buf) in enumerate(zip(DIL_PAIRS, bufs)):
        o, lse, nb = dilated_group_sample(qkv[:, :, 0, gi], qkv[:, :, 1, gi], qkv[:, :, 2, gi], buf, window, dil)
        outs.append(o)
        lses.append(lse)
        new.append(nb)
    return dil_merge(outs, lses, w_out), new


def gated_linear_scan(q, k, v, g, s0):
    B, L, H, DK = q.shape
    DV = v.shape[-1]
    C = min(REC_CHUNK, L)
    Lp = -(-L // C) * C
    N = Lp // C

    def chunks(t):
        t = jnp.pad(t.astype(F32), ((0, 0), (0, Lp - L), (0, 0), (0, 0)))
        return t.reshape(B, N, C, H, t.shape[-1]).transpose(1, 0, 3, 2, 4)

    causal = jnp.tril(jnp.ones((C, C), dtype=bool))[:, :, None]

    def step(S, inp):
        qc, kc, vc, gc = inp
        b = jnp.cumsum(gc, axis=2)
        rel = jnp.where(causal, b[:, :, :, None, :] - b[:, :, None, :, :], -jnp.inf)
        a = jnp.einsum('bhtk,bhsk,bhtsk->bhts', qc, kc, jnp.exp(rel))
        o = jnp.einsum('bhts,bhsv->bhtv', a, vc) + jnp.einsum('bhtk,bhkv->bhtv', qc * jnp.exp(b), S)
        b_end = b[:, :, -1:, :]
        S = jnp.exp(b_end[:, :, 0, :, None]) * S + jnp.einsum('bhsk,bhsv->bhkv', kc * jnp.exp(b_end - b), vc)
        return S, o

    S, o = lax.scan(step, s0.astype(F32), (chunks(q), chunks(k), chunks(v), chunks(g)))
    o = o.transpose(1, 0, 3, 2, 4).reshape(B, Lp, H, DV)[:, :L]
    return o.astype(v.dtype), S.astype(s0.dtype)


def gla_mixer(h, s0, w_in, w_a2, b_a2, norm_g, w_out):
    B, L, _ = h.shape
    nk, nv = GLA_H * GLA_DK, GLA_H * GLA_DV
    z = h @ w_in
    q = z[..., :nk].reshape(B, L, GLA_H, GLA_DK) * (GLA_DK ** -0.5)
    k = z[..., nk:2 * nk].reshape(B, L, GLA_H, GLA_DK)
    v = z[..., 2 * nk:2 * nk + nv].reshape(B, L, GLA_H, GLA_DV)
    r = z[..., 2 * nk + nv:2 * nk + 2 * nv]
    a = z[..., 2 * nk + 2 * nv:]
    g = (jax.nn.log_sigmoid((a @ w_a2 + b_a2).astype(F32)) / GLA_TAU).reshape(B, L, GLA_H, GLA_DK)
    o, s = gated_linear_scan(q, k, v, g, s0)
    o = rmsnorm(o, norm_g).reshape(B, L, nv) * jax.nn.silu(r)
    return o @ w_out, s


def hgrn2_mixer(h, s0, layer, w_in, lb_logits, norm_g, w_out):
    B, L, _ = h.shape
    nk, nv = HG_H * HG_DK, HG_H * HG_DV
    z = h @ w_in
    fz = z[..., :nk].astype(F32)
    i_in = z[..., nk:nk + nv].reshape(B, L, HG_H, HG_DV)
    q = jax.nn.silu(z[..., nk + nv:2 * nk + nv]).reshape(B, L, HG_H, HG_DK)
    og = z[..., 2 * nk + nv:]
    sm = jax.nn.softmax(lb_logits.astype(F32), axis=0)
    lb = jnp.sum(sm[1:layer + 1], axis=0)
    log_f = jnp.logaddexp(jnp.log(lb), jnp.log1p(-lb) + jax.nn.log_sigmoid(fz))
    k = (1.0 - lb) * jax.nn.sigmoid(-fz)
    o, s = gated_linear_scan(q, k.reshape(B, L, HG_H, HG_DK), i_in, log_f.reshape(B, L, HG_H, HG_DK), s0)
    o = rmsnorm(o, norm_g).reshape(B, L, nv) * jax.nn.silu(og)
    return o @ w_out, s


def hier_moe(h, w_rg, b_rg, w_re, b_re, w_up, w_down):
    shp = h.shape
    x = h.reshape(-1, D_MODEL)
    N = x.shape[0]
    g_logit = (x @ w_rg + b_rg).astype(F32)
    grp = jnp.argmax(g_logit, axis=-1)
    p_grp = jnp.take_along_axis(jax.nn.softmax(g_logit, axis=-1), grp[:, None], axis=-1)
    e_logit = (x @ w_re + b_re).astype(F32).reshape(N, MOE_GROUPS, MOE_PER_GROUP)
    e_in = jnp.take_along_axis(e_logit, grp[:, None, None], axis=1)[:, 0]
    top_v, top_i = lax.top_k(e_in, MOE_TOPK)
    w = jax.nn.softmax(top_v, axis=-1) * p_grp
    eid = grp[:, None] * MOE_PER_GROUP + top_i
    gates = jnp.sum(jax.nn.one_hot(eid, MOE_E, dtype=F32) * w[..., None], axis=1)
    hid = jnp.einsum('nd,edf->nef', x, w_up)
    hid = jax.nn.silu(hid[..., :MOE_FF]) * hid[..., MOE_FF:]
    y = jnp.einsum('nef,efd->nd', hid * gates[..., None].astype(hid.dtype), w_down)
    return y.reshape(shp)


def setup_inputs(seed: int = 0) -> dict:
    key = jax.random.key(seed)
    kit = iter(jax.random.split(key, 48))

    def nrm(shape, scale=1.0):
        return jax.random.normal(next(kit), shape, F32) * scale

    n_pages = PAST_LEN // PAGE_SIZE
    n_pool = (5 * DEC_BATCH * n_pages + 3) // 4
    page_table = jax.random.permutation(next(kit), n_pool)[:DEC_BATCH * n_pages]
    page_table = page_table.reshape(DEC_BATCH, n_pages).astype(jnp.int32)
    return {
        'x_prompt': nrm((BATCH, SEQ, D_MODEL)),
        'x_sample': nrm((DEC_BATCH, DEC_SEQ, D_MODEL)),
        'cache_nsa_kv': nrm((n_pool, PAGE_SIZE, 4, NSA_KVH, HD)),
        'state_nsa_win': nrm((DEC_BATCH, min(NSA_WIN, PAST_LEN), 2, NSA_KVH, HD)),
        'state_dil_0': nrm((DEC_BATCH, min(DIL_PAIRS[0][0], PAST_LEN), 2, DIL_SLOTS, HD)),
        'state_dil_1': nrm((DEC_BATCH, min(DIL_PAIRS[1][0], PAST_LEN), 2, DIL_SLOTS, HD)),
        'state_dil_2': nrm((DEC_BATCH, min(DIL_PAIRS[2][0], PAST_LEN), 2, DIL_SLOTS, HD)),
        'state_gla': nrm((DEC_BATCH, GLA_H, GLA_DK, GLA_DV)),
        'state_hgrn': nrm((DEC_BATCH, HG_H, HG_DK, HG_DV), 0.5),
        'page_table': page_table,
        'c_prompt': nrm((BATCH, D_MODEL)),
        'c_sample': nrm((DEC_BATCH, D_MODEL)),
        'nsa_w_in': nrm((D_MODEL, NSA_IN), D_MODEL ** -0.5),
        'nsa_cmp_pe': nrm((2, NSA_CMP_LEN, HD), 0.1),
        'nsa_cmp_w1': nrm((2, NSA_CMP_LEN * HD, NSA_CMP_HID), (NSA_CMP_LEN * HD) ** -0.5),
        'nsa_cmp_w2': nrm((2, NSA_CMP_HID, HD), NSA_CMP_HID ** -0.5),
        'nsa_w_out': nrm((NSA_H * HD, D_MODEL), (NSA_H * HD) ** -0.5),
        'dil_w_in': nrm((D_MODEL, DIL_IN), D_MODEL ** -0.5),
        'dil_w_out': nrm((DIL_SLOTS * HD, D_MODEL), (DIL_SLOTS * HD) ** -0.5),
        'gla_w_in': nrm((D_MODEL, GLA_IN), D_MODEL ** -0.5),
        'gla_w_a2': nrm((GLA_RANK, GLA_H * GLA_DK), GLA_RANK ** -0.5),
        'gla_b_a2': nrm((GLA_H * GLA_DK,), 0.1),
        'gla_norm_g': 1.0 + nrm((GLA_DV,), 0.1),
        'gla_w_out': nrm((GLA_H * GLA_DV, D_MODEL), (GLA_H * GLA_DV) ** -0.5),
        'hg_w_in': nrm((D_MODEL, HG_IN), D_MODEL ** -0.5),
        'hg_lb_logits': nrm((DEPTH, HG_H * HG_DK), 0.1),
        'hg_norm_g': 1.0 + nrm((HG_DV,), 0.1),
        'hg_w_out': nrm((HG_H * HG_DV, D_MODEL), (HG_H * HG_DV) ** -0.5),
        'norm_g': 1.0 + nrm((DEPTH, 2, D_MODEL), 0.1),
        'ada_w': nrm((DEPTH, D_MODEL, ADA_N * D_MODEL), 0.5 * D_MODEL ** -0.5),
        'ada_b': nrm((DEPTH, ADA_N * D_MODEL), 0.1),
        'moe_w_rg': nrm((DEPTH, D_MODEL, MOE_GROUPS), D_MODEL ** -0.5),
        'moe_b_rg': nrm((DEPTH, MOE_GROUPS), 0.01),
        'moe_w_re': nrm((DEPTH, D_MODEL, MOE_E), D_MODEL ** -0.5),
        'moe_b_re': nrm((DEPTH, MOE_E), 0.01),
        'moe_w_up': nrm((DEPTH, MOE_E, D_MODEL, 2 * MOE_FF), D_MODEL ** -0.5),
        'moe_w_down': nrm((DEPTH, MOE_E, MOE_FF, D_MODEL), MOE_FF ** -0.5),
        'final_norm_g': 1.0 + nrm((D_MODEL,), 0.1),
    }


def reference(x_prompt, x_sample, cache_nsa_kv, state_nsa_win, state_dil_0, state_dil_1, state_dil_2,
              state_gla, state_hgrn, page_table, c_prompt, c_sample,
              nsa_w_in, nsa_cmp_pe, nsa_cmp_w1, nsa_cmp_w2, nsa_w_out, dil_w_in, dil_w_out,
              gla_w_in, gla_w_a2, gla_b_a2, gla_norm_g, gla_w_out,
              hg_w_in, hg_lb_logits, hg_norm_g, hg_w_out,
              norm_g, ada_w, ada_b, moe_w_rg, moe_b_rg, moe_w_re, moe_b_re, moe_w_up, moe_w_down,
              final_norm_g):
    xp, xs = x_prompt, x_sample
    for i in range(DEPTH):
        mod_p = adaln(c_prompt, ada_w[i], ada_b[i])
        mod_s = adaln(c_sample, ada_w[i], ada_b[i])
        hp = modulate(xp, norm_g[i, 0], mod_p[:, 0], mod_p[:, 1])
        hs = modulate(xs, norm_g[i, 0], mod_s[:, 0], mod_s[:, 1])
        kind = i % N_MIXERS
        if kind == 0:
            yp, nsa_kv_p, nsa_win_p = nsa_prompt(hp, nsa_w_in, nsa_cmp_pe, nsa_cmp_w1, nsa_cmp_w2, nsa_w_out)
            ys, nsa_kv_s, nsa_win_s = nsa_sample(hs, cache_nsa_kv, state_nsa_win, page_table,
                                                 nsa_w_in, nsa_cmp_pe, nsa_cmp_w1, nsa_cmp_w2, nsa_w_out)
        elif kind == 1:
            yp, (dil_p0, dil_p1, dil_p2) = dilated_prompt(hp, dil_w_in, dil_w_out)
            ys, (dil_s0, dil_s1, dil_s2) = dilated_sample(hs, (state_dil_0, state_dil_1, state_dil_2), dil_w_in, dil_w_out)
        elif kind == 2:
            s0 = jnp.zeros((xp.shape[0], GLA_H, GLA_DK, GLA_DV), state_gla.dtype)
            yp, gla_p = gla_mixer(hp, s0, gla_w_in, gla_w_a2, gla_b_a2, gla_norm_g, gla_w_out)
            ys, gla_s = gla_mixer(hs, state_gla, gla_w_in, gla_w_a2, gla_b_a2, gla_norm_g, gla_w_out)
        else:
            s0 = jnp.zeros((xp.shape[0], HG_H, HG_DK, HG_DV), state_hgrn.dtype)
            yp, hg_p = hgrn2_mixer(hp, s0, i, hg_w_in, hg_lb_logits, hg_norm_g, hg_w_out)
            ys, hg_s = hgrn2_mixer(hs, state_hgrn, i, hg_w_in, hg_lb_logits, hg_norm_g, hg_w_out)
        xp = xp + mod_p[:, 2] * yp
        xs = xs + mod_s[:, 2] * ys
        hp = modulate(xp, norm_g[i, 1], mod_p[:, 3], mod_p[:, 4])
        hs = modulate(xs, norm_g[i, 1], mod_s[:, 3], mod_s[:, 4])
        xp = xp + mod_p[:, 5] * hier_moe(hp, moe_w_rg[i], moe_b_rg[i], moe_w_re[i], moe_b_re[i], moe_w_up[i], moe_w_down[i])
        xs = xs + mod_s[:, 5] * hier_moe(hs, moe_w_rg[i], moe_b_rg[i], moe_w_re[i], moe_b_re[i], moe_w_up[i], moe_w_down[i])
    y_prompt = rmsnorm(xp, final_norm_g)
    y_sample = rmsnorm(xs, final_norm_g)
    return (y_prompt, y_sample, nsa_kv_p, nsa_kv_s, nsa_win_p, nsa_win_s,
            dil_p0, dil_s0, dil_p1, dil_s1, dil_p2, dil_s2, gla_p, gla_s, hg_p, hg_s)
```

```python
import functools

import jax
import jax.numpy as jnp
from jax import lax
from jax.experimental import pallas as pl
from jax.experimental.pallas import tpu as pltpu

F32 = jnp.float32
BF16 = jnp.bfloat16
EPS = 1e-6
D_MODEL = 1024
HD = 64
ADA_N = 6
MIB = 1024 * 1024
NEG_BIG = -1e30


def _params(sem, vmem_mib):
    return pltpu.CompilerParams(dimension_semantics=sem, vmem_limit_bytes=vmem_mib * MIB)


def _silu(x):
    return x * jax.nn.sigmoid(x)


def _bdot(a, b):
    return jnp.dot(a, b, preferred_element_type=F32)


def _dot_nt(a, b):
    return lax.dot_general(a, b, (((1,), (1,)), ((), ())), preferred_element_type=F32)


def _dot_tn(a, b):
    return lax.dot_general(a, b, (((0,), (0,)), ((), ())), preferred_element_type=F32)


def _split_hi_lo(x):
    hi = x.astype(BF16)
    lo = (x - hi.astype(F32)).astype(BF16)
    return hi, lo


def _rms(x):
    return x * lax.rsqrt(jnp.mean(x * x, axis=-1, keepdims=True) + EPS)


class _Tokens:
    def __init__(self, mod, per_row, nb, rows_per_b, tm):
        self.mod = mod
        self.per_row = per_row
        self.nb = nb
        self.rows_per_b = rows_per_b
        self.tm = tm

    def mod_spec(self, k):
        tm = self.tm
        if self.per_row:
            return pl.BlockSpec((1, tm, D_MODEL), lambda i, *_: (k, i, 0))
        nb, rpb = self.nb, self.rows_per_b
        return pl.BlockSpec((1, 1, D_MODEL), lambda i, *_: (k * nb + (i * tm) // rpb, 0, 0))


def _row_spec(tm, n):
    return pl.BlockSpec((tm, n), lambda i, *_: (i, 0))


def _const_spec(shape):
    nd = len(shape)
    return pl.BlockSpec(shape, lambda *_: (0,) * nd)


def _adaln_kernel(c_ref, w_ref, b_ref, o_ref):
    x = _silu(c_ref[...]).astype(BF16)
    o_ref[0] = _bdot(x, w_ref[0].astype(BF16)) + b_ref[0]


def _adaln(c_all, ada_w, ada_b):
    depth, _, n = ada_w.shape
    rows = c_all.shape[0]
    tn = 1536
    return pl.pallas_call(
        _adaln_kernel,
        grid=(depth, n // tn),
        in_specs=[
            pl.BlockSpec((rows, D_MODEL), lambda l, j: (0, 0)),
            pl.BlockSpec((1, D_MODEL, tn), lambda l, j: (l, 0, j)),
            pl.BlockSpec((1, 1, tn), lambda l, j: (l, 0, j)),
        ],
        out_specs=pl.BlockSpec((1, rows, tn), lambda l, j: (l, 0, j)),
        out_shape=jax.ShapeDtypeStruct((depth, rows, n), F32),
        compiler_params=_params(("arbitrary", "arbitrary"), 40),
        name="adaln",
    )(c_all, ada_w, ada_b.reshape(depth, 1, n))


def _modulate(x, g, shift, scale):
    return _rms(x) * g * (1.0 + scale) + shift


def _mod_proj_kernel(x_ref, g_ref, sh_ref, sc_ref, *refs, n_out):
    w_refs, o_refs = refs[:n_out], refs[n_out:]
    h = _modulate(x_ref[...], g_ref[...], sh_ref[0], sc_ref[0]).astype(BF16)
    for w_ref, o_ref in zip(w_refs, o_refs):
        o_ref[...] = _bdot(h, w_ref[...]).astype(o_ref.dtype)


def _mod_proj(tok, x, g, ws, dtypes, name):
    m = x.shape[0]
    tm = tok.tm
    return pl.pallas_call(
        functools.partial(_mod_proj_kernel, n_out=len(ws)),
        grid=(m // tm,),
        in_specs=[_row_spec(tm, D_MODEL), _const_spec((1, D_MODEL)), tok.mod_spec(0), tok.mod_spec(1)]
        + [_const_spec(w.shape) for w in ws],
        out_specs=[_row_spec(tm, w.shape[1]) for w in ws],
        out_shape=[jax.ShapeDtypeStruct((m, w.shape[1]), dt) for w, dt in zip(ws, dtypes)],
        compiler_params=_params(("arbitrary",), 56),
        name=name,
    )(x, g.reshape(1, D_MODEL), tok.mod, tok.mod, *ws)


def _out_proj_kernel(x_ref, gt_ref, *refs, mode, heads):
    if mode == "plain":
        a_ref, w_ref, o_ref = refs
        a = a_ref[...].astype(BF16)
    elif mode == "dil":
        o0, o1, o2, l0, l1, l2, w_ref, o_ref = refs
        la, lb, lc = l0[...], l1[...], l2[...]
        mx = jnp.maximum(jnp.maximum(la, lb), lc)
        wa, wb, wc = jnp.exp(la - mx), jnp.exp(lb - mx), jnp.exp(lc - mx)
        a = ((wa * o0[...] + wb * o1[...] + wc * o2[...]) / (wa + wb + wc)).astype(BF16)
    else:
        s_ref, r_ref, ng_ref, w_ref, o_ref = refs
        dv = s_ref.shape[1] // heads
        parts = []
        for h in range(heads):
            parts.append(_rms(s_ref[:, h * dv:(h + 1) * dv]) * ng_ref[...])
        a = (jnp.concatenate(parts, axis=1) * _silu(r_ref[...])).astype(BF16)
    o_ref[...] = x_ref[...] + gt_ref[0] * _bdot(a, w_ref[...])


def _out_proj(tok, x, ins, w, mode, name, heads=1, col_blocks=None):
    m = x.shape[0]
    tm = tok.tm
    in_specs = [_row_spec(tm, D_MODEL), tok.mod_spec(2)]
    for k, a in enumerate(ins):
        if a.shape[0] != m:
            in_specs.append(_const_spec(a.shape))
        elif col_blocks and col_blocks[k] is not None:
            width, blk = col_blocks[k]
            in_specs.append(pl.BlockSpec((tm, width), lambda i, blk=blk: (i, blk)))
        else:
            in_specs.append(_row_spec(tm, a.shape[1]))
    in_specs.append(_const_spec(w.shape))
    return pl.pallas_call(
        functools.partial(_out_proj_kernel, mode=mode, heads=heads),
        grid=(m // tm,),
        in_specs=in_specs,
        out_specs=_row_spec(tm, D_MODEL),
        out_shape=jax.ShapeDtypeStruct((m, D_MODEL), F32),
        compiler_params=_params(("arbitrary",), 48),
        name=name,
    )(x, tok.mod, *ins, w)


MOE_GROUPS = 4
MOE_PER_GROUP = 4
MOE_FF = 256
ROUTER_LANES = 128


def _router_gates(logits):
    lane = lax.broadcasted_iota(jnp.int32, logits.shape, 1)
    lane_f = lane.astype(F32)
    neg = -jnp.inf
    glog = jnp.where(lane < MOE_GROUPS, logits, neg)
    gmax = jnp.max(glog, axis=-1, keepdims=True)
    grp = jnp.min(jnp.where(glog == gmax, lane_f, 1e9), axis=-1, keepdims=True)
    p_grp = 1.0 / jnp.sum(jnp.exp(glog - gmax), axis=-1, keepdims=True)
    e_grp = ((lane - MOE_GROUPS) >> 2).astype(F32)
    n_e = MOE_GROUPS * MOE_PER_GROUP
    in_grp = (lane >= MOE_GROUPS) & (lane < MOE_GROUPS + n_e) & (e_grp == grp)
    e_in = jnp.where(in_grp, logits, neg)
    v1 = jnp.max(e_in, axis=-1, keepdims=True)
    i1 = jnp.min(jnp.where(e_in == v1, lane_f, 1e9), axis=-1, keepdims=True)
    e2 = jnp.where(lane_f == i1, neg, e_in)
    v2 = jnp.max(e2, axis=-1, keepdims=True)
    i2 = jnp.min(jnp.where(e2 == v2, lane_f, 1e9), axis=-1, keepdims=True)
    t = jnp.exp(v2 - v1)
    w1 = p_grp / (1.0 + t)
    w2 = p_grp * t / (1.0 + t)
    return jnp.where(lane_f == i1, w1, 0.0) + jnp.where(lane_f == i2, w2, 0.0)


def _moe_kernel(x_ref, g_ref, sh_ref, sc_ref, gt_ref, wrh_ref, wrl_ref, br_ref, up_ref, dn_ref, *rest, final):
    if final:
        fg_ref, o_ref, h_sc, gate_sc, acc_sc = rest
    else:
        o_ref, h_sc, gate_sc, acc_sc = rest
    grp = pl.program_id(1)

    @pl.when(grp == 0)
    def _():
        h = _modulate(x_ref[...], g_ref[...], sh_ref[0], sc_ref[0])
        hh, hl = _split_hi_lo(h)
        h_sc[...] = hh
        logits = _bdot(hh, wrh_ref[...]) + _bdot(hl, wrh_ref[...]) + _bdot(hh, wrl_ref[...]) + br_ref[...]
        gate_sc[...] = _router_gates(logits)
        acc_sc[...] = jnp.zeros_like(acc_sc)

    hid = _bdot(h_sc[...], up_ref[0])
    gates = gate_sc[...]
    lane = lax.broadcasted_iota(jnp.int32, gates.shape, 1)
    acts = []
    for e in range(MOE_PER_GROUP):
        col = jnp.sum(jnp.where(lane == MOE_GROUPS + MOE_PER_GROUP * grp + e, gates, 0.0), axis=-1, keepdims=True)
        a = hid[:, 2 * MOE_FF * e:2 * MOE_FF * e + MOE_FF]
        b = hid[:, 2 * MOE_FF * e + MOE_FF:2 * MOE_FF * (e + 1)]
        acts.append((_silu(a) * b * col).astype(BF16))
    acc_sc[...] += _bdot(jnp.concatenate(acts, axis=1), dn_ref[0])

    @pl.when(grp == MOE_GROUPS - 1)
    def _():
        y = x_ref[...] + gt_ref[0] * acc_sc[...]
        if final:
            y = _rms(y) * fg_ref[...]
        o_ref[...] = y


def _moe(tok, x, g, wr_hi, wr_lo, br, up, dn, final_g, name):
    m = x.shape[0]
    tm = tok.tm
    final = final_g is not None
    ins = [x, g.reshape(1, D_MODEL), tok.mod, tok.mod, tok.mod, wr_hi, wr_lo, br, up, dn]
    in_specs = [
        _row_spec(tm, D_MODEL), _const_spec((1, D_MODEL)), tok.mod_spec(3), tok.mod_spec(4), tok.mod_spec(5),
        _const_spec(wr_hi.shape), _const_spec(wr_lo.shape), _const_spec(br.shape),
        pl.BlockSpec((1,) + up.shape[1:], lambda i, e: (e, 0, 0)),
        pl.BlockSpec((1,) + dn.shape[1:], lambda i, e: (e, 0, 0)),
    ]
    if final:
        ins.append(final_g.reshape(1, D_MODEL))
        in_specs.append(_const_spec((1, D_MODEL)))
    return pl.pallas_call(
        functools.partial(_moe_kernel, final=final),
        grid=(m // tm, MOE_GROUPS),
        in_specs=in_specs,
        out_specs=_row_spec(tm, D_MODEL),
        out_shape=jax.ShapeDtypeStruct((m, D_MODEL), F32),
        scratch_shapes=[
            pltpu.VMEM((tm, D_MODEL), BF16),
            pltpu.VMEM((tm, ROUTER_LANES), F32),
            pltpu.VMEM((tm, D_MODEL), F32),
        ],
        compiler_params=_params(("arbitrary", "arbitrary"), 48),
        name=name,
    )(*ins)


NSA_KVH = 4
NSA_G = 4
NSA_CMP_LEN = 32
NSA_CMP_STRIDE = 16
NSA_CMP_HID = 128
NSA_SEL_LEN = 64
NSA_TOPN = 16
NSA_WIN = 512
NSA_FORCE_BONUS = 1000.0
KV_COLS = NSA_KVH * HD


def _compress_rows(src_ref, n_ch, w1_ref, pe_ref, w2_ref):
    hid = [jnp.zeros((n_ch, NSA_CMP_HID), F32) for _ in range(NSA_KVH)]
    for r in range(NSA_CMP_LEN):
        halves = [(src_ref[h, pl.ds(r, n_ch, stride=NSA_CMP_STRIDE), :] + pe_ref[r]).astype(BF16) for h in range(2)]
        for kvh in range(NSA_KVH):
            c0 = (kvh % 2) * HD
            hid[kvh] = hid[kvh] + _bdot(halves[kvh // 2][:, c0:c0 + HD], w1_ref[r])
    outs = [_bdot(_silu(h).astype(BF16), w2_ref[...]) for h in hid]
    return jnp.concatenate(outs, axis=1)


def _masked_softmax_rows(s, mask):
    sm = jnp.where(mask, s, NEG_BIG)
    m = jnp.max(sm, axis=-1, keepdims=True)
    e = jnp.where(mask, jnp.exp(sm - m), 0.0)
    den = jnp.sum(e, axis=-1, keepdims=True)
    return e / jnp.where(den > 0, den, 1.0), m, den


def _top_blocks(score, n_valid, topn):
    lane = lax.broadcasted_iota(jnp.int32, score.shape, 1)
    rank = jnp.zeros(score.shape, F32)
    for i in range(n_valid):
        si = score[:, i:i + 1]
        ahead = (si > score) | ((si == score) & (lane > i))
        rank = rank + jnp.where(ahead, 1.0, 0.0)
    return jnp.where((rank < topn) & (lane < n_valid), 1.0, 0.0)


def _nsa_prompt_kernel(q_ref, kc_ref, vc_ref, ks_ref, vs_ref, kw_ref, vw_ref, gt_ref, ov_ref, ex_ref, o_ref,
                       ks_sc, vs_sc, kw_sc, vw_sc, *, tq, seq, chunk):
    qi = pl.program_id(1)
    t0 = qi * tq

    @pl.when(qi == 0)
    def _():
        ks_sc[...] = ks_ref[...].astype(BF16)
        vs_sc[...] = vs_ref[...].astype(BF16)
        kw_sc[...] = kw_ref[...].astype(BF16)
        vw_sc[...] = vw_ref[...].astype(BF16)

    n_cmp = kc_ref.shape[0]
    n_blk = seq // NSA_SEL_LEN
    wkeys = NSA_WIN + tq
    gates = jax.nn.sigmoid(gt_ref[...])
    tpos = t0 + lax.broadcasted_iota(jnp.int32, (tq, 1), 0)
    rows = NSA_G * tq

    def tile_g(x):
        return jnp.concatenate([x] * NSA_G, axis=0)

    c_idx = lax.broadcasted_iota(jnp.int32, (tq, n_cmp), 1)
    cmp_mask = (c_idx * NSA_CMP_STRIDE + NSA_CMP_LEN - 1 <= tpos) & (c_idx < n_cmp - 1)
    j_idx = lax.broadcasted_iota(jnp.int32, (tq, 128), 1)
    tb = tpos // NSA_SEL_LEN
    forced = (j_idx == 0) | (j_idx == tb) | (j_idx == tb - 1)
    n_sel_chunks = (t0 + tq + chunk - 1) // chunk
    w0 = jnp.maximum(t0 - NSA_WIN, 0)
    w0 = pl.multiple_of(w0, tq)
    wpos = w0 + lax.broadcasted_iota(jnp.int32, (tq, wkeys), 1)
    wdist = tpos - wpos
    win_mask = tile_g((wdist >= 0) & (wdist <= NSA_WIN))

    for kvh in range(NSA_KVH):
        c0 = kvh * HD
        qs = jnp.concatenate(
            [q_ref[:, (kvh * NSA_G + g) * HD:(kvh * NSA_G + g + 1) * HD] for g in range(NSA_G)], axis=0)
        qs = (qs * (HD ** -0.5)).astype(BF16)

        s = _dot_nt(qs, kc_ref[:, c0:c0 + HD].astype(BF16))
        p_c, _, _ = _masked_softmax_rows(s, tile_g(cmp_mask))
        o_cmp = _bdot(p_c.astype(BF16), vc_ref[:, c0:c0 + HD].astype(BF16))
        p_sum = p_c[0:tq]
        for g in range(1, NSA_G):
            p_sum = p_sum + p_c[g * tq:(g + 1) * tq]
        ph, plo = _split_hi_lo(p_sum)
        imp = _bdot(ph, ov_ref[...]) + _bdot(plo, ov_ref[...])
        score = jnp.where(j_idx <= tb, imp + jnp.where(forced, NSA_FORCE_BONUS, 0.0), -jnp.inf)
        sel = _top_blocks(score, n_blk, NSA_TOPN).astype(BF16)

        def sel_body(ci, carry):
            m, l, acc = carry
            k0 = pl.multiple_of(ci * chunk, chunk)
            kch = ks_sc[pl.ds(k0, chunk), c0:c0 + HD]
            vch = vs_sc[pl.ds(k0, chunk), c0:c0 + HD]
            sc = _dot_nt(qs, kch)
            chosen = _bdot(sel, ex_ref[ci])
            kpos = k0 + lax.broadcasted_iota(jnp.int32, (tq, chunk), 1)
            mask = tile_g((chosen > 0.5) & (kpos <= tpos))
            sm = jnp.where(mask, sc, NEG_BIG)
            m_new = jnp.maximum(m, jnp.max(sm, axis=-1, keepdims=True))
            p = jnp.where(mask, jnp.exp(sm - m_new), 0.0)
            alpha = jnp.exp(m - m_new)
            l = alpha * l + jnp.sum(p, axis=-1, keepdims=True)
            acc = alpha * acc + _bdot(p.astype(BF16), vch)
            return m_new, l, acc

        init = (jnp.full((rows, 1), NEG_BIG, F32), jnp.zeros((rows, 1), F32), jnp.zeros((rows, HD), F32))
        _, l_s, acc_s = lax.fori_loop(0, n_sel_chunks, sel_body, init)
        o_sel = acc_s / jnp.where(l_s > 0, l_s, 1.0)

        kwin = kw_sc[pl.ds(w0, wkeys), c0:c0 + HD]
        vwin = vw_sc[pl.ds(w0, wkeys), c0:c0 + HD]
        p_w, _, _ = _masked_softmax_rows(_dot_nt(qs, kwin), win_mask)
        o_win = _bdot(p_w.astype(BF16), vwin)

        outs = []
        for g in range(NSA_G):
            h = kvh * NSA_G + g
            r = slice(g * tq, (g + 1) * tq)
            outs.append(gates[:, 3 * h:3 * h + 1] * o_cmp[r] + gates[:, 3 * h + 1:3 * h + 2] * o_sel[r]
                        + gates[:, 3 * h + 2:3 * h + 3] * o_win[r])
        o_ref[:, kvh * NSA_G * HD:(kvh + 1) * NSA_G * HD] = jnp.concatenate(outs, axis=1).astype(o_ref.dtype)


def _compress_weights(pe, w1, w2):
    w1r = w1.reshape(2, NSA_CMP_LEN, HD, NSA_CMP_HID)
    pe2 = jnp.tile(pe, (1, 1, 2)).reshape(2, NSA_CMP_LEN, 1, 2 * HD)
    return w1r.astype(BF16), pe2, w2.astype(BF16)


def _compress_prompt_kernel(src_ref, w1_ref, pe_ref, w2_ref, o_ref, stage_sc, *, n_ch):
    rows = NSA_CMP_STRIDE * n_ch
    for h in range(2):
        stage_sc[h, 0:rows, :] = src_ref[:, 128 * h:128 * (h + 1)]
        stage_sc[h, rows:rows + NSA_CMP_STRIDE, :] = jnp.zeros((NSA_CMP_STRIDE, 128), F32)
    o_ref[0] = _compress_rows(stage_sc, n_ch, w1_ref.at[0], pe_ref.at[0], w2_ref[0])


def _compress_prompt(kv4, nb, seq, cw):
    w1r, pe2, w2 = cw
    n_ch = seq // NSA_CMP_STRIDE
    return pl.pallas_call(
        functools.partial(_compress_prompt_kernel, n_ch=n_ch),
        grid=(2, nb),
        in_specs=[
            pl.BlockSpec((seq, KV_COLS), lambda s, b: (b, s)),
            pl.BlockSpec((1,) + w1r.shape[1:], lambda s, b: (s, 0, 0, 0)),
            pl.BlockSpec((1,) + pe2.shape[1:], lambda s, b: (s, 0, 0, 0)),
            pl.BlockSpec((1,) + w2.shape[1:], lambda s, b: (s, 0, 0)),
        ],
        out_specs=pl.BlockSpec((1, n_ch, KV_COLS), lambda s, b: (s, b, 0)),
        out_shape=jax.ShapeDtypeStruct((2, nb * n_ch, KV_COLS), F32),
        scratch_shapes=[pltpu.VMEM((2, seq + NSA_CMP_STRIDE, 128), F32)],
        compiler_params=_params(("arbitrary", "arbitrary"), 40),
        name="nsa_compress_prompt",
    )(kv4, w1r, pe2, w2)


def _overlap_matrix(n_cmp_pad, n_cmp, n_blk, lanes=128):
    c = jnp.arange(n_cmp_pad)[:, None] * NSA_CMP_STRIDE
    j = jnp.arange(lanes)[None, :] * NSA_SEL_LEN
    ok = (c < j + NSA_SEL_LEN) & (c + NSA_CMP_LEN > j)
    ok = ok & (jnp.arange(n_cmp_pad)[:, None] < n_cmp) & (jnp.arange(lanes)[None, :] < n_blk)
    return ok.astype(BF16)


def _expand_matrix(n_keys, chunk, lanes=128):
    key = jnp.arange(n_keys).reshape(n_keys // chunk, 1, chunk)
    j = jnp.arange(lanes).reshape(1, lanes, 1)
    return (key // NSA_SEL_LEN == j).astype(BF16)


def _nsa_prompt_attention(q, kc, vc, kv4, kvw, gates, nb, seq, tq=128, chunk=512):
    nq = seq // tq
    n_cmp = seq // NSA_CMP_STRIDE
    ov = _overlap_matrix(n_cmp, n_cmp - 1, seq // NSA_SEL_LEN)
    ex = _expand_matrix(seq, chunk)
    return pl.pallas_call(
        functools.partial(_nsa_prompt_kernel, tq=tq, seq=seq, chunk=chunk),
        grid=(nb, nq),
        in_specs=[
            pl.BlockSpec((tq, q.shape[1]), lambda b, i: (b * nq + i, 0)),
            pl.BlockSpec((n_cmp, KV_COLS), lambda b, i: (b, 0)),
            pl.BlockSpec((n_cmp, KV_COLS), lambda b, i: (b, 0)),
            pl.BlockSpec((seq, KV_COLS), lambda b, i: (b, 2)),
            pl.BlockSpec((seq, KV_COLS), lambda b, i: (b, 3)),
            pl.BlockSpec((seq, KV_COLS), lambda b, i: (b, 0)),
            pl.BlockSpec((seq, KV_COLS), lambda b, i: (b, 1)),
            pl.BlockSpec((tq, gates.shape[1]), lambda b, i: (b * nq + i, 0)),
            _const_spec(ov.shape),
            _const_spec(ex.shape),
        ],
        out_specs=pl.BlockSpec((tq, q.shape[1]), lambda b, i: (b * nq + i, 0)),
        out_shape=jax.ShapeDtypeStruct(q.shape, BF16),
        scratch_shapes=[pltpu.VMEM((seq, KV_COLS), BF16)] * 4,
        compiler_params=_params(("arbitrary", "arbitrary"), 48),
        name="nsa_prompt_attention",
    )(q, kc, vc, kv4, kv4, kvw, kvw, gates, ov, ex)


def _nsa_weights(w_in):
    nq = NSA_KVH * NSA_G * HD
    wg = jnp.pad(w_in[:, nq + 6 * KV_COLS:], ((0, 0), (0, 128 - 3 * NSA_KVH * NSA_G)))
    return [w_in[:, :nq].astype(BF16), w_in[:, nq:nq + 4 * KV_COLS].astype(BF16),
            w_in[:, nq + 4 * KV_COLS:nq + 6 * KV_COLS].astype(BF16), wg.astype(BF16)]


def _nsa_prompt(tok, x, g, nb, seq, w_in_parts, cw, w_out, name="nsa"):
    q, kv4, kvw, gates = _mod_proj(tok, x, g, w_in_parts, [F32, F32, F32, F32], name + "_proj")
    cmp_rows = _compress_prompt(kv4, nb, seq, cw)
    o = _nsa_prompt_attention(q, cmp_rows[0], cmp_rows[1], kv4, kvw, gates, nb, seq)
    x_new = _out_proj(tok, x, [o], w_out, "plain", name + "_out")
    return x_new, kv4, kvw


T_PAD = 8
PAGE_SIZE = 128
PAGES_PER_STEP = 16


def _softmax_two(s1, mask1, v1, s2, mask2, v2):
    m = jnp.maximum(jnp.max(jnp.where(mask1, s1, NEG_BIG), axis=-1, keepdims=True),
                    jnp.max(jnp.where(mask2, s2, NEG_BIG), axis=-1, keepdims=True))
    p1 = jnp.where(mask1, jnp.exp(jnp.where(mask1, s1, NEG_BIG) - m), 0.0)
    p2 = jnp.where(mask2, jnp.exp(jnp.where(mask2, s2, NEG_BIG) - m), 0.0)
    den = jnp.sum(p1, axis=-1, keepdims=True) + jnp.sum(p2, axis=-1, keepdims=True)
    acc = _bdot(p1.astype(BF16), v1) + _bdot(p2.astype(BF16), v2)
    return acc / jnp.where(den > 0, den, 1.0), m, den


def _stack_heads(q_ref, kvh):
    qs = jnp.concatenate(
        [q_ref[0, :, (kvh * NSA_G + g) * HD:(kvh * NSA_G + g + 1) * HD] for g in range(NSA_G)], axis=0)
    return (qs * (HD ** -0.5)).astype(BF16)


def _nsa_sample_cmp_kernel(pt_ref, *refs, n_pages, n_valid):
    del pt_ref
    pages = refs[:PAGES_PER_STEP]
    q_ref, w1_ref, pe_ref, w2_ref, ov_ref, ocmp_ref, sel_ref, stage_sc = refs[PAGES_PER_STEP:]
    step = pl.program_id(1)
    n_rows = n_pages * PAGE_SIZE
    n_ch = n_rows // NSA_CMP_STRIDE

    @pl.when(step == 0)
    def _():
        for h in range(4):
            stage_sc[h, n_rows:n_rows + NSA_CMP_STRIDE, :] = jnp.zeros((NSA_CMP_STRIDE, 128), F32)

    for k, page in enumerate(pages):
        r0 = pl.multiple_of((step * PAGES_PER_STEP + k) * PAGE_SIZE, PAGE_SIZE)
        for h in range(4):
            stage_sc[h, pl.ds(r0, PAGE_SIZE), :] = page[0, :, 128 * h:128 * (h + 1)]

    @pl.when(step == pl.num_programs(1) - 1)
    def _():
        kc = _compress_rows(stage_sc.at[0:2], n_ch, w1_ref.at[0], pe_ref.at[0], w2_ref[0]).astype(BF16)
        vc = _compress_rows(stage_sc.at[2:4], n_ch, w1_ref.at[1], pe_ref.at[1], w2_ref[1]).astype(BF16)
        tpos = n_rows + lax.broadcasted_iota(jnp.int32, (T_PAD, 1), 0)
        c_idx = lax.broadcasted_iota(jnp.int32, (T_PAD, n_ch), 1)
        cmp_mask = (c_idx * NSA_CMP_STRIDE + NSA_CMP_LEN - 1 <= tpos) & (c_idx < n_ch - 1)
        cmp_mask = jnp.concatenate([cmp_mask] * NSA_G, axis=0)
        lanes = sel_ref.shape[3]
        j_idx = lax.broadcasted_iota(jnp.int32, (T_PAD, lanes), 1)
        tb = tpos // NSA_SEL_LEN
        forced = (j_idx == 0) | (j_idx == tb) | (j_idx == tb - 1)
        n_blk = (n_rows + n_valid + NSA_SEL_LEN - 1) // NSA_SEL_LEN
        for kvh in range(NSA_KVH):
            c0 = kvh * HD
            qs = _stack_heads(q_ref, kvh)
            p_c, _, _ = _masked_softmax_rows(_dot_nt(qs, kc[:, c0:c0 + HD]), cmp_mask)
            o_cmp = _bdot(p_c.astype(BF16), vc[:, c0:c0 + HD])
            p_sum = p_c[0:T_PAD]
            for g in range(1, NSA_G):
                p_sum = p_sum + p_c[g * T_PAD:(g + 1) * T_PAD]
            ph, plo = _split_hi_lo(p_sum)
            imp = _bdot(ph, ov_ref[...]) + _bdot(plo, ov_ref[...])
            score = jnp.where(j_idx <= tb, imp + jnp.where(forced, NSA_FORCE_BONUS, 0.0), -jnp.inf)
            sel_ref[0, kvh] = _top_blocks(score, n_blk, NSA_TOPN)
            ocmp_ref[0, :, kvh * NSA_G * HD:(kvh + 1) * NSA_G * HD] = jnp.concatenate(
                [o_cmp[g * T_PAD:(g + 1) * T_PAD] for g in range(NSA_G)], axis=1)


def _page_specs(col_block):
    return [pl.BlockSpec((1, PAGE_SIZE, 2 * KV_COLS),
                         lambda b, s, pt, k=k: (pt[b, s * PAGES_PER_STEP + k], 0, col_block))
            for k in range(PAGES_PER_STEP)]


def _nsa_sample_cmp(q, cache, page_table, cw, n_valid):
    w1r, pe2, w2 = cw
    nb, n_pages = page_table.shape
    n_rows = n_pages * PAGE_SIZE
    n_ch = n_rows // NSA_CMP_STRIDE
    n_blk = (n_rows + n_valid + NSA_SEL_LEN - 1) // NSA_SEL_LEN
    lanes = -(-n_blk // 128) * 128
    ov = _overlap_matrix(n_ch, n_ch - 1, n_blk, lanes)
    cache3 = cache.reshape(cache.shape[0], PAGE_SIZE, 4 * KV_COLS)
    bspec = lambda shape: pl.BlockSpec(shape, lambda b, s, pt: (b,) + (0,) * (len(shape) - 1))
    cspec = lambda shape: pl.BlockSpec(shape, lambda b, s, pt: (0,) * len(shape))
    grid_spec = pltpu.PrefetchScalarGridSpec(
        num_scalar_prefetch=1,
        grid=(nb, n_pages // PAGES_PER_STEP),
        in_specs=_page_specs(0) + [bspec((1, T_PAD, q.shape[2])), cspec(w1r.shape), cspec(pe2.shape),
                                    cspec(w2.shape), cspec(ov.shape)],
        out_specs=[bspec((1, T_PAD, q.shape[2])), bspec((1, NSA_KVH, T_PAD, lanes))],
        scratch_shapes=[pltpu.VMEM((4, n_rows + NSA_CMP_STRIDE, 128), F32)],
    )
    return pl.pallas_call(
        functools.partial(_nsa_sample_cmp_kernel, n_pages=n_pages, n_valid=n_valid),
        grid_spec=grid_spec,
        out_shape=[jax.ShapeDtypeStruct(q.shape, F32), jax.ShapeDtypeStruct((nb, NSA_KVH, T_PAD, lanes), F32)],
        compiler_params=_params(("arbitrary", "arbitrary"), 56),
        name="nsa_sample_cmp",
    )(page_table, *([cache3] * PAGES_PER_STEP), q, w1r, pe2, w2, ov)


def _nsa_sample_attend_kernel(pt_ref, *refs, n_pages, n_valid):
    del pt_ref
    pages = refs[:PAGES_PER_STEP]
    (q_ref, sel_ref, ocmp_ref, new4_ref, neww_ref, win_ref, gt_ref, ex_ref, o_ref, ks_sc, vs_sc) = refs[PAGES_PER_STEP:]
    step = pl.program_id(1)
    n_rows = n_pages * PAGE_SIZE

    for k, page in enumerate(pages):
        r0 = pl.multiple_of((step * PAGES_PER_STEP + k) * PAGE_SIZE, PAGE_SIZE)
        ks_sc[pl.ds(r0, PAGE_SIZE), :] = page[0, :, 0:KV_COLS].astype(BF16)
        vs_sc[pl.ds(r0, PAGE_SIZE), :] = page[0, :, KV_COLS:2 * KV_COLS].astype(BF16)

    @pl.when(step == pl.num_programs(1) - 1)
    def _():
        def tile_g(x):
            return jnp.concatenate([x] * NSA_G, axis=0)

        lb = win_ref.shape[1]
        t_idx = lax.broadcasted_iota(jnp.int32, (T_PAD, 1), 0)
        i_new = lax.broadcasted_iota(jnp.int32, (T_PAD, T_PAD), 1)
        new_mask = tile_g((i_new <= t_idx) & (i_new < n_valid))
        w_idx = lax.broadcasted_iota(jnp.int32, (T_PAD, lb), 1)
        win_mask = tile_g(w_idx >= lb + t_idx - NSA_WIN)
        gates = jax.nn.sigmoid(gt_ref[0])
        for kvh in range(NSA_KVH):
            c0 = kvh * HD
            qs = _stack_heads(q_ref, kvh)
            chosen = tile_g(_bdot(sel_ref[0, kvh].astype(BF16), ex_ref[...]) > 0.5)
            k_new = new4_ref[0, :, 2 * KV_COLS + c0:2 * KV_COLS + c0 + HD].astype(BF16)
            v_new = new4_ref[0, :, 3 * KV_COLS + c0:3 * KV_COLS + c0 + HD].astype(BF16)
            o_sel, _, _ = _softmax_two(_dot_nt(qs, ks_sc[:, c0:c0 + HD]), chosen, vs_sc[:, c0:c0 + HD],
                                       _dot_nt(qs, k_new), new_mask, v_new)
            kw_old = win_ref[0, :, c0:c0 + HD].astype(BF16)
            vw_old = win_ref[0, :, KV_COLS + c0:KV_COLS + c0 + HD].astype(BF16)
            kw_new = neww_ref[0, :, c0:c0 + HD].astype(BF16)
            vw_new = neww_ref[0, :, KV_COLS + c0:KV_COLS + c0 + HD].astype(BF16)
            o_win, _, _ = _softmax_two(_dot_nt(qs, kw_old), win_mask, vw_old, _dot_nt(qs, kw_new), new_mask, vw_new)
            outs = []
            for g in range(NSA_G):
                h = kvh * NSA_G + g
                r = slice(g * T_PAD, (g + 1) * T_PAD)
                outs.append(gates[:, 3 * h:3 * h + 1] * ocmp_ref[0, :, h * HD:(h + 1) * HD]
                            + gates[:, 3 * h + 1:3 * h + 2] * o_sel[r] + gates[:, 3 * h + 2:3 * h + 3] * o_win[r])
            o_ref[0, :, kvh * NSA_G * HD:(kvh + 1) * NSA_G * HD] = jnp.concatenate(outs, axis=1).astype(o_ref.dtype)


def _nsa_sample_attend(q, sel, ocmp, new4, neww, win, gates, cache, page_table, n_valid):
    nb, n_pages = page_table.shape
    n_rows = n_pages * PAGE_SIZE
    lanes = sel.shape[3]
    ex = _expand_matrix(n_rows, n_rows, lanes)[0]
    cache3 = cache.reshape(cache.shape[0], PAGE_SIZE, 4 * KV_COLS)
    win3 = win.reshape(nb, win.shape[1], 2 * KV_COLS)
    bspec = lambda shape: pl.BlockSpec(shape, lambda b, s, pt: (b,) + (0,) * (len(shape) - 1))
    cspec = lambda shape: pl.BlockSpec(shape, lambda b, s, pt: (0,) * len(shape))
    grid_spec = pltpu.PrefetchScalarGridSpec(
        num_scalar_prefetch=1,
        grid=(nb, n_pages // PAGES_PER_STEP),
        in_specs=_page_specs(1) + [
            bspec((1, T_PAD, q.shape[2])), bspec((1,) + sel.shape[1:]), bspec((1, T_PAD, ocmp.shape[2])),
            bspec((1, T_PAD, new4.shape[2])), bspec((1, T_PAD, neww.shape[2])), bspec((1,) + win3.shape[1:]),
            bspec((1, T_PAD, gates.shape[2])), cspec(ex.shape)],
        out_specs=bspec((1, T_PAD, q.shape[2])),
        scratch_shapes=[pltpu.VMEM((n_rows, KV_COLS), BF16)] * 2,
    )
    return pl.pallas_call(
        functools.partial(_nsa_sample_attend_kernel, n_pages=n_pages, n_valid=n_valid),
        grid_spec=grid_spec,
        out_shape=jax.ShapeDtypeStruct(q.shape, BF16),
        compiler_params=_params(("arbitrary", "arbitrary"), 56),
        name="nsa_sample_attend",
    )(page_table, *([cache3] * PAGES_PER_STEP), q, sel, ocmp, new4, neww, win3, gates, ex)


def _nsa_sample(tok, x, g, nb, cache, win, page_table, n_valid, w_in_parts, cw, w_out, name="nsa_s"):
    q, kv4, kvw, gates = _mod_proj(tok, x, g, w_in_parts, [F32, F32, F32, F32], name + "_proj")
    r3 = lambda a: a.reshape(nb, T_PAD, a.shape[1])
    ocmp, sel = _nsa_sample_cmp(r3(q), cache, page_table, cw, n_valid)
    o = _nsa_sample_attend(r3(q), sel, ocmp, r3(kv4), r3(kvw), win, r3(gates), cache, page_table, n_valid)
    x_new = _out_proj(tok, x, [o.reshape(nb * T_PAD, o.shape[2])], w_out, "plain", name + "_out")
    return x_new, r3(kv4), r3(kvw)


DIL_PAIRS = ((128, 1), (512, 4), (2048, 16))
DIL_SLOTS = 8
DIL_COLS = DIL_SLOTS * HD


def _dil_prompt_kernel(q_ref, k_ref, v_ref, o_ref, l_ref, *, tq, ls, nback):
    t0 = pl.program_id(2) * tq
    wkeys = min(nback + tq, ls)
    w0 = pl.multiple_of(jnp.maximum(t0 - nback, 0), tq)
    tpos = t0 + lax.broadcasted_iota(jnp.int32, (tq, 1), 0)
    dist = tpos - (w0 + lax.broadcasted_iota(jnp.int32, (tq, wkeys), 1))
    mask = (dist >= 0) & (dist <= nback)
    outs, lses = [], []
    for h in range(DIL_SLOTS):
        c0 = h * HD
        qh = (q_ref[:, c0:c0 + HD] * (HD ** -0.5)).astype(BF16)
        kh = k_ref[pl.ds(w0, wkeys), c0:c0 + HD].astype(BF16)
        vh = v_ref[pl.ds(w0, wkeys), c0:c0 + HD].astype(BF16)
        p, m, den = _masked_softmax_rows(_dot_nt(qh, kh), mask)
        outs.append(_bdot(p.astype(BF16), vh))
        lses.append(jnp.broadcast_to(m + jnp.log(den), (tq, HD)))
    o_ref[...] = jnp.concatenate(outs, axis=1)
    l_ref[...] = jnp.concatenate(lses, axis=1)


def _dil_prompt_group(q3, kv, gi, nb, seq, tq=128):
    window, dil = DIL_PAIRS[gi]
    ls = seq // dil
    nq = ls // tq
    qv = q3.reshape(nb * ls, dil * 3 * DIL_COLS)
    kvv = kv.reshape(nb * ls, dil * 2 * DIL_COLS)
    out_sd = jax.ShapeDtypeStruct((nb * ls, dil * DIL_COLS), F32)
    o, lse = pl.pallas_call(
        functools.partial(_dil_prompt_kernel, tq=tq, ls=ls, nback=window // dil),
        grid=(nb, dil, nq),
        in_specs=[
            pl.BlockSpec((tq, DIL_COLS), lambda b, r, i: (b * nq + i, 3 * r + gi)),
            pl.BlockSpec((ls, DIL_COLS), lambda b, r, i: (b, 2 * r)),
            pl.BlockSpec((ls, DIL_COLS), lambda b, r, i: (b, 2 * r + 1)),
        ],
        out_specs=[pl.BlockSpec((tq, DIL_COLS), lambda b, r, i: (b * nq + i, r))] * 2,
        out_shape=[out_sd, out_sd],
        compiler_params=_params(("arbitrary", "arbitrary", "arbitrary"), 40),
        name=f"dil_prompt_g{gi}",
    )(qv, kvv, kvv)
    return o.reshape(nb * seq, DIL_COLS), lse.reshape(nb * seq, DIL_COLS)


def _dil_weights(w_in):
    n_g = len(DIL_PAIRS)
    w = w_in.reshape(D_MODEL, 3, n_g, DIL_COLS)
    parts = [w[:, 0].reshape(D_MODEL, n_g * DIL_COLS)]
    for gi in range(n_g):
        parts.append(jnp.concatenate([w[:, 1, gi], w[:, 2, gi]], axis=1))
    return [p.astype(BF16) for p in parts]


def _dil_prompt(tok, x, g, nb, seq, w_in_parts, w_out, name="dil"):
    q3, kv0, kv1, kv2 = _mod_proj(tok, x, g, w_in_parts, [F32] * 4, name + "_proj")
    kvs = (kv0, kv1, kv2)
    outs, lses = [], []
    for gi in range(len(DIL_PAIRS)):
        o, lse = _dil_prompt_group(q3, kvs[gi], gi, nb, seq)
        outs.append(o)
        lses.append(lse)
    x_new = _out_proj(tok, x, outs + lses, w_out, "dil", name + "_out")
    return x_new, kvs


def _dil_sample_kernel(q_ref, st_ref, new_ref, o_ref, l_ref, *, window, dil, n_valid):
    lb = st_ref.shape[1]
    t_idx = lax.broadcasted_iota(jnp.int32, (T_PAD, 1), 0)
    d_old = lb + t_idx - lax.broadcasted_iota(jnp.int32, (T_PAD, lb), 1)
    old_mask = ((d_old & (dil - 1)) == 0) & (d_old <= window)
    i_new = lax.broadcasted_iota(jnp.int32, (T_PAD, T_PAD), 1)
    d_new = t_idx - i_new
    new_mask = (d_new >= 0) & ((d_new & (dil - 1)) == 0) & (i_new < n_valid)
    outs, lses = [], []
    for h in range(DIL_SLOTS):
        c0 = h * HD
        qh = (q_ref[0, :, c0:c0 + HD] * (HD ** -0.5)).astype(BF16)
        k_old = st_ref[0, :, c0:c0 + HD].astype(BF16)
        v_old = st_ref[0, :, DIL_COLS + c0:DIL_COLS + c0 + HD].astype(BF16)
        k_new = new_ref[0, :, c0:c0 + HD].astype(BF16)
        v_new = new_ref[0, :, DIL_COLS + c0:DIL_COLS + c0 + HD].astype(BF16)
        o, m, den = _softmax_two(_dot_nt(qh, k_old), old_mask, v_old, _dot_nt(qh, k_new), new_mask, v_new)
        outs.append(o)
        lses.append(jnp.broadcast_to(m + jnp.log(den), (T_PAD, HD)))
    o_ref[0] = jnp.concatenate(outs, axis=1)
    l_ref[0] = jnp.concatenate(lses, axis=1)


def _dil_sample_group(q3, kv_new, state, gi, n_valid):
    window, dil = DIL_PAIRS[gi]
    nb, lb = state.shape[:2]
    st = state.reshape(nb, lb, 2 * DIL_COLS)
    out_sd = jax.ShapeDtypeStruct((nb, T_PAD, DIL_COLS), F32)
    return pl.pallas_call(
        functools.partial(_dil_sample_kernel, window=window, dil=dil, n_valid=n_valid),
        grid=(nb,),
        in_specs=[
            pl.BlockSpec((1, T_PAD, DIL_COLS), lambda b: (b, 0, gi)),
            pl.BlockSpec((1, lb, 2 * DIL_COLS), lambda b: (b, 0, 0)),
            pl.BlockSpec((1, T_PAD, 2 * DIL_COLS), lambda b: (b, 0, 0)),
        ],
        out_specs=[pl.BlockSpec((1, T_PAD, DIL_COLS), lambda b: (b, 0, 0))] * 2,
        out_shape=[out_sd, out_sd],
        compiler_params=_params(("arbitrary",), 40),
        name=f"dil_sample_g{gi}",
    )(q3, st, kv_new)


def _dil_sample(tok, x, g, nb, states, n_valid, w_in_parts, w_out, name="dil_s"):
    q3, kv0, kv1, kv2 = _mod_proj(tok, x, g, w_in_parts, [F32] * 4, name + "_proj")
    r3 = lambda a: a.reshape(nb, T_PAD, a.shape[1])
    kvs = (r3(kv0), r3(kv1), r3(kv2))
    outs, lses = [], []
    for gi in range(len(DIL_PAIRS)):
        o, lse = _dil_sample_group(r3(q3), kvs[gi], states[gi], gi, n_valid)
        outs.append(o.reshape(nb * T_PAD, DIL_COLS))
        lses.append(lse.reshape(nb * T_PAD, DIL_COLS))
    x_new = _out_proj(tok, x, outs + lses, w_out, "dil", name + "_out")
    return x_new, kvs


GLA_H, GLA_DK, GLA_DV, GLA_RANK, GLA_TAU = 4, 128, 256, 16, 16.0
HG_H, HG_DK, HG_DV = 8, 128, 128
SCAN_SUB = 16


def _log_sigmoid(x):
    return jnp.minimum(x, 0.0) - jnp.log1p(jnp.exp(-jnp.abs(x)))


def _cumsum_rows(g):
    n = g.shape[0]
    tri = (lax.broadcasted_iota(jnp.int32, (n, n), 0) >= lax.broadcasted_iota(jnp.int32, (n, n), 1)).astype(BF16)
    hi = g.astype(BF16)
    r1 = g - hi.astype(F32)
    mid = r1.astype(BF16)
    lo = (r1 - mid.astype(F32)).astype(BF16)
    return _bdot(tri, hi) + _bdot(tri, mid) + _bdot(tri, lo)


def _scan_chunk(q, k, g, v, st):
    n, dk = q.shape
    sub = min(SCAN_SUB, n)
    b = _cumsum_rows(g)
    b_end = b[n - 1:n]
    o = _dot_nt((q * jnp.exp(b)).astype(BF16), st.astype(BF16))

    lane = lax.broadcasted_iota(jnp.int32, (sub, n), 1)
    row = lax.broadcasted_iota(jnp.int32, (sub, n), 0)
    ones = jnp.ones((dk, n), BF16)
    a_rows = []
    for i in range(n // sub):
        lo = i * sub
        qi, ki, bi = q[lo:lo + sub], k[lo:lo + sub], b[lo:lo + sub]
        prods = []
        for s in range(sub):
            e = jnp.exp(jnp.minimum(bi - bi[s:s + 1], 0.0))
            prods.append((qi * ki[s:s + 1] * e).astype(BF16))
        sums = _bdot(jnp.concatenate(prods, axis=0), ones)
        a_i = jnp.zeros((sub, n), F32)
        for s in range(sub):
            a_i = a_i + jnp.where((lane == lo + s) & (row >= s), sums[s * sub:(s + 1) * sub], 0.0)
        if i > 0:
            b_ref = b[lo - 1:lo]
            qd = (qi * jnp.exp(bi - b_ref)).astype(BF16)
            kd = (k[0:lo] * jnp.exp(b_ref - b[0:lo])).astype(BF16)
            if lo < n:
                kd = jnp.concatenate([kd, jnp.zeros((n - lo, dk), BF16)], axis=0)
            a_i = a_i + _dot_nt(qd, kd)
        a_rows.append(a_i)
    a = jnp.concatenate(a_rows, axis=0) if len(a_rows) > 1 else a_rows[0]
    o = o + _bdot(a.astype(BF16), v.astype(BF16))
    kd_end = (k * jnp.exp(b_end - b)).astype(BF16)
    st_new = st * jnp.exp(b_end) + _dot_tn(v.astype(BF16), kd_end)
    return o, st_new


def _scan_kernel(*refs, kind, chunk, n_chunks, n_valid):
    if kind == "gla":
        q_ref, k_ref, v_ref, a_ref, wa_ref, ba_ref, s0_ref, o_ref, sf_ref, st_sc = refs
    else:
        f_ref, v_ref, q_ref, lb_ref, s0_ref, o_ref, sf_ref, st_sc = refs
    ci = pl.program_id(2)

    @pl.when(ci == 0)
    def _():
        st_sc[...] = s0_ref[0, 0]

    def body(j, carry):
        r0 = pl.multiple_of(j * chunk, chunk)
        rows = pl.ds(r0, chunk)
        if kind == "gla":
            q = q_ref[rows, :] * (GLA_DK ** -0.5)
            k = k_ref[rows, :]
            pre = _bdot(a_ref[rows, :].astype(BF16), wa_ref[...]) + ba_ref[...]
            g = _log_sigmoid(pre) * (1.0 / GLA_TAU)
        else:
            fz = f_ref[rows, :]
            lb = lb_ref[...]
            q = _silu(q_ref[rows, :])
            la = jnp.log(lb)
            lc = jnp.log1p(-lb) + _log_sigmoid(fz)
            g = jnp.maximum(la, lc) + jnp.log1p(jnp.exp(-jnp.abs(la - lc)))
            k = (1.0 - lb) * jax.nn.sigmoid(-fz)
        if n_valid < chunk:
            live = lax.broadcasted_iota(jnp.int32, k.shape, 0) < n_valid
            k = jnp.where(live, k, 0.0)
            g = jnp.where(live, g, 0.0)
        o, st = _scan_chunk(q, k, g, v_ref[rows, :], st_sc[...])
        st_sc[...] = st
        o_ref[rows, :] = o
        return carry

    lax.fori_loop(0, n_chunks, body, 0)

    @pl.when(ci == pl.num_programs(2) - 1)
    def _():
        sf_ref[0, 0] = st_sc[...]


def _scan(kind, ins, s0_t, nb, rows_per_b, n_valid, name):
    if kind == "gla":
        heads, dk, dv = GLA_H, GLA_DK, GLA_DV
    else:
        heads, dk, dv = HG_H, HG_DK, HG_DV
    chunk = min(64, rows_per_b)
    blk = min(512, rows_per_b)
    nblk = rows_per_b // blk
    rspec = lambda w: pl.BlockSpec((blk, w), lambda b, h, i: (b * nblk + i, h))
    if kind == "gla":
        q, k, v, a, wa, ba = ins
        args = [q, k, v, a, wa, ba]
        in_specs = [rspec(dk), rspec(dk), rspec(dv),
                    pl.BlockSpec((blk, a.shape[1]), lambda b, h, i: (b * nblk + i, 0)),
                    pl.BlockSpec((wa.shape[0], dk), lambda b, h, i: (0, h)),
                    pl.BlockSpec((1, dk), lambda b, h, i: (0, h))]
    else:
        f, v, q, lb = ins
        args = [f, v, q, lb]
        in_specs = [rspec(dk), rspec(dv), rspec(dk), pl.BlockSpec((1, dk), lambda b, h, i: (0, h))]
    m = nb * rows_per_b
    st_spec = pl.BlockSpec((1, 1, dv, dk), lambda b, h, i: (b, h, 0, 0))
    return pl.pallas_call(
        functools.partial(_scan_kernel, kind=kind, chunk=chunk, n_chunks=blk // chunk, n_valid=n_valid),
        grid=(nb, heads, nblk),
        in_specs=in_specs + [st_spec],
        out_specs=[rspec(dv), st_spec],
        out_shape=[jax.ShapeDtypeStruct((m, heads * dv), F32), jax.ShapeDtypeStruct((nb, heads, dv, dk), F32)],
        scratch_shapes=[pltpu.VMEM((dv, dk), F32)],
        compiler_params=_params(("arbitrary", "arbitrary", "arbitrary"), 40),
        name=name,
    )(*args, s0_t)


def _gla_weights(w_in, w_a2, b_a2):
    nk, nv = GLA_H * GLA_DK, GLA_H * GLA_DV
    wa = jnp.pad(w_in[:, 2 * nk + 2 * nv:], ((0, 0), (0, 128 - GLA_RANK)))
    parts = [w_in[:, :nk], w_in[:, nk:2 * nk], w_in[:, 2 * nk:2 * nk + nv], w_in[:, 2 * nk + nv:2 * nk + 2 * nv], wa]
    wa2 = jnp.pad(w_a2, ((0, 128 - GLA_RANK), (0, 0))).astype(BF16)
    return [p.astype(BF16) for p in parts], wa2, b_a2.reshape(1, nk)


def _gla(tok, x, g, s0_t, nb, rows_per_b, n_valid, wts, norm_g, w_out, name="gla"):
    parts, wa2, ba2 = wts
    q, k, v, r, a = _mod_proj(tok, x, g, parts, [F32] * 5, name + "_proj")
    o, s_t = _scan("gla", [q, k, v, a, wa2, ba2], s0_t, nb, rows_per_b, n_valid, name + "_scan")
    x_new = _out_proj(tok, x, [o, r, norm_g.reshape(1, GLA_DV)], w_out, "heads", name + "_out", heads=GLA_H)
    return x_new, s_t


def _hgrn_weights(w_in, lb_logits, layer):
    nk, nv = HG_H * HG_DK, HG_H * HG_DV
    parts = [w_in[:, :nk], w_in[:, nk:nk + nv], w_in[:, nk + nv:2 * nk + nv], w_in[:, 2 * nk + nv:]]
    sm = jax.nn.softmax(lb_logits.astype(F32), axis=0)
    lb = jnp.sum(sm[1:layer + 1], axis=0).reshape(1, nk)
    return [p.astype(BF16) for p in parts], lb


def _hgrn(tok, x, g, s0_t, nb, rows_per_b, n_valid, wts, norm_g, w_out, name="hgrn"):
    parts, lb = wts
    f, i_in, q, og = _mod_proj(tok, x, g, parts, [F32] * 4, name + "_proj")
    o, s_t = _scan("hgrn", [f, i_in, q, lb], s0_t, nb, rows_per_b, n_valid, name + "_scan")
    x_new = _out_proj(tok, x, [o, og, norm_g.reshape(1, HG_DV)], w_out, "heads", name + "_out", heads=HG_H)
    return x_new, s_t


PROMPT_TM = 256
PROMPT_MOE_TM = 512


def _moe_weights(w_rg, b_rg, w_re, b_re, w_up, w_down):
    n_e = MOE_GROUPS * MOE_PER_GROUP
    wr = jnp.zeros((D_MODEL, ROUTER_LANES), F32).at[:, :MOE_GROUPS].set(w_rg).at[:, MOE_GROUPS:MOE_GROUPS + n_e].set(w_re)
    br = jnp.zeros((1, ROUTER_LANES), F32).at[0, :MOE_GROUPS].set(b_rg).at[0, MOE_GROUPS:MOE_GROUPS + n_e].set(b_re)
    wr_hi, wr_lo = _split_hi_lo(wr)
    up = w_up.reshape(MOE_GROUPS, MOE_PER_GROUP, D_MODEL, 2 * MOE_FF).transpose(0, 2, 1, 3)
    up = up.reshape(MOE_GROUPS, D_MODEL, MOE_PER_GROUP * 2 * MOE_FF).astype(BF16)
    dn = w_down.reshape(MOE_GROUPS, MOE_PER_GROUP * MOE_FF, D_MODEL).astype(BF16)
    return wr_hi, wr_lo, br, up, dn


def kernel(x_prompt, x_sample, cache_nsa_kv, state_nsa_win, state_dil_0, state_dil_1, state_dil_2, state_gla, state_hgrn, page_table, c_prompt, c_sample, nsa_w_in, nsa_cmp_pe, nsa_cmp_w1, nsa_cmp_w2, nsa_w_out, dil_w_in, dil_w_out, gla_w_in, gla_w_a2, gla_b_a2, gla_norm_g, gla_w_out, hg_w_in, hg_lb_logits, hg_norm_g, hg_w_out, norm_g, ada_w, ada_b, moe_w_rg, moe_b_rg, moe_w_re, moe_b_re, moe_w_up, moe_w_down, final_norm_g):
    nb, seq, _ = x_prompt.shape
    ndb, n_t, _ = x_sample.shape
    depth = ada_w.shape[0]
    ms_rows = ndb * T_PAD

    c_rows = -(-(nb + ndb) // 16) * 16
    c_all = jnp.zeros((c_rows, D_MODEL), F32).at[:nb].set(c_prompt).at[nb:nb + ndb].set(c_sample)
    mod = _adaln(c_all, ada_w, ada_b)

    xp = x_prompt.reshape(nb * seq, D_MODEL)
    xs = jnp.pad(x_sample, ((0, 0), (0, T_PAD - n_t), (0, 0))).reshape(ms_rows, D_MODEL)

    outs = {}
    for i in range(depth):
        mp = mod[i, :nb].reshape(nb, ADA_N, D_MODEL).transpose(1, 0, 2).reshape(ADA_N * nb, 1, D_MODEL)
        ms = mod[i, nb:nb + ndb].reshape(ndb, ADA_N, D_MODEL).transpose(1, 0, 2)
        ms = jnp.repeat(ms, T_PAD, axis=1)
        tok_p = _Tokens(mp, False, nb, seq, PROMPT_TM)
        tok_pm = _Tokens(mp, False, nb, seq, PROMPT_MOE_TM)
        tok_s = _Tokens(ms, True, 1, ms_rows, ms_rows)
        g_mix, g_moe = norm_g[i, 0], norm_g[i, 1]
        kind = i % 4
        if kind == 0:
            parts = _nsa_weights(nsa_w_in)
            cw = _compress_weights(nsa_cmp_pe, nsa_cmp_w1, nsa_cmp_w2)
            w_out = nsa_w_out.astype(BF16)
            xp, kv4, kvw = _nsa_prompt(tok_p, xp, g_mix, nb, seq, parts, cw, w_out)
            xs, kv4_s, kvw_s = _nsa_sample(tok_s, xs, g_mix, ndb, cache_nsa_kv, state_nsa_win, page_table, n_t,
                                           parts, cw, w_out)
            keep = min(NSA_WIN, seq)
            outs["nsa_kv_p"] = kv4.reshape(nb, seq, 4, NSA_KVH, HD)
            outs["nsa_kv_s"] = kv4_s[:, :n_t].reshape(ndb, n_t, 4, NSA_KVH, HD).astype(cache_nsa_kv.dtype)
            outs["nsa_win_p"] = kvw.reshape(nb, seq, 2, NSA_KVH, HD)[:, seq - keep:]
            win_all = jnp.concatenate(
                [state_nsa_win, kvw_s[:, :n_t].reshape(ndb, n_t, 2, NSA_KVH, HD).astype(state_nsa_win.dtype)], axis=1)
            outs["nsa_win_s"] = win_all[:, win_all.shape[1] - min(NSA_WIN, win_all.shape[1]):]
        elif kind == 1:
            parts = _dil_weights(dil_w_in)
            w_out = dil_w_out.astype(BF16)
            states = (state_dil_0, state_dil_1, state_dil_2)
            xp, kvs = _dil_prompt(tok_p, xp, g_mix, nb, seq, parts, w_out)
            xs, kvs_s = _dil_sample(tok_s, xs, g_mix, ndb, states, n_t, parts, w_out)
            for gi, (window, _) in enumerate(DIL_PAIRS):
                buf = kvs[gi].reshape(nb, seq, 2, DIL_SLOTS, HD)
                outs[f"dil_p{gi}"] = buf[:, seq - min(window, seq):]
                new = kvs_s[gi][:, :n_t].reshape(ndb, n_t, 2, DIL_SLOTS, HD).astype(states[gi].dtype)
                kv_all = jnp.concatenate([states[gi], new], axis=1)
                outs[f"dil_s{gi}"] = kv_all[:, kv_all.shape[1] - min(window, kv_all.shape[1]):]
        elif kind == 2:
            wts = _gla_weights(gla_w_in, gla_w_a2, gla_b_a2)
            w_out = gla_w_out.astype(BF16)
            zero = jnp.zeros((nb, GLA_H, GLA_DV, GLA_DK), F32)
            xp, s_p = _gla(tok_p, xp, g_mix, zero, nb, seq, seq, wts, gla_norm_g, w_out)
            xs, s_s = _gla(tok_s, xs, g_mix, state_gla.astype(F32).transpose(0, 1, 3, 2), ndb, T_PAD, n_t, wts,
                           gla_norm_g, w_out, name="gla_s")
            outs["gla_p"] = s_p.transpose(0, 1, 3, 2).astype(state_gla.dtype)
            outs["gla_s"] = s_s.transpose(0, 1, 3, 2).astype(state_gla.dtype)
        else:
            wts = _hgrn_weights(hg_w_in, hg_lb_logits, i)
            w_out = hg_w_out.astype(BF16)
            zero = jnp.zeros((nb, HG_H, HG_DV, HG_DK), F32)
            xp, s_p = _hgrn(tok_p, xp, g_mix, zero, nb, seq, seq, wts, hg_norm_g, w_out)
            xs, s_s = _hgrn(tok_s, xs, g_mix, state_hgrn.astype(F32).transpose(0, 1, 3, 2), ndb, T_PAD, n_t, wts,
                            hg_norm_g, w_out, name="hgrn_s")
            outs["hg_p"] = s_p.transpose(0, 1, 3, 2).astype(state_hgrn.dtype)
            outs["hg_s"] = s_s.transpose(0, 1, 3, 2).astype(state_hgrn.dtype)
        mw = _moe_weights(moe_w_rg[i], moe_b_rg[i], moe_w_re[i], moe_b_re[i], moe_w_up[i], moe_w_down[i])
        fg = final_norm_g if i == depth - 1 else None
        xp = _moe(tok_pm, xp, g_moe, *mw, fg, f"moe_p{i}")
        xs = _moe(tok_s, xs, g_moe, *mw, fg, f"moe_s{i}")

    y_prompt = xp.reshape(nb, seq, D_MODEL)
    y_sample = xs.reshape(ndb, T_PAD, D_MODEL)[:, :n_t]
    return (y_prompt, y_sample, outs["nsa_kv_p"], outs["nsa_kv_s"], outs["nsa_win_p"], outs["nsa_win_s"],
            outs["dil_p0"], outs["dil_s0"], outs["dil_p1"], outs["dil_s1"], outs["dil_p2"], outs["dil_s2"],
            outs["gla_p"], outs["gla_s"], outs["hg_p"], outs["hg_s"])
```

```python
import functools

import jax
import jax.numpy as jnp
from jax import lax
from jax.experimental import pallas as pl
from jax.experimental.pallas import tpu as pltpu

F32 = jnp.float32
BF16 = jnp.bfloat16
EPS = 1e-6
D_MODEL = 1024
HD = 64
ADA_N = 6
MIB = 1024 * 1024
NEG_BIG = -1e30
MASK_BIAS = -2e30


def _params(sem, vmem_mib):
    return pltpu.CompilerParams(dimension_semantics=sem, vmem_limit_bytes=vmem_mib * MIB)


def _silu(x):
    return x * jax.nn.sigmoid(x)


def _bdot(a, b):
    return jnp.dot(a, b, preferred_element_type=F32)


def _dot_nt(a, b):
    return lax.dot_general(a, b, (((1,), (1,)), ((), ())), preferred_element_type=F32)


def _dot_tn(a, b):
    return lax.dot_general(a, b, (((0,), (0,)), ((), ())), preferred_element_type=F32)


def _split_hi_lo(x):
    hi = x.astype(BF16)
    lo = (x - hi.astype(F32)).astype(BF16)
    return hi, lo


def _rms(x):
    return x * lax.rsqrt(jnp.mean(x * x, axis=-1, keepdims=True) + EPS)


class _Tokens:
    def __init__(self, mod, per_row, nb, rows_per_b, tm):
        self.mod = mod
        self.per_row = per_row
        self.nb = nb
        self.rows_per_b = rows_per_b
        self.tm = tm

    def mod_spec(self, k):
        tm = self.tm
        if self.per_row:
            return pl.BlockSpec((1, tm, D_MODEL), lambda i, *_: (k, i, 0))
        nb, rpb = self.nb, self.rows_per_b
        return pl.BlockSpec((1, 1, D_MODEL), lambda i, *_: (k * nb + (i * tm) // rpb, 0, 0))


def _row_spec(tm, n):
    return pl.BlockSpec((tm, n), lambda i, *_: (i, 0))


def _const_spec(shape):
    nd = len(shape)
    return pl.BlockSpec(shape, lambda *_: (0,) * nd)


def _adaln_kernel(c_ref, w_ref, b_ref, o_ref):
    x = _silu(c_ref[...]).astype(BF16)
    o_ref[0] = _bdot(x, w_ref[0].astype(BF16)) + b_ref[0]


def _adaln(c_all, ada_w, ada_b):
    depth, _, n = ada_w.shape
    rows = c_all.shape[0]
    tn = 1536
    return pl.pallas_call(
        _adaln_kernel,
        grid=(depth, n // tn),
        in_specs=[
            pl.BlockSpec((rows, D_MODEL), lambda l, j: (0, 0)),
            pl.BlockSpec((1, D_MODEL, tn), lambda l, j: (l, 0, j)),
            pl.BlockSpec((1, 1, tn), lambda l, j: (l, 0, j)),
        ],
        out_specs=pl.BlockSpec((1, rows, tn), lambda l, j: (l, 0, j)),
        out_shape=jax.ShapeDtypeStruct((depth, rows, n), F32),
        compiler_params=_params(("arbitrary", "arbitrary"), 40),
        name="adaln",
    )(c_all, ada_w, ada_b.reshape(depth, 1, n))


def _modulate(x, g, shift, scale):
    return _rms(x) * g * (1.0 + scale) + shift


def _mod_proj_kernel(x_ref, g_ref, sh_ref, sc_ref, *refs, n_out):
    w_refs, o_refs = refs[:n_out], refs[n_out:]
    h = _modulate(x_ref[...], g_ref[...], sh_ref[0], sc_ref[0]).astype(BF16)
    for w_ref, o_ref in zip(w_refs, o_refs):
        o_ref[...] = _bdot(h, w_ref[...]).astype(o_ref.dtype)


def _mod_proj(tok, x, g, ws, dtypes, name):
    m = x.shape[0]
    tm = tok.tm
    return pl.pallas_call(
        functools.partial(_mod_proj_kernel, n_out=len(ws)),
        grid=(m // tm,),
        in_specs=[_row_spec(tm, D_MODEL), _const_spec((1, D_MODEL)), tok.mod_spec(0), tok.mod_spec(1)]
        + [_const_spec(w.shape) for w in ws],
        out_specs=[_row_spec(tm, w.shape[1]) for w in ws],
        out_shape=[jax.ShapeDtypeStruct((m, w.shape[1]), dt) for w, dt in zip(ws, dtypes)],
        compiler_params=_params(("arbitrary",), 56),
        name=name,
    )(x, g.reshape(1, D_MODEL), tok.mod, tok.mod, *ws)


def _out_proj_kernel(x_ref, gt_ref, *refs, mode, heads):
    if mode == "plain":
        a_ref, w_ref, o_ref = refs
        a = a_ref[...].astype(BF16)
    elif mode == "dil":
        o0, o1, o2, l0, l1, l2, w_ref, o_ref = refs
        la, lb, lc = l0[...], l1[...], l2[...]
        mx = jnp.maximum(jnp.maximum(la, lb), lc)
        wa, wb, wc = jnp.exp(la - mx), jnp.exp(lb - mx), jnp.exp(lc - mx)
        a = ((wa * o0[...] + wb * o1[...] + wc * o2[...]) / (wa + wb + wc)).astype(BF16)
    else:
        s_ref, r_ref, ng_ref, w_ref, o_ref = refs
        dv = s_ref.shape[1] // heads
        parts = []
        for h in range(heads):
            parts.append(_rms(s_ref[:, h * dv:(h + 1) * dv]) * ng_ref[...])
        a = (jnp.concatenate(parts, axis=1) * _silu(r_ref[...])).astype(BF16)
    o_ref[...] = x_ref[...] + gt_ref[0] * _bdot(a, w_ref[...])


def _out_proj(tok, x, ins, w, mode, name, heads=1, col_blocks=None):
    m = x.shape[0]
    tm = tok.tm
    in_specs = [_row_spec(tm, D_MODEL), tok.mod_spec(2)]
    for k, a in enumerate(ins):
        if a.shape[0] != m:
            in_specs.append(_const_spec(a.shape))
        elif col_blocks and col_blocks[k] is not None:
            width, blk = col_blocks[k]
            in_specs.append(pl.BlockSpec((tm, width), lambda i, blk=blk: (i, blk)))
        else:
            in_specs.append(_row_spec(tm, a.shape[1]))
    in_specs.append(_const_spec(w.shape))
    return pl.pallas_call(
        functools.partial(_out_proj_kernel, mode=mode, heads=heads),
        grid=(m // tm,),
        in_specs=in_specs,
        out_specs=_row_spec(tm, D_MODEL),
        out_shape=jax.ShapeDtypeStruct((m, D_MODEL), F32),
        compiler_params=_params(("arbitrary",), 48),
        name=name,
    )(x, tok.mod, *ins, w)


MOE_GROUPS = 4
MOE_PER_GROUP = 4
MOE_FF = 256
ROUTER_LANES = 128


def _router_gates(logits):
    lane = lax.broadcasted_iota(jnp.int32, logits.shape, 1)
    lane_f = lane.astype(F32)
    neg = -jnp.inf
    glog = jnp.where(lane < MOE_GROUPS, logits, neg)
    gmax = jnp.max(glog, axis=-1, keepdims=True)
    grp = jnp.min(jnp.where(glog == gmax, lane_f, 1e9), axis=-1, keepdims=True)
    p_grp = 1.0 / jnp.sum(jnp.exp(glog - gmax), axis=-1, keepdims=True)
    e_grp = ((lane - MOE_GROUPS) >> 2).astype(F32)
    n_e = MOE_GROUPS * MOE_PER_GROUP
    in_grp = (lane >= MOE_GROUPS) & (lane < MOE_GROUPS + n_e) & (e_grp == grp)
    e_in = jnp.where(in_grp, logits, neg)
    v1 = jnp.max(e_in, axis=-1, keepdims=True)
    i1 = jnp.min(jnp.where(e_in == v1, lane_f, 1e9), axis=-1, keepdims=True)
    e2 = jnp.where(lane_f == i1, neg, e_in)
    v2 = jnp.max(e2, axis=-1, keepdims=True)
    i2 = jnp.min(jnp.where(e2 == v2, lane_f, 1e9), axis=-1, keepdims=True)
    t = jnp.exp(v2 - v1)
    w1 = p_grp / (1.0 + t)
    w2 = p_grp * t / (1.0 + t)
    return jnp.where(lane_f == i1, w1, 0.0) + jnp.where(lane_f == i2, w2, 0.0)


def _moe_kernel(x_ref, g_ref, sh_ref, sc_ref, gt_ref, wrh_ref, wrl_ref, br_ref, up_ref, dn_ref, *rest, final):
    if final:
        fg_ref, o_ref, h_sc, gate_sc, acc_sc = rest
    else:
        o_ref, h_sc, gate_sc, acc_sc = rest
    grp = pl.program_id(1)

    @pl.when(grp == 0)
    def _():
        h = _modulate(x_ref[...], g_ref[...], sh_ref[0], sc_ref[0])
        hh, hl = _split_hi_lo(h)
        h_sc[...] = hh
        logits = _bdot(hh, wrh_ref[...]) + _bdot(hl, wrh_ref[...]) + _bdot(hh, wrl_ref[...]) + br_ref[...]
        gate_sc[...] = _router_gates(logits)
        acc_sc[...] = jnp.zeros_like(acc_sc)

    hid = _bdot(h_sc[...], up_ref[0])
    gates = gate_sc[...]
    lane = lax.broadcasted_iota(jnp.int32, gates.shape, 1)
    acts = []
    for e in range(MOE_PER_GROUP):
        col = jnp.sum(jnp.where(lane == MOE_GROUPS + MOE_PER_GROUP * grp + e, gates, 0.0), axis=-1, keepdims=True)
        a = hid[:, 2 * MOE_FF * e:2 * MOE_FF * e + MOE_FF]
        b = hid[:, 2 * MOE_FF * e + MOE_FF:2 * MOE_FF * (e + 1)]
        acts.append((_silu(a) * b * col).astype(BF16))
    acc_sc[...] += _bdot(jnp.concatenate(acts, axis=1), dn_ref[0])

    @pl.when(grp == MOE_GROUPS - 1)
    def _():
        y = x_ref[...] + gt_ref[0] * acc_sc[...]
        if final:
            y = _rms(y) * fg_ref[...]
        o_ref[...] = y


def _moe(tok, x, g, wr_hi, wr_lo, br, up, dn, final_g, name):
    m = x.shape[0]
    tm = tok.tm
    final = final_g is not None
    ins = [x, g.reshape(1, D_MODEL), tok.mod, tok.mod, tok.mod, wr_hi, wr_lo, br, up, dn]
    in_specs = [
        _row_spec(tm, D_MODEL), _const_spec((1, D_MODEL)), tok.mod_spec(3), tok.mod_spec(4), tok.mod_spec(5),
        _const_spec(wr_hi.shape), _const_spec(wr_lo.shape), _const_spec(br.shape),
        pl.BlockSpec((1,) + up.shape[1:], lambda i, e: (e, 0, 0)),
        pl.BlockSpec((1,) + dn.shape[1:], lambda i, e: (e, 0, 0)),
    ]
    if final:
        ins.append(final_g.reshape(1, D_MODEL))
        in_specs.append(_const_spec((1, D_MODEL)))
    return pl.pallas_call(
        functools.partial(_moe_kernel, final=final),
        grid=(m // tm, MOE_GROUPS),
        in_specs=in_specs,
        out_specs=_row_spec(tm, D_MODEL),
        out_shape=jax.ShapeDtypeStruct((m, D_MODEL), F32),
        scratch_shapes=[
            pltpu.VMEM((tm, D_MODEL), BF16),
            pltpu.VMEM((tm, ROUTER_LANES), F32),
            pltpu.VMEM((tm, D_MODEL), F32),
        ],
        compiler_params=_params(("arbitrary", "arbitrary"), 48),
        name=name,
    )(*ins)


NSA_KVH = 4
NSA_G = 4
NSA_CMP_LEN = 32
NSA_CMP_STRIDE = 16
NSA_CMP_HID = 128
NSA_SEL_LEN = 64
NSA_TOPN = 16
NSA_WIN = 512
NSA_FORCE_BONUS = 1000.0
KV_COLS = NSA_KVH * HD


def _compress_rows(src_ref, n_ch, w1_ref, pe_ref, w2_ref):
    hid = [jnp.zeros((n_ch, NSA_CMP_HID), F32) for _ in range(NSA_KVH)]
    for r in range(NSA_CMP_LEN):
        halves = [(src_ref[h, pl.ds(r, n_ch, stride=NSA_CMP_STRIDE), :] + pe_ref[r]).astype(BF16) for h in range(2)]
        for kvh in range(NSA_KVH):
            c0 = (kvh % 2) * HD
            hid[kvh] = hid[kvh] + _bdot(halves[kvh // 2][:, c0:c0 + HD], w1_ref[r])
    outs = [_bdot(_silu(h).astype(BF16), w2_ref[...]) for h in hid]
    return jnp.concatenate(outs, axis=1)


def _masked_softmax_rows(s, mask):
    sm = jnp.where(mask, s, NEG_BIG)
    m = jnp.max(sm, axis=-1, keepdims=True)
    e = jnp.where(mask, jnp.exp(sm - m), 0.0)
    den = jnp.sum(e, axis=-1, keepdims=True)
    return e / jnp.where(den > 0, den, 1.0), m, den


def _top_blocks(score, n_valid, topn):
    lane = lax.broadcasted_iota(jnp.int32, score.shape, 1)
    rank = jnp.zeros(score.shape, F32)
    for i in range(n_valid):
        si = score[:, i:i + 1]
        ahead = (si > score) | ((si == score) & (lane > i))
        rank = rank + jnp.where(ahead, 1.0, 0.0)
    return jnp.where((rank < topn) & (lane < n_valid), 1.0, 0.0)


def _nsa_prompt_kernel(q_ref, kc_ref, vc_ref, ks_ref, vs_ref, kw_ref, vw_ref, gt_ref, ov_ref, ex_ref, o_ref,
                       ks_sc, vs_sc, kw_sc, vw_sc, *, tq, seq, chunk):
    qi = pl.program_id(1)
    t0 = qi * tq

    @pl.when(qi == 0)
    def _():
        ks_sc[...] = ks_ref[...].astype(BF16)
        vs_sc[...] = vs_ref[...].astype(BF16)
        kw_sc[...] = kw_ref[...].astype(BF16)
        vw_sc[...] = vw_ref[...].astype(BF16)

    n_cmp = kc_ref.shape[0]
    n_blk = seq // NSA_SEL_LEN
    jpad = ov_ref.shape[0]
    wkeys = NSA_WIN + tq
    rows = NSA_G * tq
    gates = jax.nn.sigmoid(gt_ref[...])
    tpos = t0 + lax.broadcasted_iota(jnp.int32, (tq, 1), 0)

    def heads3(x):
        return x.reshape(NSA_G, tq, x.shape[-1])

    c_end = lax.broadcasted_iota(jnp.int32, (tq, n_cmp), 1) * NSA_CMP_STRIDE + NSA_CMP_LEN - 1
    cmp_ok = (c_end <= tpos) & (c_end < (n_cmp - 1) * NSA_CMP_STRIDE + NSA_CMP_LEN - 1)
    cmp_bias = jnp.where(cmp_ok, 0.0, MASK_BIAS)
    has_cmp = jnp.where(tpos >= NSA_CMP_LEN - 1, 1.0, 0.0)
    w0 = pl.multiple_of(jnp.maximum(t0 - NSA_WIN, 0), tq)
    wdist = tpos - (w0 + lax.broadcasted_iota(jnp.int32, (tq, wkeys), 1))
    win_bias = jnp.where((wdist >= 0) & (wdist <= NSA_WIN), 0.0, MASK_BIAS)
    n_sel_chunks = (t0 + tq + chunk - 1) // chunk
    k_last = pl.multiple_of((n_sel_chunks - 1) * chunk, chunk)
    causal_bias = jnp.where(k_last + lax.broadcasted_iota(jnp.int32, (tq, chunk), 1) <= tpos, 0.0, MASK_BIAS)
    t_lane = t0 + lax.broadcasted_iota(jnp.int32, (jpad, tq), 1)
    j_idx = lax.broadcasted_iota(jnp.int32, (jpad, tq), 0)
    tb = t_lane // NSA_SEL_LEN
    forced = (j_idx == 0) | (j_idx == tb) | (j_idx == tb - 1)

    for kvh in range(NSA_KVH):
        c0 = kvh * HD
        qs = jnp.concatenate(
            [q_ref[:, (kvh * NSA_G + g) * HD:(kvh * NSA_G + g + 1) * HD] for g in range(NSA_G)], axis=0)
        qs = (qs * (HD ** -0.5)).astype(BF16)

        s = heads3(_dot_nt(qs, kc_ref[:, c0:c0 + HD].astype(BF16))) + cmp_bias[None]
        e = jnp.exp(s - jnp.max(s, axis=-1, keepdims=True))
        p_c = e * (has_cmp[None] / jnp.sum(e, axis=-1, keepdims=True))
        o_cmp = _bdot(p_c.reshape(rows, n_cmp).astype(BF16), vc_ref[:, c0:c0 + HD].astype(BF16))
        ph, plo = _split_hi_lo(jnp.sum(p_c, axis=0))
        imp = _dot_nt(ov_ref[...], ph) + _dot_nt(ov_ref[...], plo)
        score = jnp.where(j_idx <= tb, imp + jnp.where(forced, NSA_FORCE_BONUS, 0.0), -jnp.inf)
        rank = jnp.zeros((jpad, tq), F32)
        for i in range(n_blk):
            si = score[i:i + 1, :]
            rank = rank + jnp.where((si > score) | ((si == score) & (j_idx > i)), 1.0, 0.0)
        sel_bias = jnp.where((rank < NSA_TOPN) & (j_idx < n_blk), 0.0, MASK_BIAS).astype(BF16)

        def sel_chunk(ci, carry, extra_bias):
            m, l, acc = carry
            k0 = pl.multiple_of(ci * chunk, chunk)
            kch = ks_sc[pl.ds(k0, chunk), c0:c0 + HD]
            vch = vs_sc[pl.ds(k0, chunk), c0:c0 + HD]
            bias = _dot_tn(sel_bias, ex_ref[ci])
            if extra_bias is not None:
                bias = bias + extra_bias
            sc = heads3(_dot_nt(qs, kch)) + bias[None]
            m_new = jnp.maximum(m, jnp.max(sc, axis=-1, keepdims=True))
            p = jnp.exp(sc - m_new)
            alpha = jnp.exp(m - m_new)
            l = alpha * l + jnp.sum(p, axis=-1, keepdims=True)
            pv = _bdot(p.reshape(rows, chunk).astype(BF16), vch)
            acc = alpha * acc + heads3(pv)
            return m_new, l, acc

        init = (jnp.full((NSA_G, tq, 1), NEG_BIG, F32), jnp.zeros((NSA_G, tq, 1), F32),
                jnp.zeros((NSA_G, tq, HD), F32))
        carry = lax.fori_loop(0, n_sel_chunks - 1, lambda ci, c: sel_chunk(ci, c, None), init)
        _, l_s, acc_s = sel_chunk(n_sel_chunks - 1, carry, causal_bias)
        o_sel = acc_s / l_s

        kwin = kw_sc[pl.ds(w0, wkeys), c0:c0 + HD]
        vwin = vw_sc[pl.ds(w0, wkeys), c0:c0 + HD]
        sw = heads3(_dot_nt(qs, kwin)) + win_bias[None]
        ew = jnp.exp(sw - jnp.max(sw, axis=-1, keepdims=True))
        o_win = heads3(_bdot(ew.reshape(rows, wkeys).astype(BF16), vwin)) / jnp.sum(ew, axis=-1, keepdims=True)
        o_cmp = heads3(o_cmp)

        outs = []
        for g in range(NSA_G):
            h = kvh * NSA_G + g
            outs.append(gates[:, 3 * h:3 * h + 1] * o_cmp[g] + gates[:, 3 * h + 1:3 * h + 2] * o_sel[g]
                        + gates[:, 3 * h + 2:3 * h + 3] * o_win[g])
        o_ref[:, kvh * NSA_G * HD:(kvh + 1) * NSA_G * HD] = jnp.concatenate(outs, axis=1).astype(o_ref.dtype)


def _compress_weights(pe, w1, w2):
    w1r = w1.reshape(2, NSA_CMP_LEN, HD, NSA_CMP_HID)
    pe2 = jnp.tile(pe, (1, 1, 2)).reshape(2, NSA_CMP_LEN, 1, 2 * HD)
    return w1r.astype(BF16), pe2, w2.astype(BF16)


def _compress_prompt_kernel(src_ref, w1_ref, pe_ref, w2_ref, o_ref, stage_sc, *, n_ch):
    rows = NSA_CMP_STRIDE * n_ch
    for h in range(2):
        stage_sc[h, 0:rows, :] = src_ref[:, 128 * h:128 * (h + 1)]
        stage_sc[h, rows:rows + NSA_CMP_STRIDE, :] = jnp.zeros((NSA_CMP_STRIDE, 128), F32)
    o_ref[0] = _compress_rows(stage_sc, n_ch, w1_ref.at[0], pe_ref.at[0], w2_ref[0])


def _compress_prompt(kv4, nb, seq, cw):
    w1r, pe2, w2 = cw
    n_ch = seq // NSA_CMP_STRIDE
    return pl.pallas_call(
        functools.partial(_compress_prompt_kernel, n_ch=n_ch),
        grid=(2, nb),
        in_specs=[
            pl.BlockSpec((seq, KV_COLS), lambda s, b: (b, s)),
            pl.BlockSpec((1,) + w1r.shape[1:], lambda s, b: (s, 0, 0, 0)),
            pl.BlockSpec((1,) + pe2.shape[1:], lambda s, b: (s, 0, 0, 0)),
            pl.BlockSpec((1,) + w2.shape[1:], lambda s, b: (s, 0, 0)),
        ],
        out_specs=pl.BlockSpec((1, n_ch, KV_COLS), lambda s, b: (s, b, 0)),
        out_shape=jax.ShapeDtypeStruct((2, nb * n_ch, KV_COLS), F32),
        scratch_shapes=[pltpu.VMEM((2, seq + NSA_CMP_STRIDE, 128), F32)],
        compiler_params=_params(("arbitrary", "arbitrary"), 40),
        name="nsa_compress_prompt",
    )(kv4, w1r, pe2, w2)


def _overlap_matrix(n_cmp_pad, n_cmp, n_blk, lanes=128):
    c = jnp.arange(n_cmp_pad)[:, None] * NSA_CMP_STRIDE
    j = jnp.arange(lanes)[None, :] * NSA_SEL_LEN
    ok = (c < j + NSA_SEL_LEN) & (c + NSA_CMP_LEN > j)
    ok = ok & (jnp.arange(n_cmp_pad)[:, None] < n_cmp) & (jnp.arange(lanes)[None, :] < n_blk)
    return ok.astype(BF16)


def _expand_matrix(n_keys, chunk, lanes=128):
    key = jnp.arange(n_keys).reshape(n_keys // chunk, 1, chunk)
    j = jnp.arange(lanes).reshape(1, lanes, 1)
    return (key // NSA_SEL_LEN == j).astype(BF16)


def _nsa_prompt_attention(q, kc, vc, kv4, kvw, gates, nb, seq, tq=128, chunk=512):
    nq = seq // tq
    n_cmp = seq // NSA_CMP_STRIDE
    n_blk = seq // NSA_SEL_LEN
    jpad = -(-n_blk // 8) * 8
    ov = _overlap_matrix(n_cmp, n_cmp - 1, n_blk, jpad).T
    ex = _expand_matrix(seq, chunk, jpad)
    return pl.pallas_call(
        functools.partial(_nsa_prompt_kernel, tq=tq, seq=seq, chunk=chunk),
        grid=(nb, nq),
        in_specs=[
            pl.BlockSpec((tq, q.shape[1]), lambda b, i: (b * nq + i, 0)),
            pl.BlockSpec((n_cmp, KV_COLS), lambda b, i: (b, 0)),
            pl.BlockSpec((n_cmp, KV_COLS), lambda b, i: (b, 0)),
            pl.BlockSpec((seq, KV_COLS), lambda b, i: (b, 2)),
            pl.BlockSpec((seq, KV_COLS), lambda b, i: (b, 3)),
            pl.BlockSpec((seq, KV_COLS), lambda b, i: (b, 0)),
            pl.BlockSpec((seq, KV_COLS), lambda b, i: (b, 1)),
            pl.BlockSpec((tq, gates.shape[1]), lambda b, i: (b * nq + i, 0)),
            _const_spec(ov.shape),
            _const_spec(ex.shape),
        ],
        out_specs=pl.BlockSpec((tq, q.shape[1]), lambda b, i: (b * nq + i, 0)),
        out_shape=jax.ShapeDtypeStruct(q.shape, BF16),
        scratch_shapes=[pltpu.VMEM((seq, KV_COLS), BF16)] * 4,
        compiler_params=_params(("arbitrary", "arbitrary"), 48),
        name="nsa_prompt_attention",
    )(q, kc, vc, kv4, kv4, kvw, kvw, gates, ov, ex)


def _nsa_weights(w_in):
    nq = NSA_KVH * NSA_G * HD
    wg = jnp.pad(w_in[:, nq + 6 * KV_COLS:], ((0, 0), (0, 128 - 3 * NSA_KVH * NSA_G)))
    return [w_in[:, :nq].astype(BF16), w_in[:, nq:nq + 4 * KV_COLS].astype(BF16),
            w_in[:, nq + 4 * KV_COLS:nq + 6 * KV_COLS].astype(BF16), wg.astype(BF16)]


def _nsa_prompt(tok, x, g, nb, seq, w_in_parts, cw, w_out, name="nsa"):
    q, kv4, kvw, gates = _mod_proj(tok, x, g, w_in_parts, [F32, F32, F32, F32], name + "_proj")
    cmp_rows = _compress_prompt(kv4, nb, seq, cw)
    o = _nsa_prompt_attention(q, cmp_rows[0], cmp_rows[1], kv4, kvw, gates, nb, seq)
    x_new = _out_proj(tok, x, [o], w_out, "plain", name + "_out")
    return x_new, kv4, kvw


T_PAD = 8
PAGE_SIZE = 128
PAGES_PER_STEP = 16


def _softmax_two(s1, mask1, v1, s2, mask2, v2):
    m = jnp.maximum(jnp.max(jnp.where(mask1, s1, NEG_BIG), axis=-1, keepdims=True),
                    jnp.max(jnp.where(mask2, s2, NEG_BIG), axis=-1, keepdims=True))
    p1 = jnp.where(mask1, jnp.exp(jnp.where(mask1, s1, NEG_BIG) - m), 0.0)
    p2 = jnp.where(mask2, jnp.exp(jnp.where(mask2, s2, NEG_BIG) - m), 0.0)
    den = jnp.sum(p1, axis=-1, keepdims=True) + jnp.sum(p2, axis=-1, keepdims=True)
    acc = _bdot(p1.astype(BF16), v1) + _bdot(p2.astype(BF16), v2)
    return acc / jnp.where(den > 0, den, 1.0), m, den


def _stack_heads(q_ref, kvh):
    qs = jnp.concatenate(
        [q_ref[0, :, (kvh * NSA_G + g) * HD:(kvh * NSA_G + g + 1) * HD] for g in range(NSA_G)], axis=0)
    return (qs * (HD ** -0.5)).astype(BF16)


def _nsa_sample_cmp_kernel(pt_ref, *refs, n_pages, n_valid):
    del pt_ref
    pages = refs[:PAGES_PER_STEP]
    q_ref, w1_ref, pe_ref, w2_ref, ov_ref, ocmp_ref, sel_ref, stage_sc = refs[PAGES_PER_STEP:]
    step = pl.program_id(1)
    n_rows = n_pages * PAGE_SIZE
    n_ch = n_rows // NSA_CMP_STRIDE

    @pl.when(step == 0)
    def _():
        for h in range(4):
            stage_sc[h, n_rows:n_rows + NSA_CMP_STRIDE, :] = jnp.zeros((NSA_CMP_STRIDE, 128), F32)

    for k, page in enumerate(pages):
        r0 = pl.multiple_of((step * PAGES_PER_STEP + k) * PAGE_SIZE, PAGE_SIZE)
        for h in range(4):
            pair = [page[0, :, h // 2, 2 * (h % 2) + j, :] for j in range(2)]
            stage_sc[h, pl.ds(r0, PAGE_SIZE), :] = jnp.concatenate(pair, axis=1)

    @pl.when(step == pl.num_programs(1) - 1)
    def _():
        kc = _compress_rows(stage_sc.at[0:2], n_ch, w1_ref.at[0], pe_ref.at[0], w2_ref[0]).astype(BF16)
        vc = _compress_rows(stage_sc.at[2:4], n_ch, w1_ref.at[1], pe_ref.at[1], w2_ref[1]).astype(BF16)
        tpos = n_rows + lax.broadcasted_iota(jnp.int32, (T_PAD, 1), 0)
        c_idx = lax.broadcasted_iota(jnp.int32, (T_PAD, n_ch), 1)
        cmp_mask = (c_idx * NSA_CMP_STRIDE + NSA_CMP_LEN - 1 <= tpos) & (c_idx < n_ch - 1)
        cmp_mask = jnp.concatenate([cmp_mask] * NSA_G, axis=0)
        lanes = sel_ref.shape[3]
        j_idx = lax.broadcasted_iota(jnp.int32, (T_PAD, lanes), 1)
        tb = tpos // NSA_SEL_LEN
        forced = (j_idx == 0) | (j_idx == tb) | (j_idx == tb - 1)
        n_blk = (n_rows + n_valid + NSA_SEL_LEN - 1) // NSA_SEL_LEN
        for kvh in range(NSA_KVH):
            c0 = kvh * HD
            qs = _stack_heads(q_ref, kvh)
            p_c, _, _ = _masked_softmax_rows(_dot_nt(qs, kc[:, c0:c0 + HD]), cmp_mask)
            o_cmp = _bdot(p_c.astype(BF16), vc[:, c0:c0 + HD])
            p_sum = p_c[0:T_PAD]
            for g in range(1, NSA_G):
                p_sum = p_sum + p_c[g * T_PAD:(g + 1) * T_PAD]
            ph, plo = _split_hi_lo(p_sum)
            imp = _bdot(ph, ov_ref[...]) + _bdot(plo, ov_ref[...])
            score = jnp.where(j_idx <= tb, imp + jnp.where(forced, NSA_FORCE_BONUS, 0.0), -jnp.inf)
            sel_ref[0, kvh] = _top_blocks(score, n_blk, NSA_TOPN)
            ocmp_ref[0, :, kvh * NSA_G * HD:(kvh + 1) * NSA_G * HD] = jnp.concatenate(
                [o_cmp[g * T_PAD:(g + 1) * T_PAD] for g in range(NSA_G)], axis=1)


def _page_specs(slot_pair):
    return [pl.BlockSpec((1, PAGE_SIZE, 2, NSA_KVH, HD),
                         lambda b, s, pt, k=k: (pt[b, s * PAGES_PER_STEP + k], 0, slot_pair, 0, 0))
            for k in range(PAGES_PER_STEP)]


def _nsa_sample_cmp(q, cache, page_table, cw, n_valid):
    w1r, pe2, w2 = cw
    nb, n_pages = page_table.shape
    n_rows = n_pages * PAGE_SIZE
    n_ch = n_rows // NSA_CMP_STRIDE
    n_blk = (n_rows + n_valid + NSA_SEL_LEN - 1) // NSA_SEL_LEN
    lanes = -(-n_blk // 128) * 128
    ov = _overlap_matrix(n_ch, n_ch - 1, n_blk, lanes)
    bspec = lambda shape: pl.BlockSpec(shape, lambda b, s, pt: (b,) + (0,) * (len(shape) - 1))
    cspec = lambda shape: pl.BlockSpec(shape, lambda b, s, pt: (0,) * len(shape))
    grid_spec = pltpu.PrefetchScalarGridSpec(
        num_scalar_prefetch=1,
        grid=(nb, n_pages // PAGES_PER_STEP),
        in_specs=_page_specs(0) + [bspec((1, T_PAD, q.shape[2])), cspec(w1r.shape), cspec(pe2.shape),
                                    cspec(w2.shape), cspec(ov.shape)],
        out_specs=[bspec((1, T_PAD, q.shape[2])), bspec((1, NSA_KVH, T_PAD, lanes))],
        scratch_shapes=[pltpu.VMEM((4, n_rows + NSA_CMP_STRIDE, 128), F32)],
    )
    return pl.pallas_call(
        functools.partial(_nsa_sample_cmp_kernel, n_pages=n_pages, n_valid=n_valid),
        grid_spec=grid_spec,
        out_shape=[jax.ShapeDtypeStruct(q.shape, F32), jax.ShapeDtypeStruct((nb, NSA_KVH, T_PAD, lanes), F32)],
        compiler_params=_params(("arbitrary", "arbitrary"), 56),
        name="nsa_sample_cmp",
    )(page_table, *([cache] * PAGES_PER_STEP), q, w1r, pe2, w2, ov)


def _nsa_sample_attend_kernel(pt_ref, *refs, n_pages, n_valid):
    del pt_ref
    pages = refs[:PAGES_PER_STEP]
    (q_ref, sel_ref, ocmp_ref, new4_ref, neww_ref, win_ref, gt_ref, ex_ref, o_ref, ks_sc, vs_sc) = refs[PAGES_PER_STEP:]
    step = pl.program_id(1)
    n_rows = n_pages * PAGE_SIZE

    for k, page in enumerate(pages):
        r0 = pl.multiple_of((step * PAGES_PER_STEP + k) * PAGE_SIZE, PAGE_SIZE)
        for slot, dst in enumerate((ks_sc, vs_sc)):
            heads = [page[0, :, slot, h, :] for h in range(NSA_KVH)]
            dst[pl.ds(r0, PAGE_SIZE), :] = jnp.concatenate(heads, axis=1).astype(BF16)

    @pl.when(step == pl.num_programs(1) - 1)
    def _():
        def tile_g(x):
            return jnp.concatenate([x] * NSA_G, axis=0)

        lb = win_ref.shape[1]
        t_idx = lax.broadcasted_iota(jnp.int32, (T_PAD, 1), 0)
        i_new = lax.broadcasted_iota(jnp.int32, (T_PAD, T_PAD), 1)
        new_mask = tile_g((i_new <= t_idx) & (i_new < n_valid))
        w_idx = lax.broadcasted_iota(jnp.int32, (T_PAD, lb), 1)
        win_mask = tile_g(w_idx >= lb + t_idx - NSA_WIN)
        gates = jax.nn.sigmoid(gt_ref[0])
        for kvh in range(NSA_KVH):
            c0 = kvh * HD
            qs = _stack_heads(q_ref, kvh)
            chosen = tile_g(_bdot(sel_ref[0, kvh].astype(BF16), ex_ref[...]) > 0.5)
            k_new = new4_ref[0, :, 2 * KV_COLS + c0:2 * KV_COLS + c0 + HD].astype(BF16)
            v_new = new4_ref[0, :, 3 * KV_COLS + c0:3 * KV_COLS + c0 + HD].astype(BF16)
            o_sel, _, _ = _softmax_two(_dot_nt(qs, ks_sc[:, c0:c0 + HD]), chosen, vs_sc[:, c0:c0 + HD],
                                       _dot_nt(qs, k_new), new_mask, v_new)
            kw_old = win_ref[0, :, 0, kvh, :].astype(BF16)
            vw_old = win_ref[0, :, 1, kvh, :].astype(BF16)
            kw_new = neww_ref[0, :, c0:c0 + HD].astype(BF16)
            vw_new = neww_ref[0, :, KV_COLS + c0:KV_COLS + c0 + HD].astype(BF16)
            o_win, _, _ = _softmax_two(_dot_nt(qs, kw_old), win_mask, vw_old, _dot_nt(qs, kw_new), new_mask, vw_new)
            outs = []
            for g in range(NSA_G):
                h = kvh * NSA_G + g
                r = slice(g * T_PAD, (g + 1) * T_PAD)
                outs.append(gates[:, 3 * h:3 * h + 1] * ocmp_ref[0, :, h * HD:(h + 1) * HD]
                            + gates[:, 3 * h + 1:3 * h + 2] * o_sel[r] + gates[:, 3 * h + 2:3 * h + 3] * o_win[r])
            o_ref[0, :, kvh * NSA_G * HD:(kvh + 1) * NSA_G * HD] = jnp.concatenate(outs, axis=1).astype(o_ref.dtype)


def _nsa_sample_attend(q, sel, ocmp, new4, neww, win, gates, cache, page_table, n_valid):
    nb, n_pages = page_table.shape
    n_rows = n_pages * PAGE_SIZE
    lanes = sel.shape[3]
    ex = _expand_matrix(n_rows, n_rows, lanes)[0]
    bspec = lambda shape: pl.BlockSpec(shape, lambda b, s, pt: (b,) + (0,) * (len(shape) - 1))
    cspec = lambda shape: pl.BlockSpec(shape, lambda b, s, pt: (0,) * len(shape))
    grid_spec = pltpu.PrefetchScalarGridSpec(
        num_scalar_prefetch=1,
        grid=(nb, n_pages // PAGES_PER_STEP),
        in_specs=_page_specs(1) + [
            bspec((1, T_PAD, q.shape[2])), bspec((1,) + sel.shape[1:]), bspec((1, T_PAD, ocmp.shape[2])),
            bspec((1, T_PAD, new4.shape[2])), bspec((1, T_PAD, neww.shape[2])), bspec((1,) + win.shape[1:]),
            bspec((1, T_PAD, gates.shape[2])), cspec(ex.shape)],
        out_specs=bspec((1, T_PAD, q.shape[2])),
        scratch_shapes=[pltpu.VMEM((n_rows, KV_COLS), BF16)] * 2,
    )
    return pl.pallas_call(
        functools.partial(_nsa_sample_attend_kernel, n_pages=n_pages, n_valid=n_valid),
        grid_spec=grid_spec,
        out_shape=jax.ShapeDtypeStruct(q.shape, BF16),
        compiler_params=_params(("arbitrary", "arbitrary"), 56),
        name="nsa_sample_attend",
    )(page_table, *([cache] * PAGES_PER_STEP), q, sel, ocmp, new4, neww, win, gates, ex)


def _nsa_sample(tok, x, g, nb, cache, win, page_table, n_valid, w_in_parts, cw, w_out, name="nsa_s"):
    q, kv4, kvw, gates = _mod_proj(tok, x, g, w_in_parts, [F32, F32, F32, F32], name + "_proj")
    r3 = lambda a: a.reshape(nb, T_PAD, a.shape[1])
    ocmp, sel = _nsa_sample_cmp(r3(q), cache, page_table, cw, n_valid)
    o = _nsa_sample_attend(r3(q), sel, ocmp, r3(kv4), r3(kvw), win, r3(gates), cache, page_table, n_valid)
    x_new = _out_proj(tok, x, [o.reshape(nb * T_PAD, o.shape[2])], w_out, "plain", name + "_out")
    return x_new, r3(kv4), r3(kvw)


DIL_PAIRS = ((128, 1), (512, 4), (2048, 16))
DIL_SLOTS = 8
DIL_COLS = DIL_SLOTS * HD


def _dil_prompt_kernel(q_ref, k_ref, v_ref, o_ref, l_ref, *, tq, ls, nback):
    t0 = pl.program_id(2) * tq
    wkeys = min(nback + tq, ls)
    w0 = pl.multiple_of(jnp.maximum(t0 - nback, 0), tq)
    tpos = t0 + lax.broadcasted_iota(jnp.int32, (tq, 1), 0)
    dist = tpos - (w0 + lax.broadcasted_iota(jnp.int32, (tq, wkeys), 1))
    mask = (dist >= 0) & (dist <= nback)
    outs, lses = [], []
    for h in range(DIL_SLOTS):
        c0 = h * HD
        qh = (q_ref[:, c0:c0 + HD] * (HD ** -0.5)).astype(BF16)
        kh = k_ref[pl.ds(w0, wkeys), c0:c0 + HD].astype(BF16)
        vh = v_ref[pl.ds(w0, wkeys), c0:c0 + HD].astype(BF16)
        p, m, den = _masked_softmax_rows(_dot_nt(qh, kh), mask)
        outs.append(_bdot(p.astype(BF16), vh))
        lses.append(jnp.broadcast_to(m + jnp.log(den), (tq, HD)))
    o_ref[...] = jnp.concatenate(outs, axis=1)
    l_ref[...] = jnp.concatenate(lses, axis=1)


def _dil_prompt_group(q3, kv, gi, nb, seq, tq=128):
    window, dil = DIL_PAIRS[gi]
    ls = seq // dil
    nq = ls // tq
    qv = q3.reshape(nb * ls, dil * 3 * DIL_COLS)
    kvv = kv.reshape(nb * ls, dil * 2 * DIL_COLS)
    out_sd = jax.ShapeDtypeStruct((nb * ls, dil * DIL_COLS), F32)
    o, lse = pl.pallas_call(
        functools.partial(_dil_prompt_kernel, tq=tq, ls=ls, nback=window // dil),
        grid=(nb, dil, nq),
        in_specs=[
            pl.BlockSpec((tq, DIL_COLS), lambda b, r, i: (b * nq + i, 3 * r + gi)),
            pl.BlockSpec((ls, DIL_COLS), lambda b, r, i: (b, 2 * r)),
            pl.BlockSpec((ls, DIL_COLS), lambda b, r, i: (b, 2 * r + 1)),
        ],
        out_specs=[pl.BlockSpec((tq, DIL_COLS), lambda b, r, i: (b * nq + i, r))] * 2,
        out_shape=[out_sd, out_sd],
        compiler_params=_params(("arbitrary", "arbitrary", "arbitrary"), 40),
        name=f"dil_prompt_g{gi}",
    )(qv, kvv, kvv)
    return o.reshape(nb * seq, DIL_COLS), lse.reshape(nb * seq, DIL_COLS)


def _dil_weights(w_in):
    n_g = len(DIL_PAIRS)
    w = w_in.reshape(D_MODEL, 3, n_g, DIL_COLS)
    parts = [w[:, 0].reshape(D_MODEL, n_g * DIL_COLS)]
    for gi in range(n_g):
        parts.append(jnp.concatenate([w[:, 1, gi], w[:, 2, gi]], axis=1))
    return [p.astype(BF16) for p in parts]


def _dil_prompt(tok, x, g, nb, seq, w_in_parts, w_out, name="dil"):
    q3, kv0, kv1, kv2 = _mod_proj(tok, x, g, w_in_parts, [F32] * 4, name + "_proj")
    kvs = (kv0, kv1, kv2)
    outs, lses = [], []
    for gi in range(len(DIL_PAIRS)):
        o, lse = _dil_prompt_group(q3, kvs[gi], gi, nb, seq)
        outs.append(o)
        lses.append(lse)
    x_new = _out_proj(tok, x, outs + lses, w_out, "dil", name + "_out")
    return x_new, kvs


def _dil_sample_kernel(q_ref, st_ref, new_ref, o_ref, l_ref, *, window, dil, n_valid):
    lb = st_ref.shape[1]
    t_idx = lax.broadcasted_iota(jnp.int32, (T_PAD, 1), 0)
    d_old = lb + t_idx - lax.broadcasted_iota(jnp.int32, (T_PAD, lb), 1)
    old_mask = ((d_old & (dil - 1)) == 0) & (d_old <= window)
    i_new = lax.broadcasted_iota(jnp.int32, (T_PAD, T_PAD), 1)
    d_new = t_idx - i_new
    new_mask = (d_new >= 0) & ((d_new & (dil - 1)) == 0) & (i_new < n_valid)
    outs, lses = [], []
    for h in range(DIL_SLOTS):
        c0 = h * HD
        qh = (q_ref[0, :, c0:c0 + HD] * (HD ** -0.5)).astype(BF16)
        k_old = st_ref[0, :, 0, h, :].astype(BF16)
        v_old = st_ref[0, :, 1, h, :].astype(BF16)
        k_new = new_ref[0, :, c0:c0 + HD].astype(BF16)
        v_new = new_ref[0, :, DIL_COLS + c0:DIL_COLS + c0 + HD].astype(BF16)
        o, m, den = _softmax_two(_dot_nt(qh, k_old), old_mask, v_old, _dot_nt(qh, k_new), new_mask, v_new)
        outs.append(o)
        lses.append(jnp.broadcast_to(m + jnp.log(den), (T_PAD, HD)))
    o_ref[0] = jnp.concatenate(outs, axis=1)
    l_ref[0] = jnp.concatenate(lses, axis=1)


def _dil_sample_group(q3, kv_new, state, gi, n_valid):
    window, dil = DIL_PAIRS[gi]
    nb, lb = state.shape[:2]
    out_sd = jax.ShapeDtypeStruct((nb, T_PAD, DIL_COLS), F32)
    return pl.pallas_call(
        functools.partial(_dil_sample_kernel, window=window, dil=dil, n_valid=n_valid),
        grid=(nb,),
        in_specs=[
            pl.BlockSpec((1, T_PAD, DIL_COLS), lambda b: (b, 0, gi)),
            pl.BlockSpec((1, lb, 2, DIL_SLOTS, HD), lambda b: (b, 0, 0, 0, 0)),
            pl.BlockSpec((1, T_PAD, 2 * DIL_COLS), lambda b: (b, 0, 0)),
        ],
        out_specs=[pl.BlockSpec((1, T_PAD, DIL_COLS), lambda b: (b, 0, 0))] * 2,
        out_shape=[out_sd, out_sd],
        compiler_params=_params(("arbitrary",), 40),
        name=f"dil_sample_g{gi}",
    )(q3, state, kv_new)


def _dil_sample(tok, x, g, nb, states, n_valid, w_in_parts, w_out, name="dil_s"):
    q3, kv0, kv1, kv2 = _mod_proj(tok, x, g, w_in_parts, [F32] * 4, name + "_proj")
    r3 = lambda a: a.reshape(nb, T_PAD, a.shape[1])
    kvs = (r3(kv0), r3(kv1), r3(kv2))
    outs, lses = [], []
    for gi in range(len(DIL_PAIRS)):
        o, lse = _dil_sample_group(r3(q3), kvs[gi], states[gi], gi, n_valid)
        outs.append(o.reshape(nb * T_PAD, DIL_COLS))
        lses.append(lse.reshape(nb * T_PAD, DIL_COLS))
    x_new = _out_proj(tok, x, outs + lses, w_out, "dil", name + "_out")
    return x_new, kvs


GLA_H, GLA_DK, GLA_DV, GLA_RANK, GLA_TAU = 4, 128, 256, 16, 16.0
HG_H, HG_DK, HG_DV = 8, 128, 128
SCAN_SUB = 8
SCAN_HEADS_PER_STEP = 4
SCAN_UNROLL = 2


def _log_sigmoid(x):
    return jnp.minimum(x, 0.0) - jnp.log1p(jnp.exp(-jnp.abs(x)))


def _cumsum_rows(g):
    n = g.shape[0]
    tri = (lax.broadcasted_iota(jnp.int32, (n, n), 0) >= lax.broadcasted_iota(jnp.int32, (n, n), 1)).astype(BF16)
    hi = g.astype(BF16)
    r1 = g - hi.astype(F32)
    mid = r1.astype(BF16)
    lo = (r1 - mid.astype(F32)).astype(BF16)
    return _bdot(tri, hi) + _bdot(tri, mid) + _bdot(tri, lo)


def _scan_chunk(q, k, g, v, st):
    n, dk = q.shape
    sub = min(SCAN_SUB, n)
    b = _cumsum_rows(g)
    b_end = b[n - 1:n]
    o = _dot_nt((q * jnp.exp(b)).astype(BF16), st.astype(BF16))

    lane = lax.broadcasted_iota(jnp.int32, (sub, n), 1)
    row = lax.broadcasted_iota(jnp.int32, (sub, n), 0)
    ones = jnp.ones((dk, n), BF16)
    a_rows = []
    for i in range(n // sub):
        lo = i * sub
        qi, ki, bi = q[lo:lo + sub], k[lo:lo + sub], b[lo:lo + sub]
        prods = []
        for s in range(sub):
            e = jnp.exp(jnp.minimum(bi - bi[s:s + 1], 0.0))
            prods.append((qi * ki[s:s + 1] * e).astype(BF16))
        sums = _bdot(jnp.concatenate(prods, axis=0), ones)
        a_i = jnp.zeros((sub, n), F32)
        for s in range(sub):
            a_i = a_i + jnp.where((lane == lo + s) & (row >= s), sums[s * sub:(s + 1) * sub], 0.0)
        if i > 0:
            b_ref = b[lo - 1:lo]
            qd = (qi * jnp.exp(bi - b_ref)).astype(BF16)
            kd = (k[0:lo] * jnp.exp(b_ref - b[0:lo])).astype(BF16)
            if lo < n:
                kd = jnp.concatenate([kd, jnp.zeros((n - lo, dk), BF16)], axis=0)
            a_i = a_i + _dot_nt(qd, kd)
        a_rows.append(a_i)
    a = jnp.concatenate(a_rows, axis=0) if len(a_rows) > 1 else a_rows[0]
    o = o + _bdot(a.astype(BF16), v.astype(BF16))
    kd_end = (k * jnp.exp(b_end - b)).astype(BF16)
    st_new = st * jnp.exp(b_end) + _dot_tn(v.astype(BF16), kd_end)
    return o, st_new


def _scan_kernel(*refs, kind, chunk, n_chunks, n_valid, hpb, dk, dv):
    if kind == "gla":
        q_ref, k_ref, v_ref, a_ref, wa_ref, ba_ref, s0_ref, o_ref, sf_ref, st_sc = refs
    else:
        f_ref, v_ref, q_ref, lb_ref, s0_ref, o_ref, sf_ref, st_sc = refs
    ci = pl.program_id(2)

    @pl.when(ci == 0)
    def _():
        st_sc[...] = s0_ref[0]

    def body(j, carry):
        r0 = pl.multiple_of(j * chunk, chunk)
        rows = pl.ds(r0, chunk)
        for hh in range(hpb):
            kc = slice(hh * dk, (hh + 1) * dk)
            vc = slice(hh * dv, (hh + 1) * dv)
            if kind == "gla":
                q = q_ref[rows, kc] * (GLA_DK ** -0.5)
                k = k_ref[rows, kc]
                pre = _bdot(a_ref[rows, :].astype(BF16), wa_ref[:, kc]) + ba_ref[:, kc]
                g = _log_sigmoid(pre) * (1.0 / GLA_TAU)
            else:
                fz = f_ref[rows, kc]
                lb = lb_ref[:, kc]
                q = _silu(q_ref[rows, kc])
                la = jnp.log(lb)
                lc = jnp.log1p(-lb) + _log_sigmoid(fz)
                g = jnp.maximum(la, lc) + jnp.log1p(jnp.exp(-jnp.abs(la - lc)))
                k = (1.0 - lb) * jax.nn.sigmoid(-fz)
            if n_valid < chunk:
                live = lax.broadcasted_iota(jnp.int32, k.shape, 0) < n_valid
                k = jnp.where(live, k, 0.0)
                g = jnp.where(live, g, 0.0)
            o, st = _scan_chunk(q, k, g, v_ref[rows, vc], st_sc[hh])
            st_sc[hh] = st
            o_ref[rows, vc] = o
        return carry

    lax.fori_loop(0, n_chunks, body, 0, unroll=min(SCAN_UNROLL, n_chunks))

    @pl.when(ci == pl.num_programs(2) - 1)
    def _():
        sf_ref[0] = st_sc[...]


def _scan(kind, ins, s0_t, nb, rows_per_b, n_valid, name):
    if kind == "gla":
        heads, dk, dv = GLA_H, GLA_DK, GLA_DV
    else:
        heads, dk, dv = HG_H, HG_DK, HG_DV
    chunk = min(64, rows_per_b)
    blk = min(512, rows_per_b)
    nblk = rows_per_b // blk
    hpb = SCAN_HEADS_PER_STEP
    rspec = lambda w: pl.BlockSpec((blk, hpb * w), lambda b, h, i: (b * nblk + i, h))
    if kind == "gla":
        q, k, v, a, wa, ba = ins
        args = [q, k, v, a, wa, ba]
        in_specs = [rspec(dk), rspec(dk), rspec(dv),
                    pl.BlockSpec((blk, a.shape[1]), lambda b, h, i: (b * nblk + i, 0)),
                    pl.BlockSpec((wa.shape[0], hpb * dk), lambda b, h, i: (0, h)),
                    pl.BlockSpec((1, hpb * dk), lambda b, h, i: (0, h))]
    else:
        f, v, q, lb = ins
        args = [f, v, q, lb]
        in_specs = [rspec(dk), rspec(dv), rspec(dk), pl.BlockSpec((1, hpb * dk), lambda b, h, i: (0, h))]
    m = nb * rows_per_b
    st_spec = pl.BlockSpec((1, hpb, dv, dk), lambda b, h, i: (b, h, 0, 0))
    return pl.pallas_call(
        functools.partial(_scan_kernel, kind=kind, chunk=chunk, n_chunks=blk // chunk, n_valid=n_valid,
                          hpb=hpb, dk=dk, dv=dv),
        grid=(nb, heads // hpb, nblk),
        in_specs=in_specs + [st_spec],
        out_specs=[rspec(dv), st_spec],
        out_shape=[jax.ShapeDtypeStruct((m, heads * dv), F32), jax.ShapeDtypeStruct((nb, heads, dv, dk), F32)],
        scratch_shapes=[pltpu.VMEM((hpb, dv, dk), F32)],
        compiler_params=_params(("arbitrary", "arbitrary", "arbitrary"), 40),
        name=name,
    )(*args, s0_t)


def _gla_weights(w_in, w_a2, b_a2):
    nk, nv = GLA_H * GLA_DK, GLA_H * GLA_DV
    wa = jnp.pad(w_in[:, 2 * nk + 2 * nv:], ((0, 0), (0, 128 - GLA_RANK)))
    parts = [w_in[:, :nk], w_in[:, nk:2 * nk], w_in[:, 2 * nk:2 * nk + nv], w_in[:, 2 * nk + nv:2 * nk + 2 * nv], wa]
    wa2 = jnp.pad(w_a2, ((0, 128 - GLA_RANK), (0, 0))).astype(BF16)
    return [p.astype(BF16) for p in parts], wa2, b_a2.reshape(1, nk)


def _gla(tok, x, g, s0_t, nb, rows_per_b, n_valid, wts, norm_g, w_out, name="gla"):
    parts, wa2, ba2 = wts
    q, k, v, r, a = _mod_proj(tok, x, g, parts, [F32] * 5, name + "_proj")
    o, s_t = _scan("gla", [q, k, v, a, wa2, ba2], s0_t, nb, rows_per_b, n_valid, name + "_scan")
    x_new = _out_proj(tok, x, [o, r, norm_g.reshape(1, GLA_DV)], w_out, "heads", name + "_out", heads=GLA_H)
    return x_new, s_t


def _hgrn_weights(w_in, lb_logits, layer):
    nk, nv = HG_H * HG_DK, HG_H * HG_DV
    parts = [w_in[:, :nk], w_in[:, nk:nk + nv], w_in[:, nk + nv:2 * nk + nv], w_in[:, 2 * nk + nv:]]
    sm = jax.nn.softmax(lb_logits.astype(F32), axis=0)
    lb = jnp.sum(sm[1:layer + 1], axis=0).reshape(1, nk)
    return [p.astype(BF16) for p in parts], lb


def _hgrn(tok, x, g, s0_t, nb, rows_per_b, n_valid, wts, norm_g, w_out, name="hgrn"):
    parts, lb = wts
    f, i_in, q, og = _mod_proj(tok, x, g, parts, [F32] * 4, name + "_proj")
    o, s_t = _scan("hgrn", [f, i_in, q, lb], s0_t, nb, rows_per_b, n_valid, name + "_scan")
    x_new = _out_proj(tok, x, [o, og, norm_g.reshape(1, HG_DV)], w_out, "heads", name + "_out", heads=HG_H)
    return x_new, s_t


PROMPT_TM = 256
PROMPT_MOE_TM = 512


def _moe_weights(w_rg, b_rg, w_re, b_re, w_up, w_down):
    n_e = MOE_GROUPS * MOE_PER_GROUP
    wr = jnp.zeros((D_MODEL, ROUTER_LANES), F32).at[:, :MOE_GROUPS].set(w_rg).at[:, MOE_GROUPS:MOE_GROUPS + n_e].set(w_re)
    br = jnp.zeros((1, ROUTER_LANES), F32).at[0, :MOE_GROUPS].set(b_rg).at[0, MOE_GROUPS:MOE_GROUPS + n_e].set(b_re)
    wr_hi, wr_lo = _split_hi_lo(wr)
    up = w_up.reshape(MOE_GROUPS, MOE_PER_GROUP, D_MODEL, 2 * MOE_FF).transpose(0, 2, 1, 3)
    up = up.reshape(MOE_GROUPS, D_MODEL, MOE_PER_GROUP * 2 * MOE_FF).astype(BF16)
    dn = w_down.reshape(MOE_GROUPS, MOE_PER_GROUP * MOE_FF, D_MODEL).astype(BF16)
    return wr_hi, wr_lo, br, up, dn


def kernel(x_prompt, x_sample, cache_nsa_kv, state_nsa_win, state_dil_0, state_dil_1, state_dil_2, state_gla, state_hgrn, page_table, c_prompt, c_sample, nsa_w_in, nsa_cmp_pe, nsa_cmp_w1, nsa_cmp_w2, nsa_w_out, dil_w_in, dil_w_out, gla_w_in, gla_w_a2, gla_b_a2, gla_norm_g, gla_w_out, hg_w_in, hg_lb_logits, hg_norm_g, hg_w_out, norm_g, ada_w, ada_b, moe_w_rg, moe_b_rg, moe_w_re, moe_b_re, moe_w_up, moe_w_down, final_norm_g):
    nb, seq, _ = x_prompt.shape
    ndb, n_t, _ = x_sample.shape
    depth = ada_w.shape[0]
    ms_rows = ndb * T_PAD

    c_rows = -(-(nb + ndb) // 16) * 16
    c_all = jnp.zeros((c_rows, D_MODEL), F32).at[:nb].set(c_prompt).at[nb:nb + ndb].set(c_sample)
    mod = _adaln(c_all, ada_w, ada_b)

    xp = x_prompt.reshape(nb * seq, D_MODEL)
    xs = jnp.pad(x_sample, ((0, 0), (0, T_PAD - n_t), (0, 0))).reshape(ms_rows, D_MODEL)

    outs = {}
    for i in range(depth):
        mp = mod[i, :nb].reshape(nb, ADA_N, D_MODEL).transpose(1, 0, 2).reshape(ADA_N * nb, 1, D_MODEL)
        ms = mod[i, nb:nb + ndb].reshape(ndb, ADA_N, D_MODEL).transpose(1, 0, 2)
        ms = jnp.repeat(ms, T_PAD, axis=1)
        tok_p = _Tokens(mp, False, nb, seq, PROMPT_TM)
        tok_pm = _Tokens(mp, False, nb, seq, PROMPT_MOE_TM)
        tok_s = _Tokens(ms, True, 1, ms_rows, ms_rows)
        g_mix, g_moe = norm_g[i, 0], norm_g[i, 1]
        kind = i % 4
        if kind == 0:
            parts = _nsa_weights(nsa_w_in)
            cw = _compress_weights(nsa_cmp_pe, nsa_cmp_w1, nsa_cmp_w2)
            w_out = nsa_w_out.astype(BF16)
            xp, kv4, kvw = _nsa_prompt(tok_p, xp, g_mix, nb, seq, parts, cw, w_out)
            xs, kv4_s, kvw_s = _nsa_sample(tok_s, xs, g_mix, ndb, cache_nsa_kv, state_nsa_win, page_table, n_t,
                                           parts, cw, w_out)
            keep = min(NSA_WIN, seq)
            outs["nsa_kv_p"] = kv4.reshape(nb, seq, 4, NSA_KVH, HD)
            outs["nsa_kv_s"] = kv4_s[:, :n_t].reshape(ndb, n_t, 4, NSA_KVH, HD).astype(cache_nsa_kv.dtype)
            outs["nsa_win_p"] = kvw.reshape(nb, seq, 2, NSA_KVH, HD)[:, seq - keep:]
            win_all = jnp.concatenate(
                [state_nsa_win, kvw_s[:, :n_t].reshape(ndb, n_t, 2, NSA_KVH, HD).astype(state_nsa_win.dtype)], axis=1)
            outs["nsa_win_s"] = win_all[:, win_all.shape[1] - min(NSA_WIN, win_all.shape[1]):]
        elif kind == 1:
            parts = _dil_weights(dil_w_in)
            w_out = dil_w_out.astype(BF16)
            states = (state_dil_0, state_dil_1, state_dil_2)
            xp, kvs = _dil_prompt(tok_p, xp, g_mix, nb, seq, parts, w_out)
            xs, kvs_s = _dil_sample(tok_s, xs, g_mix, ndb, states, n_t, parts, w_out)
            for gi, (window, _) in enumerate(DIL_PAIRS):
                buf = kvs[gi].reshape(nb, seq, 2, DIL_SLOTS, HD)
                outs[f"dil_p{gi}"] = buf[:, seq - min(window, seq):]
                new = kvs_s[gi][:, :n_t].reshape(ndb, n_t, 2, DIL_SLOTS, HD).astype(states[gi].dtype)
                kv_all = jnp.concatenate([states[gi], new], axis=1)
                outs[f"dil_s{gi}"] = kv_all[:, kv_all.shape[1] - min(window, kv_all.shape[1]):]
        elif kind == 2:
            wts = _gla_weights(gla_w_in, gla_w_a2, gla_b_a2)
            w_out = gla_w_out.astype(BF16)
            zero = jnp.zeros((nb, GLA_H, GLA_DV, GLA_DK), F32)
            xp, s_p = _gla(tok_p, xp, g_mix, zero, nb, seq, seq, wts, gla_norm_g, w_out)
            xs, s_s = _gla(tok_s, xs, g_mix, state_gla.astype(F32).transpose(0, 1, 3, 2), ndb, T_PAD, n_t, wts,
                           gla_norm_g, w_out, name="gla_s")
            outs["gla_p"] = s_p.transpose(0, 1, 3, 2).astype(state_gla.dtype)
            outs["gla_s"] = s_s.transpose(0, 1, 3, 2).astype(state_gla.dtype)
        else:
            wts = _hgrn_weights(hg_w_in, hg_lb_logits, i)
            w_out = hg_w_out.astype(BF16)
            zero = jnp.zeros((nb, HG_H, HG_DV, HG_DK), F32)
            xp, s_p = _hgrn(tok_p, xp, g_mix, zero, nb, seq, seq, wts, hg_norm_g, w_out)
            xs, s_s = _hgrn(tok_s, xs, g_mix, state_hgrn.astype(F32).transpose(0, 1, 3, 2), ndb, T_PAD, n_t, wts,
                            hg_norm_g, w_out, name="hgrn_s")
            outs["hg_p"] = s_p.transpose(0, 1, 3, 2).astype(state_hgrn.dtype)
            outs["hg_s"] = s_s.transpose(0, 1, 3, 2).astype(state_hgrn.dtype)
        mw = _moe_weights(moe_w_rg[i], moe_b_rg[i], moe_w_re[i], moe_b_re[i], moe_w_up[i], moe_w_down[i])
        fg = final_norm_g if i == depth - 1 else None
        xp = _moe(tok_pm, xp, g_moe, *mw, fg, f"moe_p{i}")
        xs = _moe(tok_s, xs, g_moe, *mw, fg, f"moe_s{i}")

    y_prompt = xp.reshape(nb, seq, D_MODEL)
    y_sample = xs.reshape(ndb, T_PAD, D_MODEL)[:, :n_t]
    return (y_prompt, y_sample, outs["nsa_kv_p"], outs["nsa_kv_s"], outs["nsa_win_p"], outs["nsa_win_s"],
            outs["dil_p0"], outs["dil_s0"], outs["dil_p1"], outs["dil_s1"], outs["dil_p2"], outs["dil_s2"],
            outs["gla_p"], outs["gla_s"], outs["hg_p"], outs["hg_s"])
```

```python
import functools

import jax
import jax.numpy as jnp
from jax import lax
from jax.experimental import pallas as pl
from jax.experimental.pallas import tpu as pltpu

F32 = jnp.float32
BF16 = jnp.bfloat16
EPS = 1e-6
D_MODEL = 1024
HD = 64
ADA_N = 6
MIB = 1024 * 1024
NEG_BIG = -1e30
MASK_BIAS = -2e30


def _params(sem, vmem_mib):
    return pltpu.CompilerParams(dimension_semantics=sem, vmem_limit_bytes=vmem_mib * MIB)


def _silu(x):
    return x * jax.nn.sigmoid(x)


def _bdot(a, b):
    return jnp.dot(a, b, preferred_element_type=F32)


def _dot_nt(a, b):
    return lax.dot_general(a, b, (((1,), (1,)), ((), ())), preferred_element_type=F32)


def _dot_tn(a, b):
    return lax.dot_general(a, b, (((0,), (0,)), ((), ())), preferred_element_type=F32)


def _split_hi_lo(x):
    hi = x.astype(BF16)
    lo = (x - hi.astype(F32)).astype(BF16)
    return hi, lo


def _rms(x):
    return x * lax.rsqrt(jnp.mean(x * x, axis=-1, keepdims=True) + EPS)


class _Tokens:
    def __init__(self, mod, per_row, nb, rows_per_b, tm):
        self.mod = mod
        self.per_row = per_row
        self.nb = nb
        self.rows_per_b = rows_per_b
        self.tm = tm

    def mod_spec(self, k):
        tm = self.tm
        if self.per_row:
            return pl.BlockSpec((1, tm, D_MODEL), lambda i, *_: (k, i, 0))
        nb, rpb = self.nb, self.rows_per_b
        return pl.BlockSpec((1, 1, D_MODEL), lambda i, *_: (k * nb + (i * tm) // rpb, 0, 0))


def _row_spec(tm, n):
    return pl.BlockSpec((tm, n), lambda i, *_: (i, 0))


def _const_spec(shape):
    nd = len(shape)
    return pl.BlockSpec(shape, lambda *_: (0,) * nd)


def _adaln_kernel(c_ref, w_ref, b_ref, o_ref):
    x = _silu(c_ref[...]).astype(BF16)
    o_ref[0] = _bdot(x, w_ref[0].astype(BF16)) + b_ref[0]


def _adaln(c_all, ada_w, ada_b):
    depth, _, n = ada_w.shape
    rows = c_all.shape[0]
    tn = 1536
    return pl.pallas_call(
        _adaln_kernel,
        grid=(depth, n // tn),
        in_specs=[
            pl.BlockSpec((rows, D_MODEL), lambda l, j: (0, 0)),
            pl.BlockSpec((1, D_MODEL, tn), lambda l, j: (l, 0, j)),
            pl.BlockSpec((1, 1, tn), lambda l, j: (l, 0, j)),
        ],
        out_specs=pl.BlockSpec((1, rows, tn), lambda l, j: (l, 0, j)),
        out_shape=jax.ShapeDtypeStruct((depth, rows, n), F32),
        compiler_params=_params(("arbitrary", "arbitrary"), 40),
        name="adaln",
    )(c_all, ada_w, ada_b.reshape(depth, 1, n))


def _modulate(x, g, shift, scale):
    return _rms(x) * g * (1.0 + scale) + shift


def _mod_proj_kernel(x_ref, g_ref, sh_ref, sc_ref, *refs, n_out):
    w_refs, o_refs = refs[:n_out], refs[n_out:]
    h = _modulate(x_ref[...], g_ref[...], sh_ref[0], sc_ref[0]).astype(BF16)
    for w_ref, o_ref in zip(w_refs, o_refs):
        o_ref[...] = _bdot(h, w_ref[...]).astype(o_ref.dtype)


def _mod_proj(tok, x, g, ws, dtypes, name):
    m = x.shape[0]
    tm = tok.tm
    return pl.pallas_call(
        functools.partial(_mod_proj_kernel, n_out=len(ws)),
        grid=(m // tm,),
        in_specs=[_row_spec(tm, D_MODEL), _const_spec((1, D_MODEL)), tok.mod_spec(0), tok.mod_spec(1)]
        + [_const_spec(w.shape) for w in ws],
        out_specs=[_row_spec(tm, w.shape[1]) for w in ws],
        out_shape=[jax.ShapeDtypeStruct((m, w.shape[1]), dt) for w, dt in zip(ws, dtypes)],
        compiler_params=_params(("arbitrary",), 56),
        name=name,
    )(x, g.reshape(1, D_MODEL), tok.mod, tok.mod, *ws)


def _out_proj_kernel(x_ref, gt_ref, *refs, mode, heads):
    if mode == "plain":
        a_ref, w_ref, o_ref = refs
        a = a_ref[...].astype(BF16)
    elif mode == "dil":
        o0, o1, o2, l0, l1, l2, w_ref, o_ref = refs
        la, lb, lc = l0[...], l1[...], l2[...]
        mx = jnp.maximum(jnp.maximum(la, lb), lc)
        wa, wb, wc = jnp.exp(la - mx), jnp.exp(lb - mx), jnp.exp(lc - mx)
        a = ((wa * o0[...] + wb * o1[...] + wc * o2[...]) / (wa + wb + wc)).astype(BF16)
    else:
        s_ref, r_ref, ng_ref, w_ref, o_ref = refs
        dv = s_ref.shape[1] // heads
        parts = []
        for h in range(heads):
            parts.append(_rms(s_ref[:, h * dv:(h + 1) * dv]) * ng_ref[...])
        a = (jnp.concatenate(parts, axis=1) * _silu(r_ref[...])).astype(BF16)
    o_ref[...] = x_ref[...] + gt_ref[0] * _bdot(a, w_ref[...])


def _out_proj(tok, x, ins, w, mode, name, heads=1, col_blocks=None):
    m = x.shape[0]
    tm = tok.tm
    in_specs = [_row_spec(tm, D_MODEL), tok.mod_spec(2)]
    for k, a in enumerate(ins):
        if a.shape[0] != m:
            in_specs.append(_const_spec(a.shape))
        elif col_blocks and col_blocks[k] is not None:
            width, blk = col_blocks[k]
            in_specs.append(pl.BlockSpec((tm, width), lambda i, blk=blk: (i, blk)))
        else:
            in_specs.append(_row_spec(tm, a.shape[1]))
    in_specs.append(_const_spec(w.shape))
    return pl.pallas_call(
        functools.partial(_out_proj_kernel, mode=mode, heads=heads),
        grid=(m // tm,),
        in_specs=in_specs,
        out_specs=_row_spec(tm, D_MODEL),
        out_shape=jax.ShapeDtypeStruct((m, D_MODEL), F32),
        compiler_params=_params(("arbitrary",), 48),
        name=name,
    )(x, tok.mod, *ins, w)


MOE_GROUPS = 4
MOE_PER_GROUP = 4
MOE_FF = 256
ROUTER_LANES = 128


def _router_gates(logits):
    lane = lax.broadcasted_iota(jnp.int32, logits.shape, 1)
    lane_f = lane.astype(F32)
    neg = -jnp.inf
    glog = jnp.where(lane < MOE_GROUPS, logits, neg)
    gmax = jnp.max(glog, axis=-1, keepdims=True)
    grp = jnp.min(jnp.where(glog == gmax, lane_f, 1e9), axis=-1, keepdims=True)
    p_grp = 1.0 / jnp.sum(jnp.exp(glog - gmax), axis=-1, keepdims=True)
    e_grp = ((lane - MOE_GROUPS) >> 2).astype(F32)
    n_e = MOE_GROUPS * MOE_PER_GROUP
    in_grp = (lane >= MOE_GROUPS) & (lane < MOE_GROUPS + n_e) & (e_grp == grp)
    e_in = jnp.where(in_grp, logits, neg)
    v1 = jnp.max(e_in, axis=-1, keepdims=True)
    i1 = jnp.min(jnp.where(e_in == v1, lane_f, 1e9), axis=-1, keepdims=True)
    e2 = jnp.where(lane_f == i1, neg, e_in)
    v2 = jnp.max(e2, axis=-1, keepdims=True)
    i2 = jnp.min(jnp.where(e2 == v2, lane_f, 1e9), axis=-1, keepdims=True)
    t = jnp.exp(v2 - v1)
    w1 = p_grp / (1.0 + t)
    w2 = p_grp * t / (1.0 + t)
    return jnp.where(lane_f == i1, w1, 0.0) + jnp.where(lane_f == i2, w2, 0.0)


def _moe_kernel(x_ref, g_ref, sh_ref, sc_ref, gt_ref, wrh_ref, wrl_ref, br_ref, up_ref, dn_ref, *rest, final):
    if final:
        fg_ref, o_ref, h_sc, gate_sc, acc_sc = rest
    else:
        o_ref, h_sc, gate_sc, acc_sc = rest
    grp = pl.program_id(1)

    @pl.when(grp == 0)
    def _():
        h = _modulate(x_ref[...], g_ref[...], sh_ref[0], sc_ref[0])
        hh, hl = _split_hi_lo(h)
        h_sc[...] = hh
        logits = _bdot(hh, wrh_ref[...]) + _bdot(hl, wrh_ref[...]) + _bdot(hh, wrl_ref[...]) + br_ref[...]
        gate_sc[...] = _router_gates(logits)
        acc_sc[...] = jnp.zeros_like(acc_sc)

    hid = _bdot(h_sc[...], up_ref[0])
    gates = gate_sc[...]
    lane = lax.broadcasted_iota(jnp.int32, gates.shape, 1)
    acts = []
    for e in range(MOE_PER_GROUP):
        col = jnp.sum(jnp.where(lane == MOE_GROUPS + MOE_PER_GROUP * grp + e, gates, 0.0), axis=-1, keepdims=True)
        a = hid[:, 2 * MOE_FF * e:2 * MOE_FF * e + MOE_FF]
        b = hid[:, 2 * MOE_FF * e + MOE_FF:2 * MOE_FF * (e + 1)]
        acts.append((_silu(a) * b * col).astype(BF16))
    acc_sc[...] += _bdot(jnp.concatenate(acts, axis=1), dn_ref[0])

    @pl.when(grp == MOE_GROUPS - 1)
    def _():
        y = x_ref[...] + gt_ref[0] * acc_sc[...]
        if final:
            y = _rms(y) * fg_ref[...]
        o_ref[...] = y


def _moe(tok, x, g, wr_hi, wr_lo, br, up, dn, final_g, name):
    m = x.shape[0]
    tm = tok.tm
    final = final_g is not None
    ins = [x, g.reshape(1, D_MODEL), tok.mod, tok.mod, tok.mod, wr_hi, wr_lo, br, up, dn]
    in_specs = [
        _row_spec(tm, D_MODEL), _const_spec((1, D_MODEL)), tok.mod_spec(3), tok.mod_spec(4), tok.mod_spec(5),
        _const_spec(wr_hi.shape), _const_spec(wr_lo.shape), _const_spec(br.shape),
        pl.BlockSpec((1,) + up.shape[1:], lambda i, e: (e, 0, 0)),
        pl.BlockSpec((1,) + dn.shape[1:], lambda i, e: (e, 0, 0)),
    ]
    if final:
        ins.append(final_g.reshape(1, D_MODEL))
        in_specs.append(_const_spec((1, D_MODEL)))
    return pl.pallas_call(
        functools.partial(_moe_kernel, final=final),
        grid=(m // tm, MOE_GROUPS),
        in_specs=in_specs,
        out_specs=_row_spec(tm, D_MODEL),
        out_shape=jax.ShapeDtypeStruct((m, D_MODEL), F32),
        scratch_shapes=[
            pltpu.VMEM((tm, D_MODEL), BF16),
            pltpu.VMEM((tm, ROUTER_LANES), F32),
            pltpu.VMEM((tm, D_MODEL), F32),
        ],
        compiler_params=_params(("arbitrary", "arbitrary"), 48),
        name=name,
    )(*ins)


NSA_KVH = 4
NSA_G = 4
NSA_CMP_LEN = 32
NSA_CMP_STRIDE = 16
NSA_CMP_HID = 128
NSA_SEL_LEN = 64
NSA_TOPN = 16
NSA_WIN = 512
NSA_FORCE_BONUS = 1000.0
KV_COLS = NSA_KVH * HD


def _compress_rows(src_ref, n_ch, w1_ref, pe_ref, w2_ref):
    hid = [jnp.zeros((n_ch, 2 * NSA_CMP_HID), F32) for _ in range(2)]
    for r in range(NSA_CMP_LEN):
        for h in range(2):
            rows = (src_ref[h, pl.ds(r, n_ch, stride=NSA_CMP_STRIDE), :] + pe_ref[r]).astype(BF16)
            hid[h] = hid[h] + _bdot(rows, w1_ref[r])
    outs = [_bdot(_silu(h).astype(BF16), w2_ref[...]) for h in hid]
    return jnp.concatenate(outs, axis=1)


def _masked_softmax_rows(s, mask):
    sm = jnp.where(mask, s, NEG_BIG)
    m = jnp.max(sm, axis=-1, keepdims=True)
    e = jnp.where(mask, jnp.exp(sm - m), 0.0)
    den = jnp.sum(e, axis=-1, keepdims=True)
    return e / jnp.where(den > 0, den, 1.0), m, den


def _top_blocks(score, n_valid, topn):
    lane = lax.broadcasted_iota(jnp.int32, score.shape, 1)
    rank = jnp.zeros(score.shape, F32)
    for i in range(n_valid):
        si = score[:, i:i + 1]
        ahead = (si > score) | ((si == score) & (lane > i))
        rank = rank + jnp.where(ahead, 1.0, 0.0)
    return jnp.where((rank < topn) & (lane < n_valid), 1.0, 0.0)


def _nsa_prompt_kernel(q_ref, kc_ref, vc_ref, ks_ref, vs_ref, kw_ref, vw_ref, gt_ref, ov_ref, ex_ref, o_ref,
                       ks_sc, vs_sc, kw_sc, vw_sc, *, tq, seq, chunk):
    qi = pl.program_id(1)
    t0 = qi * tq

    @pl.when(qi == 0)
    def _():
        ks_sc[...] = ks_ref[...].astype(BF16)
        vs_sc[...] = vs_ref[...].astype(BF16)
        kw_sc[...] = kw_ref[...].astype(BF16)
        vw_sc[...] = vw_ref[...].astype(BF16)

    n_cmp = kc_ref.shape[0]
    n_blk = seq // NSA_SEL_LEN
    jpad = ov_ref.shape[0]
    wkeys = NSA_WIN + tq
    rows = NSA_G * tq
    gates = jax.nn.sigmoid(gt_ref[...])
    tpos = t0 + lax.broadcasted_iota(jnp.int32, (tq, 1), 0)

    def heads3(x):
        return x.reshape(NSA_G, tq, x.shape[-1])

    c_end = lax.broadcasted_iota(jnp.int32, (tq, n_cmp), 1) * NSA_CMP_STRIDE + NSA_CMP_LEN - 1
    cmp_ok = (c_end <= tpos) & (c_end < (n_cmp - 1) * NSA_CMP_STRIDE + NSA_CMP_LEN - 1)
    cmp_bias = jnp.where(cmp_ok, 0.0, MASK_BIAS)
    has_cmp = jnp.where(tpos >= NSA_CMP_LEN - 1, 1.0, 0.0)
    w0 = pl.multiple_of(jnp.maximum(t0 - NSA_WIN, 0), tq)
    wdist = tpos - (w0 + lax.broadcasted_iota(jnp.int32, (tq, wkeys), 1))
    win_bias = jnp.where((wdist >= 0) & (wdist <= NSA_WIN), 0.0, MASK_BIAS)
    n_sel_chunks = (t0 + tq + chunk - 1) // chunk
    k_last = pl.multiple_of((n_sel_chunks - 1) * chunk, chunk)
    causal_bias = jnp.where(k_last + lax.broadcasted_iota(jnp.int32, (tq, chunk), 1) <= tpos, 0.0, MASK_BIAS)
    t_lane = t0 + lax.broadcasted_iota(jnp.int32, (jpad, tq), 1)
    j_idx = lax.broadcasted_iota(jnp.int32, (jpad, tq), 0)
    tb = t_lane // NSA_SEL_LEN
    forced = (j_idx == 0) | (j_idx == tb) | (j_idx == tb - 1)

    for kvh in range(NSA_KVH):
        c0 = kvh * HD
        qs = jnp.concatenate(
            [q_ref[:, (kvh * NSA_G + g) * HD:(kvh * NSA_G + g + 1) * HD] for g in range(NSA_G)], axis=0)
        qs = (qs * (HD ** -0.5)).astype(BF16)

        s = heads3(_dot_nt(qs, kc_ref[:, c0:c0 + HD].astype(BF16))) + cmp_bias[None]
        e = jnp.exp(s - jnp.max(s, axis=-1, keepdims=True))
        p_c = e * (has_cmp[None] / jnp.sum(e, axis=-1, keepdims=True))
        o_cmp = _bdot(p_c.reshape(rows, n_cmp).astype(BF16), vc_ref[:, c0:c0 + HD].astype(BF16))
        ph, plo = _split_hi_lo(jnp.sum(p_c, axis=0))
        imp = _dot_nt(ov_ref[...], ph) + _dot_nt(ov_ref[...], plo)
        score = jnp.where(j_idx <= tb, imp + jnp.where(forced, NSA_FORCE_BONUS, 0.0), -jnp.inf)
        rank = jnp.zeros((jpad, tq), F32)
        for i in range(n_blk):
            si = score[i:i + 1, :]
            rank = rank + jnp.where((si > score) | ((si == score) & (j_idx > i)), 1.0, 0.0)
        sel_bias = jnp.where((rank < NSA_TOPN) & (j_idx < n_blk), 0.0, MASK_BIAS).astype(BF16)

        def sel_chunk(ci, carry, extra_bias):
            m, l, acc = carry
            k0 = pl.multiple_of(ci * chunk, chunk)
            kch = ks_sc[pl.ds(k0, chunk), c0:c0 + HD]
            vch = vs_sc[pl.ds(k0, chunk), c0:c0 + HD]
            bias = _dot_tn(sel_bias, ex_ref[ci])
            if extra_bias is not None:
                bias = bias + extra_bias
            sc = heads3(_dot_nt(qs, kch)) + bias[None]
            m_new = jnp.maximum(m, jnp.max(sc, axis=-1, keepdims=True))
            p = jnp.exp(sc - m_new)
            alpha = jnp.exp(m - m_new)
            l = alpha * l + jnp.sum(p, axis=-1, keepdims=True)
            pv = _bdot(p.reshape(rows, chunk).astype(BF16), vch)
            acc = alpha * acc + heads3(pv)
            return m_new, l, acc

        init = (jnp.full((NSA_G, tq, 1), NEG_BIG, F32), jnp.zeros((NSA_G, tq, 1), F32),
                jnp.zeros((NSA_G, tq, HD), F32))
        carry = lax.fori_loop(0, n_sel_chunks - 1, lambda ci, c: sel_chunk(ci, c, None), init)
        _, l_s, acc_s = sel_chunk(n_sel_chunks - 1, carry, causal_bias)
        o_sel = acc_s / l_s

        kwin = kw_sc[pl.ds(w0, wkeys), c0:c0 + HD]
        vwin = vw_sc[pl.ds(w0, wkeys), c0:c0 + HD]
        sw = heads3(_dot_nt(qs, kwin)) + win_bias[None]
        ew = jnp.exp(sw - jnp.max(sw, axis=-1, keepdims=True))
        o_win = heads3(_bdot(ew.reshape(rows, wkeys).astype(BF16), vwin)) / jnp.sum(ew, axis=-1, keepdims=True)
        o_cmp = heads3(o_cmp)

        outs = []
        for g in range(NSA_G):
            h = kvh * NSA_G + g
            outs.append(gates[:, 3 * h:3 * h + 1] * o_cmp[g] + gates[:, 3 * h + 1:3 * h + 2] * o_sel[g]
                        + gates[:, 3 * h + 2:3 * h + 3] * o_win[g])
        o_ref[:, kvh * NSA_G * HD:(kvh + 1) * NSA_G * HD] = jnp.concatenate(outs, axis=1).astype(o_ref.dtype)


def _compress_weights(pe, w1, w2):
    eye = jnp.eye(2, dtype=F32)
    w1r = w1.reshape(2, NSA_CMP_LEN, HD, NSA_CMP_HID)
    w1bd = jnp.einsum("hg,srdj->srhdgj", eye, w1r).reshape(2, NSA_CMP_LEN, 2 * HD, 2 * NSA_CMP_HID)
    w2bd = jnp.einsum("hg,sjd->shjgd", eye, w2).reshape(2, 2 * NSA_CMP_HID, 2 * HD)
    pe2 = jnp.tile(pe, (1, 1, 2)).reshape(2, NSA_CMP_LEN, 1, 2 * HD)
    return w1bd.astype(BF16), pe2, w2bd.astype(BF16)


def _compress_prompt_kernel(src_ref, w1_ref, pe_ref, w2_ref, o_ref, stage_sc, *, n_ch):
    rows = NSA_CMP_STRIDE * n_ch
    for h in range(2):
        stage_sc[h, 0:rows, :] = src_ref[:, 128 * h:128 * (h + 1)]
        stage_sc[h, rows:rows + NSA_CMP_STRIDE, :] = jnp.zeros((NSA_CMP_STRIDE, 128), F32)
    o_ref[0] = _compress_rows(stage_sc, n_ch, w1_ref.at[0], pe_ref.at[0], w2_ref[0])


def _compress_prompt(kv4, nb, seq, cw):
    w1r, pe2, w2 = cw
    n_ch = seq // NSA_CMP_STRIDE
    return pl.pallas_call(
        functools.partial(_compress_prompt_kernel, n_ch=n_ch),
        grid=(2, nb),
        in_specs=[
            pl.BlockSpec((seq, KV_COLS), lambda s, b: (b, s)),
            pl.BlockSpec((1,) + w1r.shape[1:], lambda s, b: (s, 0, 0, 0)),
            pl.BlockSpec((1,) + pe2.shape[1:], lambda s, b: (s, 0, 0, 0)),
            pl.BlockSpec((1,) + w2.shape[1:], lambda s, b: (s, 0, 0)),
        ],
        out_specs=pl.BlockSpec((1, n_ch, KV_COLS), lambda s, b: (s, b, 0)),
        out_shape=jax.ShapeDtypeStruct((2, nb * n_ch, KV_COLS), F32),
        scratch_shapes=[pltpu.VMEM((2, seq + NSA_CMP_STRIDE, 128), F32)],
        compiler_params=_params(("arbitrary", "arbitrary"), 40),
        name="nsa_compress_prompt",
    )(kv4, w1r, pe2, w2)


def _overlap_matrix(n_cmp_pad, n_cmp, n_blk, lanes=128):
    c = jnp.arange(n_cmp_pad)[:, None] * NSA_CMP_STRIDE
    j = jnp.arange(lanes)[None, :] * NSA_SEL_LEN
    ok = (c < j + NSA_SEL_LEN) & (c + NSA_CMP_LEN > j)
    ok = ok & (jnp.arange(n_cmp_pad)[:, None] < n_cmp) & (jnp.arange(lanes)[None, :] < n_blk)
    return ok.astype(BF16)


def _expand_matrix(n_keys, chunk, lanes=128):
    key = jnp.arange(n_keys).reshape(n_keys // chunk, 1, chunk)
    j = jnp.arange(lanes).reshape(1, lanes, 1)
    return (key // NSA_SEL_LEN == j).astype(BF16)


def _nsa_prompt_attention(q, kc, vc, kv4, kvw, gates, nb, seq, tq=128, chunk=512):
    nq = seq // tq
    n_cmp = seq // NSA_CMP_STRIDE
    n_blk = seq // NSA_SEL_LEN
    jpad = -(-n_blk // 8) * 8
    ov = _overlap_matrix(n_cmp, n_cmp - 1, n_blk, jpad).T
    ex = _expand_matrix(seq, chunk, jpad)
    return pl.pallas_call(
        functools.partial(_nsa_prompt_kernel, tq=tq, seq=seq, chunk=chunk),
        grid=(nb, nq),
        in_specs=[
            pl.BlockSpec((tq, q.shape[1]), lambda b, i: (b * nq + i, 0)),
            pl.BlockSpec((n_cmp, KV_COLS), lambda b, i: (b, 0)),
            pl.BlockSpec((n_cmp, KV_COLS), lambda b, i: (b, 0)),
            pl.BlockSpec((seq, KV_COLS), lambda b, i: (b, 2)),
            pl.BlockSpec((seq, KV_COLS), lambda b, i: (b, 3)),
            pl.BlockSpec((seq, KV_COLS), lambda b, i: (b, 0)),
            pl.BlockSpec((seq, KV_COLS), lambda b, i: (b, 1)),
            pl.BlockSpec((tq, gates.shape[1]), lambda b, i: (b * nq + i, 0)),
            _const_spec(ov.shape),
            _const_spec(ex.shape),
        ],
        out_specs=pl.BlockSpec((tq, q.shape[1]), lambda b, i: (b * nq + i, 0)),
        out_shape=jax.ShapeDtypeStruct(q.shape, BF16),
        scratch_shapes=[pltpu.VMEM((seq, KV_COLS), BF16)] * 4,
        compiler_params=_params(("arbitrary", "arbitrary"), 48),
        name="nsa_prompt_attention",
    )(q, kc, vc, kv4, kv4, kvw, kvw, gates, ov, ex)


def _nsa_weights(w_in):
    nq = NSA_KVH * NSA_G * HD
    wg = jnp.pad(w_in[:, nq + 6 * KV_COLS:], ((0, 0), (0, 128 - 3 * NSA_KVH * NSA_G)))
    return [w_in[:, :nq].astype(BF16), w_in[:, nq:nq + 4 * KV_COLS].astype(BF16),
            w_in[:, nq + 4 * KV_COLS:nq + 6 * KV_COLS].astype(BF16), wg.astype(BF16)]


def _nsa_prompt(tok, x, g, nb, seq, w_in_parts, cw, w_out, name="nsa"):
    q, kv4, kvw, gates = _mod_proj(tok, x, g, w_in_parts, [F32, F32, F32, F32], name + "_proj")
    cmp_rows = _compress_prompt(kv4, nb, seq, cw)
    o = _nsa_prompt_attention(q, cmp_rows[0], cmp_rows[1], kv4, kvw, gates, nb, seq)
    x_new = _out_proj(tok, x, [o], w_out, "plain", name + "_out")
    return x_new, kv4, kvw


T_PAD = 8
PAGE_SIZE = 128
PAGES_PER_STEP = 16


def _softmax_two(s1, mask1, v1t, s2, mask2, v2):
    m = jnp.maximum(jnp.max(jnp.where(mask1, s1, NEG_BIG), axis=-1, keepdims=True),
                    jnp.max(jnp.where(mask2, s2, NEG_BIG), axis=-1, keepdims=True))
    p1 = jnp.where(mask1, jnp.exp(jnp.where(mask1, s1, NEG_BIG) - m), 0.0)
    p2 = jnp.where(mask2, jnp.exp(jnp.where(mask2, s2, NEG_BIG) - m), 0.0)
    den = jnp.sum(p1, axis=-1, keepdims=True) + jnp.sum(p2, axis=-1, keepdims=True)
    acc = _dot_nt(p1.astype(BF16), v1t) + _bdot(p2.astype(BF16), v2)
    return acc / jnp.where(den > 0, den, 1.0), m, den


def _stack_heads(q_ref, kvh):
    qs = jnp.concatenate(
        [q_ref[0, :, (kvh * NSA_G + g) * HD:(kvh * NSA_G + g + 1) * HD] for g in range(NSA_G)], axis=0)
    return (qs * (HD ** -0.5)).astype(BF16)


def _nsa_sample_cmp_kernel(pt_ref, *refs, n_pages, n_valid):
    del pt_ref
    pages = refs[:PAGES_PER_STEP]
    q_ref, w1_ref, pe_ref, w2_ref, ov_ref, ocmp_ref, sel_ref, stage_sc = refs[PAGES_PER_STEP:]
    step = pl.program_id(1)
    n_rows = n_pages * PAGE_SIZE
    n_ch = n_rows // NSA_CMP_STRIDE

    @pl.when(step == 0)
    def _():
        for h in range(4):
            stage_sc[h, n_rows:n_rows + NSA_CMP_STRIDE, :] = jnp.zeros((NSA_CMP_STRIDE, 128), F32)

    for k, page in enumerate(pages):
        r0 = pl.multiple_of((step * PAGES_PER_STEP + k) * PAGE_SIZE, PAGE_SIZE)
        for h in range(4):
            pair = [page[0, h // 2, 2 * (h % 2) + j].T for j in range(2)]
            stage_sc[h, pl.ds(r0, PAGE_SIZE), :] = jnp.concatenate(pair, axis=1)

    @pl.when(step == pl.num_programs(1) - 1)
    def _():
        kc = _compress_rows(stage_sc.at[0:2], n_ch, w1_ref.at[0], pe_ref.at[0], w2_ref[0]).astype(BF16)
        vc = _compress_rows(stage_sc.at[2:4], n_ch, w1_ref.at[1], pe_ref.at[1], w2_ref[1]).astype(BF16)
        tpos = n_rows + lax.broadcasted_iota(jnp.int32, (T_PAD, 1), 0)
        c_idx = lax.broadcasted_iota(jnp.int32, (T_PAD, n_ch), 1)
        cmp_mask = (c_idx * NSA_CMP_STRIDE + NSA_CMP_LEN - 1 <= tpos) & (c_idx < n_ch - 1)
        cmp_mask = jnp.concatenate([cmp_mask] * NSA_G, axis=0)
        lanes = sel_ref.shape[3]
        j_idx = lax.broadcasted_iota(jnp.int32, (T_PAD, lanes), 1)
        tb = tpos // NSA_SEL_LEN
        forced = (j_idx == 0) | (j_idx == tb) | (j_idx == tb - 1)
        n_blk = (n_rows + n_valid + NSA_SEL_LEN - 1) // NSA_SEL_LEN
        for kvh in range(NSA_KVH):
            c0 = kvh * HD
            qs = _stack_heads(q_ref, kvh)
            p_c, _, _ = _masked_softmax_rows(_dot_nt(qs, kc[:, c0:c0 + HD]), cmp_mask)
            o_cmp = _bdot(p_c.astype(BF16), vc[:, c0:c0 + HD])
            p_sum = p_c[0:T_PAD]
            for g in range(1, NSA_G):
                p_sum = p_sum + p_c[g * T_PAD:(g + 1) * T_PAD]
            ph, plo = _split_hi_lo(p_sum)
            imp = _bdot(ph, ov_ref[...]) + _bdot(plo, ov_ref[...])
            score = jnp.where(j_idx <= tb, imp + jnp.where(forced, NSA_FORCE_BONUS, 0.0), -jnp.inf)
            sel_ref[0, kvh] = _top_blocks(score, n_blk, NSA_TOPN)
            ocmp_ref[0, :, kvh * NSA_G * HD:(kvh + 1) * NSA_G * HD] = jnp.concatenate(
                [o_cmp[g * T_PAD:(g + 1) * T_PAD] for g in range(NSA_G)], axis=1)


def _rows_minor(a):
    nd = a.ndim
    return a.transpose((0,) + tuple(range(2, nd)) + (1,))


def _rows_major(a):
    nd = a.ndim
    return a.transpose((0, nd - 1) + tuple(range(1, nd - 1)))


def _page_specs(slot_pair):
    return [pl.BlockSpec((1, 2, NSA_KVH, HD, PAGE_SIZE),
                         lambda b, s, pt, k=k: (pt[b, s * PAGES_PER_STEP + k], slot_pair, 0, 0, 0))
            for k in range(PAGES_PER_STEP)]


def _nsa_sample_cmp(q, cache, page_table, cw, n_valid):
    w1r, pe2, w2 = cw
    nb, n_pages = page_table.shape
    n_rows = n_pages * PAGE_SIZE
    n_ch = n_rows // NSA_CMP_STRIDE
    n_blk = (n_rows + n_valid + NSA_SEL_LEN - 1) // NSA_SEL_LEN
    lanes = -(-n_blk // 128) * 128
    ov = _overlap_matrix(n_ch, n_ch - 1, n_blk, lanes)
    bspec = lambda shape: pl.BlockSpec(shape, lambda b, s, pt: (b,) + (0,) * (len(shape) - 1))
    cspec = lambda shape: pl.BlockSpec(shape, lambda b, s, pt: (0,) * len(shape))
    grid_spec = pltpu.PrefetchScalarGridSpec(
        num_scalar_prefetch=1,
        grid=(nb, n_pages // PAGES_PER_STEP),
        in_specs=_page_specs(0) + [bspec((1, T_PAD, q.shape[2])), cspec(w1r.shape), cspec(pe2.shape),
                                    cspec(w2.shape), cspec(ov.shape)],
        out_specs=[bspec((1, T_PAD, q.shape[2])), bspec((1, NSA_KVH, T_PAD, lanes))],
        scratch_shapes=[pltpu.VMEM((4, n_rows + NSA_CMP_STRIDE, 128), F32)],
    )
    return pl.pallas_call(
        functools.partial(_nsa_sample_cmp_kernel, n_pages=n_pages, n_valid=n_valid),
        grid_spec=grid_spec,
        out_shape=[jax.ShapeDtypeStruct(q.shape, F32), jax.ShapeDtypeStruct((nb, NSA_KVH, T_PAD, lanes), F32)],
        compiler_params=_params(("arbitrary", "arbitrary"), 56),
        name="nsa_sample_cmp",
    )(page_table, *([cache] * PAGES_PER_STEP), q, w1r, pe2, w2, ov)


def _nsa_sample_attend_kernel(pt_ref, *refs, n_pages, n_valid):
    del pt_ref
    pages = refs[:PAGES_PER_STEP]
    (q_ref, sel_ref, ocmp_ref, new4_ref, neww_ref, win_ref, gt_ref, ex_ref, o_ref, ks_sc, vs_sc) = refs[PAGES_PER_STEP:]
    step = pl.program_id(1)
    n_rows = n_pages * PAGE_SIZE

    for k, page in enumerate(pages):
        r0 = pl.multiple_of((step * PAGES_PER_STEP + k) * PAGE_SIZE, PAGE_SIZE)
        for slot, dst in enumerate((ks_sc, vs_sc)):
            for h in range(NSA_KVH):
                dst[h, :, pl.ds(r0, PAGE_SIZE)] = page[0, slot, h].astype(BF16)

    @pl.when(step == pl.num_programs(1) - 1)
    def _():
        def tile_g(x):
            return jnp.concatenate([x] * NSA_G, axis=0)

        lb = win_ref.shape[4]
        t_idx = lax.broadcasted_iota(jnp.int32, (T_PAD, 1), 0)
        i_new = lax.broadcasted_iota(jnp.int32, (T_PAD, T_PAD), 1)
        new_mask = tile_g((i_new <= t_idx) & (i_new < n_valid))
        w_idx = lax.broadcasted_iota(jnp.int32, (T_PAD, lb), 1)
        win_mask = tile_g(w_idx >= lb + t_idx - NSA_WIN)
        gates = jax.nn.sigmoid(gt_ref[0])
        for kvh in range(NSA_KVH):
            c0 = kvh * HD
            qs = _stack_heads(q_ref, kvh)
            chosen = tile_g(_bdot(sel_ref[0, kvh].astype(BF16), ex_ref[...]) > 0.5)
            k_new = new4_ref[0, :, 2 * KV_COLS + c0:2 * KV_COLS + c0 + HD].astype(BF16)
            v_new = new4_ref[0, :, 3 * KV_COLS + c0:3 * KV_COLS + c0 + HD].astype(BF16)
            o_sel, _, _ = _softmax_two(_bdot(qs, ks_sc[kvh]), chosen, vs_sc[kvh], _dot_nt(qs, k_new), new_mask, v_new)
            kw_old = win_ref[0, 0, kvh].astype(BF16)
            vw_old = win_ref[0, 1, kvh].astype(BF16)
            kw_new = neww_ref[0, :, c0:c0 + HD].astype(BF16)
            vw_new = neww_ref[0, :, KV_COLS + c0:KV_COLS + c0 + HD].astype(BF16)
            o_win, _, _ = _softmax_two(_bdot(qs, kw_old), win_mask, vw_old, _dot_nt(qs, kw_new), new_mask, vw_new)
            outs = []
            for g in range(NSA_G):
                h = kvh * NSA_G + g
                r = slice(g * T_PAD, (g + 1) * T_PAD)
                outs.append(gates[:, 3 * h:3 * h + 1] * ocmp_ref[0, :, h * HD:(h + 1) * HD]
                            + gates[:, 3 * h + 1:3 * h + 2] * o_sel[r] + gates[:, 3 * h + 2:3 * h + 3] * o_win[r])
            o_ref[0, :, kvh * NSA_G * HD:(kvh + 1) * NSA_G * HD] = jnp.concatenate(outs, axis=1).astype(o_ref.dtype)


def _nsa_sample_attend(q, sel, ocmp, new4, neww, win, gates, cache, page_table, n_valid):
    nb, n_pages = page_table.shape
    n_rows = n_pages * PAGE_SIZE
    lanes = sel.shape[3]
    ex = _expand_matrix(n_rows, n_rows, lanes)[0]
    bspec = lambda shape: pl.BlockSpec(shape, lambda b, s, pt: (b,) + (0,) * (len(shape) - 1))
    cspec = lambda shape: pl.BlockSpec(shape, lambda b, s, pt: (0,) * len(shape))
    grid_spec = pltpu.PrefetchScalarGridSpec(
        num_scalar_prefetch=1,
        grid=(nb, n_pages // PAGES_PER_STEP),
        in_specs=_page_specs(1) + [
            bspec((1, T_PAD, q.shape[2])), bspec((1,) + sel.shape[1:]), bspec((1, T_PAD, ocmp.shape[2])),
            bspec((1, T_PAD, new4.shape[2])), bspec((1, T_PAD, neww.shape[2])), bspec((1,) + win.shape[1:]),
            bspec((1, T_PAD, gates.shape[2])), cspec(ex.shape)],
        out_specs=bspec((1, T_PAD, q.shape[2])),
        scratch_shapes=[pltpu.VMEM((NSA_KVH, HD, n_rows), BF16)] * 2,
    )
    return pl.pallas_call(
        functools.partial(_nsa_sample_attend_kernel, n_pages=n_pages, n_valid=n_valid),
        grid_spec=grid_spec,
        out_shape=jax.ShapeDtypeStruct(q.shape, BF16),
        compiler_params=_params(("arbitrary", "arbitrary"), 56),
        name="nsa_sample_attend",
    )(page_table, *([cache] * PAGES_PER_STEP), q, sel, ocmp, new4, neww, win, gates, ex)


def _nsa_sample(tok, x, g, nb, cache, win, page_table, n_valid, w_in_parts, cw, w_out, name="nsa_s"):
    q, kv4, kvw, gates = _mod_proj(tok, x, g, w_in_parts, [F32, F32, F32, F32], name + "_proj")
    r3 = lambda a: a.reshape(nb, T_PAD, a.shape[1])
    cache_t, win_t = _rows_minor(cache), _rows_minor(win)
    ocmp, sel = _nsa_sample_cmp(r3(q), cache_t, page_table, cw, n_valid)
    o = _nsa_sample_attend(r3(q), sel, ocmp, r3(kv4), r3(kvw), win_t, r3(gates), cache_t, page_table, n_valid)
    x_new = _out_proj(tok, x, [o.reshape(nb * T_PAD, o.shape[2])], w_out, "plain", name + "_out")
    return x_new, r3(kv4), r3(kvw)


DIL_PAIRS = ((128, 1), (512, 4), (2048, 16))
DIL_SLOTS = 8
DIL_COLS = DIL_SLOTS * HD


def _dil_prompt_kernel(q_ref, k_ref, v_ref, o_ref, l_ref, *, tq, ls, nback):
    t0 = pl.program_id(2) * tq
    wkeys = min(nback + tq, ls)
    w0 = pl.multiple_of(jnp.maximum(t0 - nback, 0), tq)
    tpos = t0 + lax.broadcasted_iota(jnp.int32, (tq, 1), 0)
    dist = tpos - (w0 + lax.broadcasted_iota(jnp.int32, (tq, wkeys), 1))
    bias = jnp.where((dist >= 0) & (dist <= nback), 0.0, MASK_BIAS)
    outs, lses = [], []
    for h in range(DIL_SLOTS):
        c0 = h * HD
        qh = (q_ref[:, c0:c0 + HD] * (HD ** -0.5)).astype(BF16)
        kh = k_ref[pl.ds(w0, wkeys), c0:c0 + HD].astype(BF16)
        vh = v_ref[pl.ds(w0, wkeys), c0:c0 + HD].astype(BF16)
        o, lse = _biased_attention(qh, kh, vh, bias)
        outs.append(o)
        lses.append(jnp.broadcast_to(lse, (tq, HD)))
    o_ref[...] = jnp.concatenate(outs, axis=1)
    l_ref[...] = jnp.concatenate(lses, axis=1)


def _biased_attention(q, k, v, bias):
    s = _dot_nt(q, k) + bias
    m = jnp.max(s, axis=-1, keepdims=True)
    e = jnp.exp(s - m)
    den = jnp.sum(e, axis=-1, keepdims=True)
    return _bdot(e.astype(BF16), v) / den, m + jnp.log(den)


def _dil_prompt_rows16_kernel(q_ref, kv_ref, o_ref, l_ref, *, dil, nback):
    n_a = q_ref.shape[0]
    sets = 16 // dil
    a_q = lax.broadcasted_iota(jnp.int32, (n_a, n_a), 0)
    a_k = lax.broadcasted_iota(jnp.int32, (n_a, n_a), 1)
    for r in range(dil):
        for h in range(DIL_SLOTS):
            c0 = h * HD
            ks = [kv_ref[:, r + dil * e, c0:c0 + HD].astype(BF16) for e in range(sets)]
            vs = [kv_ref[:, r + dil * e, DIL_COLS + c0:DIL_COLS + c0 + HD].astype(BF16) for e in range(sets)]
            k_all = jnp.concatenate(ks, axis=0) if sets > 1 else ks[0]
            v_all = jnp.concatenate(vs, axis=0) if sets > 1 else vs[0]
            for e in range(sets):
                dist = jnp.concatenate([(sets * a_q + e) - (sets * a_k + e2) for e2 in range(sets)], axis=1) \
                    if sets > 1 else a_q - a_k
                bias = jnp.where((dist >= 0) & (dist <= nback), 0.0, MASK_BIAS)
                qh = (q_ref[:, r + dil * e, c0:c0 + HD] * (HD ** -0.5)).astype(BF16)
                o, lse = _biased_attention(qh, k_all, v_all, bias)
                o_ref[:, r + dil * e, c0:c0 + HD] = o
                l_ref[:, r + dil * e, c0:c0 + HD] = jnp.broadcast_to(lse, (n_a, HD))


def _dil_prompt_group_rows16(q3, kv, gi, nb, seq):
    window, dil = DIL_PAIRS[gi]
    n_a = seq // 16
    qv = q3.reshape(nb * n_a, 16, 3 * DIL_COLS)
    kvv = kv.reshape(nb * n_a, 16, 2 * DIL_COLS)
    out_sd = jax.ShapeDtypeStruct((nb * n_a, 16, DIL_COLS), F32)
    o, lse = pl.pallas_call(
        functools.partial(_dil_prompt_rows16_kernel, dil=dil, nback=window // dil),
        grid=(nb,),
        in_specs=[
            pl.BlockSpec((n_a, 16, DIL_COLS), lambda b: (b, 0, gi)),
            pl.BlockSpec((n_a, 16, 2 * DIL_COLS), lambda b: (b, 0, 0)),
        ],
        out_specs=[pl.BlockSpec((n_a, 16, DIL_COLS), lambda b: (b, 0, 0))] * 2,
        out_shape=[out_sd, out_sd],
        compiler_params=_params(("arbitrary",), 52),
        name=f"dil_prompt_g{gi}",
    )(qv, kvv)
    return o.reshape(nb * seq, DIL_COLS), lse.reshape(nb * seq, DIL_COLS)


def _dil_prompt_group(q3, kv, gi, nb, seq, tq=128):
    window, dil = DIL_PAIRS[gi]
    ls = seq // dil
    nq = ls // tq
    qv = q3.reshape(nb * ls, dil * 3 * DIL_COLS)
    kvv = kv.reshape(nb * ls, dil * 2 * DIL_COLS)
    out_sd = jax.ShapeDtypeStruct((nb * ls, dil * DIL_COLS), F32)
    o, lse = pl.pallas_call(
        functools.partial(_dil_prompt_kernel, tq=tq, ls=ls, nback=window // dil),
        grid=(nb, dil, nq),
        in_specs=[
            pl.BlockSpec((tq, DIL_COLS), lambda b, r, i: (b * nq + i, 3 * r + gi)),
            pl.BlockSpec((ls, DIL_COLS), lambda b, r, i: (b, 2 * r)),
            pl.BlockSpec((ls, DIL_COLS), lambda b, r, i: (b, 2 * r + 1)),
        ],
        out_specs=[pl.BlockSpec((tq, DIL_COLS), lambda b, r, i: (b * nq + i, r))] * 2,
        out_shape=[out_sd, out_sd],
        compiler_params=_params(("arbitrary", "arbitrary", "arbitrary"), 40),
        name=f"dil_prompt_g{gi}",
    )(qv, kvv, kvv)
    return o.reshape(nb * seq, DIL_COLS), lse.reshape(nb * seq, DIL_COLS)


def _dil_weights(w_in):
    n_g = len(DIL_PAIRS)
    w = w_in.reshape(D_MODEL, 3, n_g, DIL_COLS)
    parts = [w[:, 0].reshape(D_MODEL, n_g * DIL_COLS)]
    for gi in range(n_g):
        parts.append(jnp.concatenate([w[:, 1, gi], w[:, 2, gi]], axis=1))
    return [p.astype(BF16) for p in parts]


def _dil_prompt(tok, x, g, nb, seq, w_in_parts, w_out, name="dil"):
    q3, kv0, kv1, kv2 = _mod_proj(tok, x, g, w_in_parts, [F32] * 4, name + "_proj")
    kvs = (kv0, kv1, kv2)
    outs, lses = [], []
    for gi in range(len(DIL_PAIRS)):
        group = _dil_prompt_group_rows16 if DIL_PAIRS[gi][1] == 16 else _dil_prompt_group
        o, lse = group(q3, kvs[gi], gi, nb, seq)
        outs.append(o)
        lses.append(lse)
    x_new = _out_proj(tok, x, outs + lses, w_out, "dil", name + "_out")
    return x_new, kvs


def _dil_sample_kernel(q_ref, st_ref, new_ref, o_ref, l_ref, *, window, dil, n_valid):
    lb = st_ref.shape[4]
    t_idx = lax.broadcasted_iota(jnp.int32, (T_PAD, 1), 0)
    d_old = lb + t_idx - lax.broadcasted_iota(jnp.int32, (T_PAD, lb), 1)
    old_mask = ((d_old & (dil - 1)) == 0) & (d_old <= window)
    i_new = lax.broadcasted_iota(jnp.int32, (T_PAD, T_PAD), 1)
    d_new = t_idx - i_new
    new_mask = (d_new >= 0) & ((d_new & (dil - 1)) == 0) & (i_new < n_valid)
    outs, lses = [], []
    for h in range(DIL_SLOTS):
        c0 = h * HD
        qh = (q_ref[0, :, c0:c0 + HD] * (HD ** -0.5)).astype(BF16)
        k_old = st_ref[0, 0, h].astype(BF16)
        v_old = st_ref[0, 1, h].astype(BF16)
        k_new = new_ref[0, :, c0:c0 + HD].astype(BF16)
        v_new = new_ref[0, :, DIL_COLS + c0:DIL_COLS + c0 + HD].astype(BF16)
        o, m, den = _softmax_two(_bdot(qh, k_old), old_mask, v_old, _dot_nt(qh, k_new), new_mask, v_new)
        outs.append(o)
        lses.append(jnp.broadcast_to(m + jnp.log(den), (T_PAD, HD)))
    o_ref[0] = jnp.concatenate(outs, axis=1)
    l_ref[0] = jnp.concatenate(lses, axis=1)


def _dil_sample_group(q3, kv_new, state, gi, n_valid):
    window, dil = DIL_PAIRS[gi]
    nb, lb = state.shape[:2]
    state_t = _rows_minor(state)
    out_sd = jax.ShapeDtypeStruct((nb, T_PAD, DIL_COLS), F32)
    return pl.pallas_call(
        functools.partial(_dil_sample_kernel, window=window, dil=dil, n_valid=n_valid),
        grid=(nb,),
        in_specs=[
            pl.BlockSpec((1, T_PAD, DIL_COLS), lambda b: (b, 0, gi)),
            pl.BlockSpec((1, 2, DIL_SLOTS, HD, lb), lambda b: (b, 0, 0, 0, 0)),
            pl.BlockSpec((1, T_PAD, 2 * DIL_COLS), lambda b: (b, 0, 0)),
        ],
        out_specs=[pl.BlockSpec((1, T_PAD, DIL_COLS), lambda b: (b, 0, 0))] * 2,
        out_shape=[out_sd, out_sd],
        compiler_params=_params(("arbitrary",), 40),
        name=f"dil_sample_g{gi}",
    )(q3, state_t, kv_new)


def _dil_sample(tok, x, g, nb, states, n_valid, w_in_parts, w_out, name="dil_s"):
    q3, kv0, kv1, kv2 = _mod_proj(tok, x, g, w_in_parts, [F32] * 4, name + "_proj")
    r3 = lambda a: a.reshape(nb, T_PAD, a.shape[1])
    kvs = (r3(kv0), r3(kv1), r3(kv2))
    outs, lses = [], []
    for gi in range(len(DIL_PAIRS)):
        o, lse = _dil_sample_group(r3(q3), kvs[gi], states[gi], gi, n_valid)
        outs.append(o.reshape(nb * T_PAD, DIL_COLS))
        lses.append(lse.reshape(nb * T_PAD, DIL_COLS))
    x_new = _out_proj(tok, x, outs + lses, w_out, "dil", name + "_out")
    return x_new, kvs


GLA_H, GLA_DK, GLA_DV, GLA_RANK, GLA_TAU = 4, 128, 256, 16, 16.0
HG_H, HG_DK, HG_DV = 8, 128, 128
SCAN_SUB = 8
SCAN_HEADS_PER_STEP = 4
SCAN_UNROLL = 2


def _log_sigmoid(x):
    return jnp.minimum(x, 0.0) - jnp.log1p(jnp.exp(-jnp.abs(x)))


def _cumsum_rows(g):
    n = g.shape[0]
    tri = (lax.broadcasted_iota(jnp.int32, (n, n), 0) >= lax.broadcasted_iota(jnp.int32, (n, n), 1)).astype(BF16)
    hi = g.astype(BF16)
    r1 = g - hi.astype(F32)
    mid = r1.astype(BF16)
    lo = (r1 - mid.astype(F32)).astype(BF16)
    return _bdot(tri, hi) + _bdot(tri, mid) + _bdot(tri, lo)


def _scan_chunk(q, k, g, v, st):
    n, dk = q.shape
    sub = min(SCAN_SUB, n)
    b = _cumsum_rows(g)
    b_end = b[n - 1:n]
    o = _dot_nt((q * jnp.exp(b)).astype(BF16), st.astype(BF16))

    lane = lax.broadcasted_iota(jnp.int32, (sub, n), 1)
    row = lax.broadcasted_iota(jnp.int32, (sub, n), 0)
    ones = jnp.ones((dk, n), BF16)
    a_rows = []
    for i in range(n // sub):
        lo = i * sub
        qi, ki, bi = q[lo:lo + sub], k[lo:lo + sub], b[lo:lo + sub]
        prods = []
        for s in range(sub):
            e = jnp.exp(jnp.minimum(bi - bi[s:s + 1], 0.0))
            prods.append((qi * ki[s:s + 1] * e).astype(BF16))
        sums = _bdot(jnp.concatenate(prods, axis=0), ones)
        a_i = jnp.zeros((sub, n), F32)
        for s in range(sub):
            a_i = a_i + jnp.where((lane == lo + s) & (row >= s), sums[s * sub:(s + 1) * sub], 0.0)
        if i > 0:
            b_ref = b[lo - 1:lo]
            qd = (qi * jnp.exp(bi - b_ref)).astype(BF16)
            kd = (k[0:lo] * jnp.exp(b_ref - b[0:lo])).astype(BF16)
            if lo < n:
                kd = jnp.concatenate([kd, jnp.zeros((n - lo, dk), BF16)], axis=0)
            a_i = a_i + _dot_nt(qd, kd)
        a_rows.append(a_i)
    a = jnp.concatenate(a_rows, axis=0) if len(a_rows) > 1 else a_rows[0]
    o = o + _bdot(a.astype(BF16), v.astype(BF16))
    kd_end = (k * jnp.exp(b_end - b)).astype(BF16)
    st_new = st * jnp.exp(b_end) + _dot_tn(v.astype(BF16), kd_end)
    return o, st_new


def _scan_kernel(*refs, kind, chunk, n_chunks, n_valid, hpb, dk, dv):
    if kind == "gla":
        q_ref, k_ref, v_ref, a_ref, wa_ref, ba_ref, s0_ref, o_ref, sf_ref, st_sc = refs
    else:
        f_ref, v_ref, q_ref, lb_ref, s0_ref, o_ref, sf_ref, st_sc = refs
    ci = pl.program_id(2)

    @pl.when(ci == 0)
    def _():
        st_sc[...] = s0_ref[0]

    def body(j, carry):
        r0 = pl.multiple_of(j * chunk, chunk)
        rows = pl.ds(r0, chunk)
        for hh in range(hpb):
            kc = slice(hh * dk, (hh + 1) * dk)
            vc = slice(hh * dv, (hh + 1) * dv)
            if kind == "gla":
                q = q_ref[rows, kc] * (GLA_DK ** -0.5)
                k = k_ref[rows, kc]
                pre = _bdot(a_ref[rows, :].astype(BF16), wa_ref[:, kc]) + ba_ref[:, kc]
                g = _log_sigmoid(pre) * (1.0 / GLA_TAU)
            else:
                fz = f_ref[rows, kc]
                lb = lb_ref[:, kc]
                q = _silu(q_ref[rows, kc])
                la = jnp.log(lb)
                lc = jnp.log1p(-lb) + _log_sigmoid(fz)
                g = jnp.maximum(la, lc) + jnp.log1p(jnp.exp(-jnp.abs(la - lc)))
                k = (1.0 - lb) * jax.nn.sigmoid(-fz)
            if n_valid < chunk:
                live = lax.broadcasted_iota(jnp.int32, k.shape, 0) < n_valid
                k = jnp.where(live, k, 0.0)
                g = jnp.where(live, g, 0.0)
            o, st = _scan_chunk(q, k, g, v_ref[rows, vc], st_sc[hh])
            st_sc[hh] = st
            o_ref[rows, vc] = o
        return carry

    lax.fori_loop(0, n_chunks, body, 0, unroll=min(SCAN_UNROLL, n_chunks))

    @pl.when(ci == pl.num_programs(2) - 1)
    def _():
        sf_ref[0] = st_sc[...]


def _scan(kind, ins, s0_t, nb, rows_per_b, n_valid, name):
    if kind == "gla":
        heads, dk, dv = GLA_H, GLA_DK, GLA_DV
    else:
        heads, dk, dv = HG_H, HG_DK, HG_DV
    chunk = min(64, rows_per_b)
    blk = min(512, rows_per_b)
    nblk = rows_per_b // blk
    hpb = SCAN_HEADS_PER_STEP
    rspec = lambda w: pl.BlockSpec((blk, hpb * w), lambda b, h, i: (b * nblk + i, h))
    if kind == "gla":
        q, k, v, a, wa, ba = ins
        args = [q, k, v, a, wa, ba]
        in_specs = [rspec(dk), rspec(dk), rspec(dv),
                    pl.BlockSpec((blk, a.shape[1]), lambda b, h, i: (b * nblk + i, 0)),
                    pl.BlockSpec((wa.shape[0], hpb * dk), lambda b, h, i: (0, h)),
                    pl.BlockSpec((1, hpb * dk), lambda b, h, i: (0, h))]
    else:
        f, v, q, lb = ins
        args = [f, v, q, lb]
        in_specs = [rspec(dk), rspec(dv), rspec(dk), pl.BlockSpec((1, hpb * dk), lambda b, h, i: (0, h))]
    m = nb * rows_per_b
    st_spec = pl.BlockSpec((1, hpb, dv, dk), lambda b, h, i: (b, h, 0, 0))
    return pl.pallas_call(
        functools.partial(_scan_kernel, kind=kind, chunk=chunk, n_chunks=blk // chunk, n_valid=n_valid,
                          hpb=hpb, dk=dk, dv=dv),
        grid=(nb, heads // hpb, nblk),
        in_specs=in_specs + [st_spec],
        out_specs=[rspec(dv), st_spec],
        out_shape=[jax.ShapeDtypeStruct((m, heads * dv), F32), jax.ShapeDtypeStruct((nb, heads, dv, dk), F32)],
        scratch_shapes=[pltpu.VMEM((hpb, dv, dk), F32)],
        compiler_params=_params(("arbitrary", "arbitrary", "arbitrary"), 40),
        name=name,
    )(*args, s0_t)


def _gla_weights(w_in, w_a2, b_a2):
    nk, nv = GLA_H * GLA_DK, GLA_H * GLA_DV
    wa = jnp.pad(w_in[:, 2 * nk + 2 * nv:], ((0, 0), (0, 128 - GLA_RANK)))
    parts = [w_in[:, :nk], w_in[:, nk:2 * nk], w_in[:, 2 * nk:2 * nk + nv], w_in[:, 2 * nk + nv:2 * nk + 2 * nv], wa]
    wa2 = jnp.pad(w_a2, ((0, 128 - GLA_RANK), (0, 0))).astype(BF16)
    return [p.astype(BF16) for p in parts], wa2, b_a2.reshape(1, nk)


def _gla(tok, x, g, s0_t, nb, rows_per_b, n_valid, wts, norm_g, w_out, name="gla"):
    parts, wa2, ba2 = wts
    q, k, v, r, a = _mod_proj(tok, x, g, parts, [F32] * 5, name + "_proj")
    o, s_t = _scan("gla", [q, k, v, a, wa2, ba2], s0_t, nb, rows_per_b, n_valid, name + "_scan")
    x_new = _out_proj(tok, x, [o, r, norm_g.reshape(1, GLA_DV)], w_out, "heads", name + "_out", heads=GLA_H)
    return x_new, s_t


def _hgrn_weights(w_in, lb_logits, layer):
    nk, nv = HG_H * HG_DK, HG_H * HG_DV
    parts = [w_in[:, :nk], w_in[:, nk:nk + nv], w_in[:, nk + nv:2 * nk + nv], w_in[:, 2 * nk + nv:]]
    sm = jax.nn.softmax(lb_logits.astype(F32), axis=0)
    lb = jnp.sum(sm[1:layer + 1], axis=0).reshape(1, nk)
    return [p.astype(BF16) for p in parts], lb


def _hgrn(tok, x, g, s0_t, nb, rows_per_b, n_valid, wts, norm_g, w_out, name="hgrn"):
    parts, lb = wts
    f, i_in, q, og = _mod_proj(tok, x, g, parts, [F32] * 4, name + "_proj")
    o, s_t = _scan("hgrn", [f, i_in, q, lb], s0_t, nb, rows_per_b, n_valid, name + "_scan")
    x_new = _out_proj(tok, x, [o, og, norm_g.reshape(1, HG_DV)], w_out, "heads", name + "_out", heads=HG_H)
    return x_new, s_t


PROMPT_TM = 256
PROMPT_MOE_TM = 512


def _moe_weights(w_rg, b_rg, w_re, b_re, w_up, w_down):
    n_e = MOE_GROUPS * MOE_PER_GROUP
    wr = jnp.zeros((D_MODEL, ROUTER_LANES), F32).at[:, :MOE_GROUPS].set(w_rg).at[:, MOE_GROUPS:MOE_GROUPS + n_e].set(w_re)
    br = jnp.zeros((1, ROUTER_LANES), F32).at[0, :MOE_GROUPS].set(b_rg).at[0, MOE_GROUPS:MOE_GROUPS + n_e].set(b_re)
    wr_hi, wr_lo = _split_hi_lo(wr)
    up = w_up.reshape(MOE_GROUPS, MOE_PER_GROUP, D_MODEL, 2 * MOE_FF).transpose(0, 2, 1, 3)
    up = up.reshape(MOE_GROUPS, D_MODEL, MOE_PER_GROUP * 2 * MOE_FF).astype(BF16)
    dn = w_down.reshape(MOE_GROUPS, MOE_PER_GROUP * MOE_FF, D_MODEL).astype(BF16)
    return wr_hi, wr_lo, br, up, dn


def kernel(x_prompt, x_sample, cache_nsa_kv, state_nsa_win, state_dil_0, state_dil_1, state_dil_2, state_gla, state_hgrn, page_table, c_prompt, c_sample, nsa_w_in, nsa_cmp_pe, nsa_cmp_w1, nsa_cmp_w2, nsa_w_out, dil_w_in, dil_w_out, gla_w_in, gla_w_a2, gla_b_a2, gla_norm_g, gla_w_out, hg_w_in, hg_lb_logits, hg_norm_g, hg_w_out, norm_g, ada_w, ada_b, moe_w_rg, moe_b_rg, moe_w_re, moe_b_re, moe_w_up, moe_w_down, final_norm_g):
    nb, seq, _ = x_prompt.shape
    ndb, n_t, _ = x_sample.shape
    depth = ada_w.shape[0]
    ms_rows = ndb * T_PAD

    c_rows = -(-(nb + ndb) // 16) * 16
    c_all = jnp.zeros((c_rows, D_MODEL), F32).at[:nb].set(c_prompt).at[nb:nb + ndb].set(c_sample)
    mod = _adaln(c_all, ada_w, ada_b)

    xp = x_prompt.reshape(nb * seq, D_MODEL)
    xs = jnp.pad(x_sample, ((0, 0), (0, T_PAD - n_t), (0, 0))).reshape(ms_rows, D_MODEL)

    outs = {}
    for i in range(depth):
        mp = mod[i, :nb].reshape(nb, ADA_N, D_MODEL).transpose(1, 0, 2).reshape(ADA_N * nb, 1, D_MODEL)
        ms = mod[i, nb:nb + ndb].reshape(ndb, ADA_N, D_MODEL).transpose(1, 0, 2)
        ms = jnp.repeat(ms, T_PAD, axis=1)
        tok_p = _Tokens(mp, False, nb, seq, PROMPT_TM)
        tok_pm = _Tokens(mp, False, nb, seq, PROMPT_MOE_TM)
        tok_s = _Tokens(ms, True, 1, ms_rows, ms_rows)
        g_mix, g_moe = norm_g[i, 0], norm_g[i, 1]
        kind = i % 4
        if kind == 0:
            parts = _nsa_weights(nsa_w_in)
            cw = _compress_weights(nsa_cmp_pe, nsa_cmp_w1, nsa_cmp_w2)
            w_out = nsa_w_out.astype(BF16)
            xp, kv4, kvw = _nsa_prompt(tok_p, xp, g_mix, nb, seq, parts, cw, w_out)
            xs, kv4_s, kvw_s = _nsa_sample(tok_s, xs, g_mix, ndb, cache_nsa_kv, state_nsa_win, page_table, n_t,
                                           parts, cw, w_out)
            keep = min(NSA_WIN, seq)
            outs["nsa_kv_p"] = kv4.reshape(nb, seq, 4, NSA_KVH, HD)
            outs["nsa_kv_s"] = kv4_s[:, :n_t].reshape(ndb, n_t, 4, NSA_KVH, HD).astype(cache_nsa_kv.dtype)
            outs["nsa_win_p"] = kvw.reshape(nb, seq, 2, NSA_KVH, HD)[:, seq - keep:]
            win_all = jnp.concatenate(
                [state_nsa_win, kvw_s[:, :n_t].reshape(ndb, n_t, 2, NSA_KVH, HD).astype(state_nsa_win.dtype)], axis=1)
            outs["nsa_win_s"] = win_all[:, win_all.shape[1] - min(NSA_WIN, win_all.shape[1]):]
        elif kind == 1:
            parts = _dil_weights(dil_w_in)
            w_out = dil_w_out.astype(BF16)
            states = (state_dil_0, state_dil_1, state_dil_2)
            xp, kvs = _dil_prompt(tok_p, xp, g_mix, nb, seq, parts, w_out)
            xs, kvs_s = _dil_sample(tok_s, xs, g_mix, ndb, states, n_t, parts, w_out)
            for gi, (window, _) in enumerate(DIL_PAIRS):
                buf = kvs[gi].reshape(nb, seq, 2, DIL_SLOTS, HD)
                outs[f"dil_p{gi}"] = buf[:, seq - min(window, seq):]
                new = kvs_s[gi][:, :n_t].reshape(ndb, n_t, 2, DIL_SLOTS, HD).astype(states[gi].dtype)
                kv_all = jnp.concatenate([states[gi], new], axis=1)
                outs[f"dil_s{gi}"] = kv_all[:, kv_all.shape[1] - min(window, kv_all.shape[1]):]
        elif kind == 2:
            wts = _gla_weights(gla_w_in, gla_w_a2, gla_b_a2)
            w_out = gla_w_out.astype(BF16)
            zero = jnp.zeros((nb, GLA_H, GLA_DV, GLA_DK), F32)
            xp, s_p = _gla(tok_p, xp, g_mix, zero, nb, seq, seq, wts, gla_norm_g, w_out)
            xs, s_s = _gla(tok_s, xs, g_mix, state_gla.astype(F32).transpose(0, 1, 3, 2), ndb, T_PAD, n_t, wts,
                           gla_norm_g, w_out, name="gla_s")
            outs["gla_p"] = s_p.transpose(0, 1, 3, 2).astype(state_gla.dtype)
            outs["gla_s"] = s_s.transpose(0, 1, 3, 2).astype(state_gla.dtype)
        else:
            wts = _hgrn_weights(hg_w_in, hg_lb_logits, i)
            w_out = hg_w_out.astype(BF16)
            zero = jnp.zeros((nb, HG_H, HG_DV, HG_DK), F32)
            xp, s_p = _hgrn(tok_p, xp, g_mix, zero, nb, seq, seq, wts, hg_norm_g, w_out)
            xs, s_s = _hgrn(tok_s, xs, g_mix, state_hgrn.astype(F32).transpose(0, 1, 3, 2), ndb, T_PAD, n_t, wts,
                            hg_norm_g, w_out, name="hgrn_s")
            outs["hg_p"] = s_p.transpose(0, 1, 3, 2).astype(state_hgrn.dtype)
            outs["hg_s"] = s_s.transpose(0, 1, 3, 2).astype(state_hgrn.dtype)
        mw = _moe_weights(moe_w_rg[i], moe_b_rg[i], moe_w_re[i], moe_b_re[i], moe_w_up[i], moe_w_down[i])
        fg = final_norm_g if i == depth - 1 else None
        xp = _moe(tok_pm, xp, g_moe, *mw, fg, f"moe_p{i}")
        xs = _moe(tok_s, xs, g_moe, *mw, fg, f"moe_s{i}")

    y_prompt = xp.reshape(nb, seq, D_MODEL)
    y_sample = xs.reshape(ndb, T_PAD, D_MODEL)[:, :n_t]
    return (y_prompt, y_sample, outs["nsa_kv_p"], outs["nsa_kv_s"], outs["nsa_win_p"], outs["nsa_win_s"],
            outs["dil_p0"], outs["dil_s0"], outs["dil_p1"], outs["dil_s1"], outs["dil_p2"], outs["dil_s2"],
            outs["gla_p"], outs["gla_s"], outs["hg_p"], outs["hg_s"])
```

```python
import functools

import jax
import jax.numpy as jnp
from jax import lax
from jax.experimental import pallas as pl
from jax.experimental.pallas import tpu as pltpu

F32 = jnp.float32
BF16 = jnp.bfloat16
EPS = 1e-6
D_MODEL = 1024
HD = 64
ADA_N = 6
MIB = 1024 * 1024
NEG_BIG = -1e30
MASK_BIAS = -2e30


def _params(sem, vmem_mib):
    return pltpu.CompilerParams(dimension_semantics=sem, vmem_limit_bytes=vmem_mib * MIB)


def _silu(x):
    return x * jax.nn.sigmoid(x)


def _bdot(a, b):
    return jnp.dot(a, b, preferred_element_type=F32)


def _dot_nt(a, b):
    return lax.dot_general(a, b, (((1,), (1,)), ((), ())), preferred_element_type=F32)


def _dot_tn(a, b):
    return lax.dot_general(a, b, (((0,), (0,)), ((), ())), preferred_element_type=F32)


def _split_hi_lo(x):
    hi = x.astype(BF16)
    lo = (x - hi.astype(F32)).astype(BF16)
    return hi, lo


def _rms(x):
    return x * lax.rsqrt(jnp.mean(x * x, axis=-1, keepdims=True) + EPS)


class _Tokens:
    def __init__(self, mod, per_row, nb, rows_per_b, tm):
        self.mod = mod
        self.per_row = per_row
        self.nb = nb
        self.rows_per_b = rows_per_b
        self.tm = tm

    def mod_spec(self, k):
        tm = self.tm
        if self.per_row:
            return pl.BlockSpec((1, tm, D_MODEL), lambda i, *_: (k, i, 0))
        nb, rpb = self.nb, self.rows_per_b
        return pl.BlockSpec((1, 1, D_MODEL), lambda i, *_: (k * nb + (i * tm) // rpb, 0, 0))


def _row_spec(tm, n):
    return pl.BlockSpec((tm, n), lambda i, *_: (i, 0))


def _const_spec(shape):
    nd = len(shape)
    return pl.BlockSpec(shape, lambda *_: (0,) * nd)


def _adaln_kernel(c_ref, w_ref, b_ref, o_ref):
    x = _silu(c_ref[...]).astype(BF16)
    o_ref[0] = _bdot(x, w_ref[0].astype(BF16)) + b_ref[0]


def _adaln(c_all, ada_w, ada_b):
    depth, _, n = ada_w.shape
    rows = c_all.shape[0]
    tn = 1536
    return pl.pallas_call(
        _adaln_kernel,
        grid=(depth, n // tn),
        in_specs=[
            pl.BlockSpec((rows, D_MODEL), lambda l, j: (0, 0)),
            pl.BlockSpec((1, D_MODEL, tn), lambda l, j: (l, 0, j)),
            pl.BlockSpec((1, 1, tn), lambda l, j: (l, 0, j)),
        ],
        out_specs=pl.BlockSpec((1, rows, tn), lambda l, j: (l, 0, j)),
        out_shape=jax.ShapeDtypeStruct((depth, rows, n), F32),
        compiler_params=_params(("arbitrary", "arbitrary"), 40),
        name="adaln",
    )(c_all, ada_w, ada_b.reshape(depth, 1, n))


def _modulate(x, g, shift, scale):
    return _rms(x) * g * (1.0 + scale) + shift


def _mod_proj_kernel(x_ref, g_ref, sh_ref, sc_ref, *refs, n_out, rows_minor):
    w_refs, o_refs, t_refs = refs[:n_out], refs[n_out:2 * n_out], list(refs[2 * n_out:])
    h = _modulate(x_ref[...], g_ref[...], sh_ref[0], sc_ref[0]).astype(BF16)
    for k, (w_ref, o_ref) in enumerate(zip(w_refs, o_refs)):
        z = _bdot(h, w_ref[...])
        o_ref[...] = z.astype(o_ref.dtype)
        if k in rows_minor:
            t_refs.pop(0)[0] = z.T


def _mod_proj(tok, x, g, ws, dtypes, name, rows_minor=()):
    m = x.shape[0]
    tm = tok.tm
    out_specs = [_row_spec(tm, w.shape[1]) for w in ws]
    out_shape = [jax.ShapeDtypeStruct((m, w.shape[1]), dt) for w, dt in zip(ws, dtypes)]
    tiles = tok.rows_per_b // tm
    for k in rows_minor:
        cols = ws[k].shape[1]
        out_specs.append(pl.BlockSpec((1, cols, tm), lambda i: (i // tiles, 0, i % tiles)))
        out_shape.append(jax.ShapeDtypeStruct((tok.nb, cols, tok.rows_per_b), F32))
    return pl.pallas_call(
        functools.partial(_mod_proj_kernel, n_out=len(ws), rows_minor=tuple(rows_minor)),
        grid=(m // tm,),
        in_specs=[_row_spec(tm, D_MODEL), _const_spec((1, D_MODEL)), tok.mod_spec(0), tok.mod_spec(1)]
        + [_const_spec(w.shape) for w in ws],
        out_specs=out_specs,
        out_shape=out_shape,
        compiler_params=_params(("arbitrary",), 56),
        name=name,
    )(x, g.reshape(1, D_MODEL), tok.mod, tok.mod, *ws)


def _out_proj_kernel(x_ref, gt_ref, *refs, mode, heads):
    if mode == "plain":
        a_ref, w_ref, o_ref = refs
        a = a_ref[...].astype(BF16)
    elif mode == "dil":
        o0, o1, o2, l0, l1, l2, w_ref, o_ref = refs
        la, lb, lc = l0[...], l1[...], l2[...]
        mx = jnp.maximum(jnp.maximum(la, lb), lc)
        wa, wb, wc = jnp.exp(la - mx), jnp.exp(lb - mx), jnp.exp(lc - mx)
        a = ((wa * o0[...] + wb * o1[...] + wc * o2[...]) / (wa + wb + wc)).astype(BF16)
    else:
        s_ref, r_ref, ng_ref, w_ref, o_ref = refs
        dv = s_ref.shape[1] // heads
        parts = []
        for h in range(heads):
            parts.append(_rms(s_ref[:, h * dv:(h + 1) * dv]) * ng_ref[...])
        a = (jnp.concatenate(parts, axis=1) * _silu(r_ref[...])).astype(BF16)
    o_ref[...] = x_ref[...] + gt_ref[0] * _bdot(a, w_ref[...])


def _out_proj(tok, x, ins, w, mode, name, heads=1, col_blocks=None):
    m = x.shape[0]
    tm = tok.tm
    in_specs = [_row_spec(tm, D_MODEL), tok.mod_spec(2)]
    for k, a in enumerate(ins):
        if a.shape[0] != m:
            in_specs.append(_const_spec(a.shape))
        elif col_blocks and col_blocks[k] is not None:
            width, blk = col_blocks[k]
            in_specs.append(pl.BlockSpec((tm, width), lambda i, blk=blk: (i, blk)))
        else:
            in_specs.append(_row_spec(tm, a.shape[1]))
    in_specs.append(_const_spec(w.shape))
    return pl.pallas_call(
        functools.partial(_out_proj_kernel, mode=mode, heads=heads),
        grid=(m // tm,),
        in_specs=in_specs,
        out_specs=_row_spec(tm, D_MODEL),
        out_shape=jax.ShapeDtypeStruct((m, D_MODEL), F32),
        compiler_params=_params(("arbitrary",), 48),
        name=name,
    )(x, tok.mod, *ins, w)


MOE_GROUPS = 4
MOE_PER_GROUP = 4
MOE_FF = 256
ROUTER_LANES = 128


def _router_gates(logits):
    lane = lax.broadcasted_iota(jnp.int32, logits.shape, 1)
    lane_f = lane.astype(F32)
    neg = -jnp.inf
    glog = jnp.where(lane < MOE_GROUPS, logits, neg)
    gmax = jnp.max(glog, axis=-1, keepdims=True)
    grp = jnp.min(jnp.where(glog == gmax, lane_f, 1e9), axis=-1, keepdims=True)
    p_grp = 1.0 / jnp.sum(jnp.exp(glog - gmax), axis=-1, keepdims=True)
    e_grp = ((lane - MOE_GROUPS) >> 2).astype(F32)
    n_e = MOE_GROUPS * MOE_PER_GROUP
    in_grp = (lane >= MOE_GROUPS) & (lane < MOE_GROUPS + n_e) & (e_grp == grp)
    e_in = jnp.where(in_grp, logits, neg)
    v1 = jnp.max(e_in, axis=-1, keepdims=True)
    i1 = jnp.min(jnp.where(e_in == v1, lane_f, 1e9), axis=-1, keepdims=True)
    e2 = jnp.where(lane_f == i1, neg, e_in)
    v2 = jnp.max(e2, axis=-1, keepdims=True)
    i2 = jnp.min(jnp.where(e2 == v2, lane_f, 1e9), axis=-1, keepdims=True)
    t = jnp.exp(v2 - v1)
    w1 = p_grp / (1.0 + t)
    w2 = p_grp * t / (1.0 + t)
    return jnp.where(lane_f == i1, w1, 0.0) + jnp.where(lane_f == i2, w2, 0.0)


def _moe_kernel(x_ref, g_ref, sh_ref, sc_ref, gt_ref, wrh_ref, wrl_ref, br_ref, up_ref, dn_ref, *rest, final):
    if final:
        fg_ref, o_ref, h_sc, gate_sc, acc_sc = rest
    else:
        o_ref, h_sc, gate_sc, acc_sc = rest
    grp = pl.program_id(1)

    @pl.when(grp == 0)
    def _():
        h = _modulate(x_ref[...], g_ref[...], sh_ref[0], sc_ref[0])
        hh, hl = _split_hi_lo(h)
        h_sc[...] = hh
        logits = _bdot(hh, wrh_ref[...]) + _bdot(hl, wrh_ref[...]) + _bdot(hh, wrl_ref[...]) + br_ref[...]
        gate_sc[...] = _router_gates(logits)
        acc_sc[...] = jnp.zeros_like(acc_sc)

    hid = _bdot(h_sc[...], up_ref[0])
    gates = gate_sc[...]
    lane = lax.broadcasted_iota(jnp.int32, gates.shape, 1)
    acts = []
    for e in range(MOE_PER_GROUP):
        col = jnp.sum(jnp.where(lane == MOE_GROUPS + MOE_PER_GROUP * grp + e, gates, 0.0), axis=-1, keepdims=True)
        a = hid[:, 2 * MOE_FF * e:2 * MOE_FF * e + MOE_FF]
        b = hid[:, 2 * MOE_FF * e + MOE_FF:2 * MOE_FF * (e + 1)]
        acts.append((_silu(a) * b * col).astype(BF16))
    acc_sc[...] += _bdot(jnp.concatenate(acts, axis=1), dn_ref[0])

    @pl.when(grp == MOE_GROUPS - 1)
    def _():
        y = x_ref[...] + gt_ref[0] * acc_sc[...]
        if final:
            y = _rms(y) * fg_ref[...]
        o_ref[...] = y


def _moe(tok, x, g, wr_hi, wr_lo, br, up, dn, final_g, name):
    m = x.shape[0]
    tm = tok.tm
    final = final_g is not None
    ins = [x, g.reshape(1, D_MODEL), tok.mod, tok.mod, tok.mod, wr_hi, wr_lo, br, up, dn]
    in_specs = [
        _row_spec(tm, D_MODEL), _const_spec((1, D_MODEL)), tok.mod_spec(3), tok.mod_spec(4), tok.mod_spec(5),
        _const_spec(wr_hi.shape), _const_spec(wr_lo.shape), _const_spec(br.shape),
        pl.BlockSpec((1,) + up.shape[1:], lambda i, e: (e, 0, 0)),
        pl.BlockSpec((1,) + dn.shape[1:], lambda i, e: (e, 0, 0)),
    ]
    if final:
        ins.append(final_g.reshape(1, D_MODEL))
        in_specs.append(_const_spec((1, D_MODEL)))
    return pl.pallas_call(
        functools.partial(_moe_kernel, final=final),
        grid=(m // tm, MOE_GROUPS),
        in_specs=in_specs,
        out_specs=_row_spec(tm, D_MODEL),
        out_shape=jax.ShapeDtypeStruct((m, D_MODEL), F32),
        scratch_shapes=[
            pltpu.VMEM((tm, D_MODEL), BF16),
            pltpu.VMEM((tm, ROUTER_LANES), F32),
            pltpu.VMEM((tm, D_MODEL), F32),
        ],
        compiler_params=_params(("arbitrary", "arbitrary"), 48),
        name=name,
    )(*ins)


NSA_KVH = 4
NSA_G = 4
NSA_CMP_LEN = 32
NSA_CMP_STRIDE = 16
NSA_CMP_HID = 128
NSA_SEL_LEN = 64
NSA_TOPN = 16
NSA_WIN = 512
NSA_FORCE_BONUS = 1000.0
KV_COLS = NSA_KVH * HD
CMP_TAIL = 8 * NSA_CMP_STRIDE


def _compress_rows(src_ref, n_ch, w1_ref, pe_ref, w2_ref):
    hid = [jnp.zeros((n_ch, 2 * NSA_CMP_HID), F32) for _ in range(2)]
    for r in range(NSA_CMP_STRIDE):
        for h in range(2):
            rows = src_ref[h, pl.ds(r, n_ch + 8, stride=NSA_CMP_STRIDE), :]
            lo = (rows[0:n_ch] + pe_ref[r]).astype(BF16)
            hi = (rows[1:n_ch + 1] + pe_ref[r + NSA_CMP_STRIDE]).astype(BF16)
            hid[h] = hid[h] + _bdot(lo, w1_ref[r]) + _bdot(hi, w1_ref[r + NSA_CMP_STRIDE])
    outs = [_bdot(_silu(h).astype(BF16), w2_ref[...]) for h in hid]
    return jnp.concatenate(outs, axis=1)


def _masked_softmax_rows(s, mask):
    sm = jnp.where(mask, s, NEG_BIG)
    m = jnp.max(sm, axis=-1, keepdims=True)
    e = jnp.where(mask, jnp.exp(sm - m), 0.0)
    den = jnp.sum(e, axis=-1, keepdims=True)
    return e / jnp.where(den > 0, den, 1.0), m, den


def _top_blocks(score, n_valid, topn):
    lane = lax.broadcasted_iota(jnp.int32, score.shape, 1)
    rank = jnp.zeros(score.shape, F32)
    for i in range(n_valid):
        si = score[:, i:i + 1]
        ahead = (si > score) | ((si == score) & (lane > i))
        rank = rank + jnp.where(ahead, 1.0, 0.0)
    return jnp.where((rank < topn) & (lane < n_valid), 1.0, 0.0)


def _nsa_prompt_kernel(q_ref, kc_ref, vc_ref, ks_ref, vs_ref, kw_ref, vw_ref, gt_ref, ov_ref, ex_ref, o_ref,
                       ks_sc, vs_sc, kw_sc, vw_sc, *, tq, seq, chunk):
    qi = pl.program_id(1)
    t0 = qi * tq

    @pl.when(qi == 0)
    def _():
        ks_sc[...] = ks_ref[...].astype(BF16)
        vs_sc[...] = vs_ref[...].astype(BF16)
        kw_sc[...] = kw_ref[...].astype(BF16)
        vw_sc[...] = vw_ref[...].astype(BF16)

    n_cmp = kc_ref.shape[0]
    n_blk = seq // NSA_SEL_LEN
    jpad = ov_ref.shape[0]
    wkeys = NSA_WIN + tq
    rows = NSA_G * tq
    gates = jax.nn.sigmoid(gt_ref[...])
    tpos = t0 + lax.broadcasted_iota(jnp.int32, (tq, 1), 0)

    def heads3(x):
        return x.reshape(NSA_G, tq, x.shape[-1])

    c_end = lax.broadcasted_iota(jnp.int32, (tq, n_cmp), 1) * NSA_CMP_STRIDE + NSA_CMP_LEN - 1
    cmp_ok = (c_end <= tpos) & (c_end < (n_cmp - 1) * NSA_CMP_STRIDE + NSA_CMP_LEN - 1)
    cmp_bias = jnp.where(cmp_ok, 0.0, MASK_BIAS)
    has_cmp = jnp.where(tpos >= NSA_CMP_LEN - 1, 1.0, 0.0)
    w0 = pl.multiple_of(jnp.maximum(t0 - NSA_WIN, 0), tq)
    wdist = tpos - (w0 + lax.broadcasted_iota(jnp.int32, (tq, wkeys), 1))
    win_bias = jnp.where((wdist >= 0) & (wdist <= NSA_WIN), 0.0, MASK_BIAS)
    n_sel_chunks = (t0 + tq + chunk - 1) // chunk
    k_last = pl.multiple_of((n_sel_chunks - 1) * chunk, chunk)
    causal_bias = jnp.where(k_last + lax.broadcasted_iota(jnp.int32, (tq, chunk), 1) <= tpos, 0.0, MASK_BIAS)
    t_lane = t0 + lax.broadcasted_iota(jnp.int32, (jpad, tq), 1)
    j_idx = lax.broadcasted_iota(jnp.int32, (jpad, tq), 0)
    tb = t_lane // NSA_SEL_LEN
    forced = (j_idx == 0) | (j_idx == tb) | (j_idx == tb - 1)

    for kvh in range(NSA_KVH):
        c0 = kvh * HD
        qs = jnp.concatenate(
            [q_ref[:, (kvh * NSA_G + g) * HD:(kvh * NSA_G + g + 1) * HD] for g in range(NSA_G)], axis=0)
        qs = (qs * (HD ** -0.5)).astype(BF16)

        s = heads3(_dot_nt(qs, kc_ref[:, c0:c0 + HD].astype(BF16))) + cmp_bias[None]
        e = jnp.exp(s - jnp.max(s, axis=-1, keepdims=True))
        p_c = e * (has_cmp[None] / jnp.sum(e, axis=-1, keepdims=True))
        o_cmp = _bdot(p_c.reshape(rows, n_cmp).astype(BF16), vc_ref[:, c0:c0 + HD].astype(BF16))
        ph, plo = _split_hi_lo(jnp.sum(p_c, axis=0))
        imp = _dot_nt(ov_ref[...], ph) + _dot_nt(ov_ref[...], plo)
        score = jnp.where(j_idx <= tb, imp + jnp.where(forced, NSA_FORCE_BONUS, 0.0), -jnp.inf)
        rank = jnp.zeros((jpad, tq), F32)
        for i in range(n_blk):
            si = score[i:i + 1, :]
            rank = rank + jnp.where((si > score) | ((si == score) & (j_idx > i)), 1.0, 0.0)
        sel_bias = jnp.where((rank < NSA_TOPN) & (j_idx < n_blk), 0.0, MASK_BIAS).astype(BF16)

        def sel_chunk(ci, carry, extra_bias):
            m, l, acc = carry
            k0 = pl.multiple_of(ci * chunk, chunk)
            kch = ks_sc[pl.ds(k0, chunk), c0:c0 + HD]
            vch = vs_sc[pl.ds(k0, chunk), c0:c0 + HD]
            bias = _dot_tn(sel_bias, ex_ref[ci])
            if extra_bias is not None:
                bias = bias + extra_bias
            sc = heads3(_dot_nt(qs, kch)) + bias[None]
            m_new = jnp.maximum(m, jnp.max(sc, axis=-1, keepdims=True))
            p = jnp.exp(sc - m_new)
            alpha = jnp.exp(m - m_new)
            l = alpha * l + jnp.sum(p, axis=-1, keepdims=True)
            pv = _bdot(p.reshape(rows, chunk).astype(BF16), vch)
            acc = alpha * acc + heads3(pv)
            return m_new, l, acc

        init = (jnp.full((NSA_G, tq, 1), NEG_BIG, F32), jnp.zeros((NSA_G, tq, 1), F32),
                jnp.zeros((NSA_G, tq, HD), F32))
        carry = lax.fori_loop(0, n_sel_chunks - 1, lambda ci, c: sel_chunk(ci, c, None), init)
        _, l_s, acc_s = sel_chunk(n_sel_chunks - 1, carry, causal_bias)
        o_sel = acc_s / l_s

        kwin = kw_sc[pl.ds(w0, wkeys), c0:c0 + HD]
        vwin = vw_sc[pl.ds(w0, wkeys), c0:c0 + HD]
        sw = heads3(_dot_nt(qs, kwin)) + win_bias[None]
        ew = jnp.exp(sw - jnp.max(sw, axis=-1, keepdims=True))
        o_win = heads3(_bdot(ew.reshape(rows, wkeys).astype(BF16), vwin)) / jnp.sum(ew, axis=-1, keepdims=True)
        o_cmp = heads3(o_cmp)

        outs = []
        for g in range(NSA_G):
            h = kvh * NSA_G + g
            outs.append(gates[:, 3 * h:3 * h + 1] * o_cmp[g] + gates[:, 3 * h + 1:3 * h + 2] * o_sel[g]
                        + gates[:, 3 * h + 2:3 * h + 3] * o_win[g])
        o_ref[:, kvh * NSA_G * HD:(kvh + 1) * NSA_G * HD] = jnp.concatenate(outs, axis=1).astype(o_ref.dtype)


def _compress_weights(pe, w1, w2):
    eye = jnp.eye(2, dtype=F32)
    w1r = w1.reshape(2, NSA_CMP_LEN, HD, NSA_CMP_HID)
    w1bd = jnp.einsum("hg,srdj->srhdgj", eye, w1r).reshape(2, NSA_CMP_LEN, 2 * HD, 2 * NSA_CMP_HID)
    w2bd = jnp.einsum("hg,sjd->shjgd", eye, w2).reshape(2, 2 * NSA_CMP_HID, 2 * HD)
    pe2 = jnp.tile(pe, (1, 1, 2)).reshape(2, NSA_CMP_LEN, 1, 2 * HD)
    return w1bd.astype(BF16), pe2, w2bd.astype(BF16)


def _compress_prompt_kernel(src_ref, w1_ref, pe_ref, w2_ref, o_ref, stage_sc, *, n_ch):
    rows = NSA_CMP_STRIDE * n_ch
    for h in range(2):
        stage_sc[h, 0:rows, :] = src_ref[:, 128 * h:128 * (h + 1)]
        stage_sc[h, rows:rows + CMP_TAIL, :] = jnp.zeros((CMP_TAIL, 128), F32)
    o_ref[0] = _compress_rows(stage_sc, n_ch, w1_ref.at[0], pe_ref.at[0], w2_ref[0])


def _compress_prompt(kv4, nb, seq, cw):
    w1r, pe2, w2 = cw
    n_ch = seq // NSA_CMP_STRIDE
    return pl.pallas_call(
        functools.partial(_compress_prompt_kernel, n_ch=n_ch),
        grid=(2, nb),
        in_specs=[
            pl.BlockSpec((seq, KV_COLS), lambda s, b: (b, s)),
            pl.BlockSpec((1,) + w1r.shape[1:], lambda s, b: (s, 0, 0, 0)),
            pl.BlockSpec((1,) + pe2.shape[1:], lambda s, b: (s, 0, 0, 0)),
            pl.BlockSpec((1,) + w2.shape[1:], lambda s, b: (s, 0, 0)),
        ],
        out_specs=pl.BlockSpec((1, n_ch, KV_COLS), lambda s, b: (s, b, 0)),
        out_shape=jax.ShapeDtypeStruct((2, nb * n_ch, KV_COLS), F32),
        scratch_shapes=[pltpu.VMEM((2, seq + CMP_TAIL, 128), F32)],
        compiler_params=_params(("arbitrary", "arbitrary"), 40),
        name="nsa_compress_prompt",
    )(kv4, w1r, pe2, w2)


def _overlap_matrix(n_cmp_pad, n_cmp, n_blk, lanes=128):
    c = jnp.arange(n_cmp_pad)[:, None] * NSA_CMP_STRIDE
    j = jnp.arange(lanes)[None, :] * NSA_SEL_LEN
    ok = (c < j + NSA_SEL_LEN) & (c + NSA_CMP_LEN > j)
    ok = ok & (jnp.arange(n_cmp_pad)[:, None] < n_cmp) & (jnp.arange(lanes)[None, :] < n_blk)
    return ok.astype(BF16)


def _expand_matrix(n_keys, chunk, lanes=128):
    key = jnp.arange(n_keys).reshape(n_keys // chunk, 1, chunk)
    j = jnp.arange(lanes).reshape(1, lanes, 1)
    return (key // NSA_SEL_LEN == j).astype(BF16)


def _nsa_prompt_attention(q, kc, vc, kv4, kvw, gates, nb, seq, tq=128, chunk=512):
    nq = seq // tq
    n_cmp = seq // NSA_CMP_STRIDE
    n_blk = seq // NSA_SEL_LEN
    jpad = -(-n_blk // 8) * 8
    ov = _overlap_matrix(n_cmp, n_cmp - 1, n_blk, jpad).T
    ex = _expand_matrix(seq, chunk, jpad)
    return pl.pallas_call(
        functools.partial(_nsa_prompt_kernel, tq=tq, seq=seq, chunk=chunk),
        grid=(nb, nq),
        in_specs=[
            pl.BlockSpec((tq, q.shape[1]), lambda b, i: (b * nq + i, 0)),
            pl.BlockSpec((n_cmp, KV_COLS), lambda b, i: (b, 0)),
            pl.BlockSpec((n_cmp, KV_COLS), lambda b, i: (b, 0)),
            pl.BlockSpec((seq, KV_COLS), lambda b, i: (b, 2)),
            pl.BlockSpec((seq, KV_COLS), lambda b, i: (b, 3)),
            pl.BlockSpec((seq, KV_COLS), lambda b, i: (b, 0)),
            pl.BlockSpec((seq, KV_COLS), lambda b, i: (b, 1)),
            pl.BlockSpec((tq, gates.shape[1]), lambda b, i: (b * nq + i, 0)),
            _const_spec(ov.shape),
            _const_spec(ex.shape),
        ],
        out_specs=pl.BlockSpec((tq, q.shape[1]), lambda b, i: (b * nq + i, 0)),
        out_shape=jax.ShapeDtypeStruct(q.shape, BF16),
        scratch_shapes=[pltpu.VMEM((seq, KV_COLS), BF16)] * 4,
        compiler_params=_params(("arbitrary", "arbitrary"), 48),
        name="nsa_prompt_attention",
    )(q, kc, vc, kv4, kv4, kvw, kvw, gates, ov, ex)


def _nsa_weights(w_in):
    nq = NSA_KVH * NSA_G * HD
    wg = jnp.pad(w_in[:, nq + 6 * KV_COLS:], ((0, 0), (0, 128 - 3 * NSA_KVH * NSA_G)))
    return [w_in[:, :nq].astype(BF16), w_in[:, nq:nq + 4 * KV_COLS].astype(BF16),
            w_in[:, nq + 4 * KV_COLS:nq + 6 * KV_COLS].astype(BF16), wg.astype(BF16)]


def _nsa_prompt(tok, x, g, nb, seq, w_in_parts, cw, w_out, name="nsa"):
    q, kv4, kvw, gates, kv4_t, kvw_t = _mod_proj(tok, x, g, w_in_parts, [F32, F32, F32, F32], name + "_proj",
                                                 rows_minor=(1, 2))
    cmp_rows = _compress_prompt(kv4, nb, seq, cw)
    o = _nsa_prompt_attention(q, cmp_rows[0], cmp_rows[1], kv4, kvw, gates, nb, seq)
    x_new = _out_proj(tok, x, [o], w_out, "plain", name + "_out")
    return x_new, kv4_t, kvw_t


T_PAD = 8
PAGE_SIZE = 128
PAGES_PER_STEP = 16


def _softmax_two(s1, mask1, v1t, s2, mask2, v2):
    m = jnp.maximum(jnp.max(jnp.where(mask1, s1, NEG_BIG), axis=-1, keepdims=True),
                    jnp.max(jnp.where(mask2, s2, NEG_BIG), axis=-1, keepdims=True))
    p1 = jnp.where(mask1, jnp.exp(jnp.where(mask1, s1, NEG_BIG) - m), 0.0)
    p2 = jnp.where(mask2, jnp.exp(jnp.where(mask2, s2, NEG_BIG) - m), 0.0)
    den = jnp.sum(p1, axis=-1, keepdims=True) + jnp.sum(p2, axis=-1, keepdims=True)
    acc = _dot_nt(p1.astype(BF16), v1t) + _bdot(p2.astype(BF16), v2)
    return acc / jnp.where(den > 0, den, 1.0), m, den


def _stack_heads(q_ref, kvh):
    qs = jnp.concatenate(
        [q_ref[0, :, (kvh * NSA_G + g) * HD:(kvh * NSA_G + g + 1) * HD] for g in range(NSA_G)], axis=0)
    return (qs * (HD ** -0.5)).astype(BF16)


def _nsa_sample_cmp_kernel(pt_ref, *refs, n_pages, n_valid):
    del pt_ref
    pages = refs[:PAGES_PER_STEP]
    q_ref, w1_ref, pe_ref, w2_ref, ov_ref, ocmp_ref, sel_ref, stage_sc = refs[PAGES_PER_STEP:]
    step = pl.program_id(1)
    n_rows = n_pages * PAGE_SIZE
    n_ch = n_rows // NSA_CMP_STRIDE

    @pl.when(step == 0)
    def _():
        for h in range(4):
            stage_sc[h, n_rows:n_rows + CMP_TAIL, :] = jnp.zeros((CMP_TAIL, 128), F32)

    for k, page in enumerate(pages):
        r0 = pl.multiple_of((step * PAGES_PER_STEP + k) * PAGE_SIZE, PAGE_SIZE)
        for h in range(4):
            pair = [page[0, h // 2, 2 * (h % 2) + j].T for j in range(2)]
            stage_sc[h, pl.ds(r0, PAGE_SIZE), :] = jnp.concatenate(pair, axis=1)

    @pl.when(step == pl.num_programs(1) - 1)
    def _():
        kc = _compress_rows(stage_sc.at[0:2], n_ch, w1_ref.at[0], pe_ref.at[0], w2_ref[0]).astype(BF16)
        vc = _compress_rows(stage_sc.at[2:4], n_ch, w1_ref.at[1], pe_ref.at[1], w2_ref[1]).astype(BF16)
        tpos = n_rows + lax.broadcasted_iota(jnp.int32, (T_PAD, 1), 0)
        c_idx = lax.broadcasted_iota(jnp.int32, (T_PAD, n_ch), 1)
        cmp_mask = (c_idx * NSA_CMP_STRIDE + NSA_CMP_LEN - 1 <= tpos) & (c_idx < n_ch - 1)
        cmp_mask = jnp.concatenate([cmp_mask] * NSA_G, axis=0)
        lanes = sel_ref.shape[3]
        j_idx = lax.broadcasted_iota(jnp.int32, (T_PAD, lanes), 1)
        tb = tpos // NSA_SEL_LEN
        forced = (j_idx == 0) | (j_idx == tb) | (j_idx == tb - 1)
        n_blk = (n_rows + n_valid + NSA_SEL_LEN - 1) // NSA_SEL_LEN
        for kvh in range(NSA_KVH):
            c0 = kvh * HD
            qs = _stack_heads(q_ref, kvh)
            p_c, _, _ = _masked_softmax_rows(_dot_nt(qs, kc[:, c0:c0 + HD]), cmp_mask)
            o_cmp = _bdot(p_c.astype(BF16), vc[:, c0:c0 + HD])
            p_sum = p_c[0:T_PAD]
            for g in range(1, NSA_G):
                p_sum = p_sum + p_c[g * T_PAD:(g + 1) * T_PAD]
            ph, plo = _split_hi_lo(p_sum)
            imp = _bdot(ph, ov_ref[...]) + _bdot(plo, ov_ref[...])
            score = jnp.where(j_idx <= tb, imp + jnp.where(forced, NSA_FORCE_BONUS, 0.0), -jnp.inf)
            sel_ref[0, kvh] = _top_blocks(score, n_blk, NSA_TOPN)
            ocmp_ref[0, :, kvh * NSA_G * HD:(kvh + 1) * NSA_G * HD] = jnp.concatenate(
                [o_cmp[g * T_PAD:(g + 1) * T_PAD] for g in range(NSA_G)], axis=1)


def _rows_minor(a):
    nd = a.ndim
    return a.transpose((0,) + tuple(range(2, nd)) + (1,))


def _rows_major(a):
    nd = a.ndim
    return a.transpose((0, nd - 1) + tuple(range(1, nd - 1)))


def _page_specs(slot_pair):
    return [pl.BlockSpec((1, 2, NSA_KVH, HD, PAGE_SIZE),
                         lambda b, s, pt, k=k: (pt[b, s * PAGES_PER_STEP + k], slot_pair, 0, 0, 0))
            for k in range(PAGES_PER_STEP)]


def _nsa_sample_cmp(q, cache, page_table, cw, n_valid):
    w1r, pe2, w2 = cw
    nb, n_pages = page_table.shape
    n_rows = n_pages * PAGE_SIZE
    n_ch = n_rows // NSA_CMP_STRIDE
    n_blk = (n_rows + n_valid + NSA_SEL_LEN - 1) // NSA_SEL_LEN
    lanes = -(-n_blk // 128) * 128
    ov = _overlap_matrix(n_ch, n_ch - 1, n_blk, lanes)
    bspec = lambda shape: pl.BlockSpec(shape, lambda b, s, pt: (b,) + (0,) * (len(shape) - 1))
    cspec = lambda shape: pl.BlockSpec(shape, lambda b, s, pt: (0,) * len(shape))
    grid_spec = pltpu.PrefetchScalarGridSpec(
        num_scalar_prefetch=1,
        grid=(nb, n_pages // PAGES_PER_STEP),
        in_specs=_page_specs(0) + [bspec((1, T_PAD, q.shape[2])), cspec(w1r.shape), cspec(pe2.shape),
                                    cspec(w2.shape), cspec(ov.shape)],
        out_specs=[bspec((1, T_PAD, q.shape[2])), bspec((1, NSA_KVH, T_PAD, lanes))],
        scratch_shapes=[pltpu.VMEM((4, n_rows + CMP_TAIL, 128), F32)],
    )
    return pl.pallas_call(
        functools.partial(_nsa_sample_cmp_kernel, n_pages=n_pages, n_valid=n_valid),
        grid_spec=grid_spec,
        out_shape=[jax.ShapeDtypeStruct(q.shape, F32), jax.ShapeDtypeStruct((nb, NSA_KVH, T_PAD, lanes), F32)],
        compiler_params=_params(("arbitrary", "arbitrary"), 56),
        name="nsa_sample_cmp",
    )(page_table, *([cache] * PAGES_PER_STEP), q, w1r, pe2, w2, ov)


def _nsa_sample_attend_kernel(pt_ref, *refs, n_pages, n_valid):
    del pt_ref
    pages = refs[:PAGES_PER_STEP]
    (q_ref, sel_ref, ocmp_ref, new4_ref, neww_ref, win_ref, gt_ref, ex_ref, o_ref, ks_sc, vs_sc) = refs[PAGES_PER_STEP:]
    step = pl.program_id(1)
    n_rows = n_pages * PAGE_SIZE

    for k, page in enumerate(pages):
        r0 = pl.multiple_of((step * PAGES_PER_STEP + k) * PAGE_SIZE, PAGE_SIZE)
        for slot, dst in enumerate((ks_sc, vs_sc)):
            for h in range(NSA_KVH):
                dst[h, :, pl.ds(r0, PAGE_SIZE)] = page[0, slot, h].astype(BF16)

    @pl.when(step == pl.num_programs(1) - 1)
    def _():
        def tile_g(x):
            return jnp.concatenate([x] * NSA_G, axis=0)

        lb = win_ref.shape[4]
        t_idx = lax.broadcasted_iota(jnp.int32, (T_PAD, 1), 0)
        i_new = lax.broadcasted_iota(jnp.int32, (T_PAD, T_PAD), 1)
        new_mask = tile_g((i_new <= t_idx) & (i_new < n_valid))
        w_idx = lax.broadcasted_iota(jnp.int32, (T_PAD, lb), 1)
        win_mask = tile_g(w_idx >= lb + t_idx - NSA_WIN)
        gates = jax.nn.sigmoid(gt_ref[0])
        for kvh in range(NSA_KVH):
            c0 = kvh * HD
            qs = _stack_heads(q_ref, kvh)
            chosen = tile_g(_bdot(sel_ref[0, kvh].astype(BF16), ex_ref[...]) > 0.5)
            k_new = new4_ref[0, :, 2 * KV_COLS + c0:2 * KV_COLS + c0 + HD].astype(BF16)
            v_new = new4_ref[0, :, 3 * KV_COLS + c0:3 * KV_COLS + c0 + HD].astype(BF16)
            o_sel, _, _ = _softmax_two(_bdot(qs, ks_sc[kvh]), chosen, vs_sc[kvh], _dot_nt(qs, k_new), new_mask, v_new)
            kw_old = win_ref[0, 0, kvh].astype(BF16)
            vw_old = win_ref[0, 1, kvh].astype(BF16)
            kw_new = neww_ref[0, :, c0:c0 + HD].astype(BF16)
            vw_new = neww_ref[0, :, KV_COLS + c0:KV_COLS + c0 + HD].astype(BF16)
            o_win, _, _ = _softmax_two(_bdot(qs, kw_old), win_mask, vw_old, _dot_nt(qs, kw_new), new_mask, vw_new)
            outs = []
            for g in range(NSA_G):
                h = kvh * NSA_G + g
                r = slice(g * T_PAD, (g + 1) * T_PAD)
                outs.append(gates[:, 3 * h:3 * h + 1] * ocmp_ref[0, :, h * HD:(h + 1) * HD]
                            + gates[:, 3 * h + 1:3 * h + 2] * o_sel[r] + gates[:, 3 * h + 2:3 * h + 3] * o_win[r])
            o_ref[0, :, kvh * NSA_G * HD:(kvh + 1) * NSA_G * HD] = jnp.concatenate(outs, axis=1).astype(o_ref.dtype)


def _nsa_sample_attend(q, sel, ocmp, new4, neww, win, gates, cache, page_table, n_valid):
    nb, n_pages = page_table.shape
    n_rows = n_pages * PAGE_SIZE
    lanes = sel.shape[3]
    ex = _expand_matrix(n_rows, n_rows, lanes)[0]
    bspec = lambda shape: pl.BlockSpec(shape, lambda b, s, pt: (b,) + (0,) * (len(shape) - 1))
    cspec = lambda shape: pl.BlockSpec(shape, lambda b, s, pt: (0,) * len(shape))
    grid_spec = pltpu.PrefetchScalarGridSpec(
        num_scalar_prefetch=1,
        grid=(nb, n_pages // PAGES_PER_STEP),
        in_specs=_page_specs(1) + [
            bspec((1, T_PAD, q.shape[2])), bspec((1,) + sel.shape[1:]), bspec((1, T_PAD, ocmp.shape[2])),
            bspec((1, T_PAD, new4.shape[2])), bspec((1, T_PAD, neww.shape[2])), bspec((1,) + win.shape[1:]),
            bspec((1, T_PAD, gates.shape[2])), cspec(ex.shape)],
        out_specs=bspec((1, T_PAD, q.shape[2])),
        scratch_shapes=[pltpu.VMEM((NSA_KVH, HD, n_rows), BF16)] * 2,
    )
    return pl.pallas_call(
        functools.partial(_nsa_sample_attend_kernel, n_pages=n_pages, n_valid=n_valid),
        grid_spec=grid_spec,
        out_shape=jax.ShapeDtypeStruct(q.shape, BF16),
        compiler_params=_params(("arbitrary", "arbitrary"), 56),
        name="nsa_sample_attend",
    )(page_table, *([cache] * PAGES_PER_STEP), q, sel, ocmp, new4, neww, win, gates, ex)


def _nsa_sample(tok, x, g, nb, cache, win, page_table, n_valid, w_in_parts, cw, w_out, name="nsa_s"):
    q, kv4, kvw, gates = _mod_proj(tok, x, g, w_in_parts, [F32, F32, F32, F32], name + "_proj")
    r3 = lambda a: a.reshape(nb, T_PAD, a.shape[1])
    cache_t, win_t = _rows_minor(cache), _rows_minor(win)
    ocmp, sel = _nsa_sample_cmp(r3(q), cache_t, page_table, cw, n_valid)
    o = _nsa_sample_attend(r3(q), sel, ocmp, r3(kv4), r3(kvw), win_t, r3(gates), cache_t, page_table, n_valid)
    x_new = _out_proj(tok, x, [o.reshape(nb * T_PAD, o.shape[2])], w_out, "plain", name + "_out")
    return x_new, r3(kv4), r3(kvw)


DIL_PAIRS = ((128, 1), (512, 4), (2048, 16))
DIL_SLOTS = 8
DIL_COLS = DIL_SLOTS * HD


def _dil_prompt_kernel(q_ref, k_ref, v_ref, o_ref, l_ref, *, tq, ls, nback):
    t0 = pl.program_id(2) * tq
    wkeys = min(nback + tq, ls)
    w0 = pl.multiple_of(jnp.maximum(t0 - nback, 0), tq)
    tpos = t0 + lax.broadcasted_iota(jnp.int32, (tq, 1), 0)
    dist = tpos - (w0 + lax.broadcasted_iota(jnp.int32, (tq, wkeys), 1))
    bias = jnp.where((dist >= 0) & (dist <= nback), 0.0, MASK_BIAS)
    outs, lses = [], []
    for h in range(DIL_SLOTS):
        c0 = h * HD
        qh = (q_ref[:, c0:c0 + HD] * (HD ** -0.5)).astype(BF16)
        kh = k_ref[pl.ds(w0, wkeys), c0:c0 + HD].astype(BF16)
        vh = v_ref[pl.ds(w0, wkeys), c0:c0 + HD].astype(BF16)
        o, lse = _biased_attention(qh, kh, vh, bias)
        outs.append(o)
        lses.append(jnp.broadcast_to(lse, (tq, HD)))
    o_ref[...] = jnp.concatenate(outs, axis=1)
    l_ref[...] = jnp.concatenate(lses, axis=1)


def _biased_attention(q, k, v, bias):
    s = _dot_nt(q, k) + bias
    m = jnp.max(s, axis=-1, keepdims=True)
    e = jnp.exp(s - m)
    den = jnp.sum(e, axis=-1, keepdims=True)
    return _bdot(e.astype(BF16), v) / den, m + jnp.log(den)


def _dil_prompt_rows16_kernel(q_ref, kv_ref, o_ref, l_ref, *, dil, nback):
    n_a = q_ref.shape[0]
    sets = 16 // dil
    a_q = lax.broadcasted_iota(jnp.int32, (n_a, n_a), 0)
    a_k = lax.broadcasted_iota(jnp.int32, (n_a, n_a), 1)
    for r in range(dil):
        for h in range(DIL_SLOTS):
            c0 = h * HD
            ks = [kv_ref[:, r + dil * e, c0:c0 + HD].astype(BF16) for e in range(sets)]
            vs = [kv_ref[:, r + dil * e, DIL_COLS + c0:DIL_COLS + c0 + HD].astype(BF16) for e in range(sets)]
            k_all = jnp.concatenate(ks, axis=0) if sets > 1 else ks[0]
            v_all = jnp.concatenate(vs, axis=0) if sets > 1 else vs[0]
            for e in range(sets):
                dist = jnp.concatenate([(sets * a_q + e) - (sets * a_k + e2) for e2 in range(sets)], axis=1) \
                    if sets > 1 else a_q - a_k
                bias = jnp.where((dist >= 0) & (dist <= nback), 0.0, MASK_BIAS)
                qh = (q_ref[:, r + dil * e, c0:c0 + HD] * (HD ** -0.5)).astype(BF16)
                o, lse = _biased_attention(qh, k_all, v_all, bias)
                o_ref[:, r + dil * e, c0:c0 + HD] = o
                l_ref[:, r + dil * e, c0:c0 + HD] = jnp.broadcast_to(lse, (n_a, HD))


def _dil_prompt_group_rows16(q3, kv, gi, nb, seq):
    window, dil = DIL_PAIRS[gi]
    n_a = seq // 16
    qv = q3.reshape(nb * n_a, 16, 3 * DIL_COLS)
    kvv = kv.reshape(nb * n_a, 16, 2 * DIL_COLS)
    out_sd = jax.ShapeDtypeStruct((nb * n_a, 16, DIL_COLS), F32)
    o, lse = pl.pallas_call(
        functools.partial(_dil_prompt_rows16_kernel, dil=dil, nback=window // dil),
        grid=(nb,),
        in_specs=[
            pl.BlockSpec((n_a, 16, DIL_COLS), lambda b: (b, 0, gi)),
            pl.BlockSpec((n_a, 16, 2 * DIL_COLS), lambda b: (b, 0, 0)),
        ],
        out_specs=[pl.BlockSpec((n_a, 16, DIL_COLS), lambda b: (b, 0, 0))] * 2,
        out_shape=[out_sd, out_sd],
        compiler_params=_params(("arbitrary",), 52),
        name=f"dil_prompt_g{gi}",
    )(qv, kvv)
    return o.reshape(nb * seq, DIL_COLS), lse.reshape(nb * seq, DIL_COLS)


def _dil_prompt_group(q3, kv, gi, nb, seq, tq=128):
    window, dil = DIL_PAIRS[gi]
    ls = seq // dil
    nq = ls // tq
    qv = q3.reshape(nb * ls, dil * 3 * DIL_COLS)
    kvv = kv.reshape(nb * ls, dil * 2 * DIL_COLS)
    out_sd = jax.ShapeDtypeStruct((nb * ls, dil * DIL_COLS), F32)
    o, lse = pl.pallas_call(
        functools.partial(_dil_prompt_kernel, tq=tq, ls=ls, nback=window // dil),
        grid=(nb, dil, nq),
        in_specs=[
            pl.BlockSpec((tq, DIL_COLS), lambda b, r, i: (b * nq + i, 3 * r + gi)),
            pl.BlockSpec((ls, DIL_COLS), lambda b, r, i: (b, 2 * r)),
            pl.BlockSpec((ls, DIL_COLS), lambda b, r, i: (b, 2 * r + 1)),
        ],
        out_specs=[pl.BlockSpec((tq, DIL_COLS), lambda b, r, i: (b * nq + i, r))] * 2,
        out_shape=[out_sd, out_sd],
        compiler_params=_params(("arbitrary", "arbitrary", "arbitrary"), 40),
        name=f"dil_prompt_g{gi}",
    )(qv, kvv, kvv)
    return o.reshape(nb * seq, DIL_COLS), lse.reshape(nb * seq, DIL_COLS)


def _dil_weights(w_in):
    n_g = len(DIL_PAIRS)
    w = w_in.reshape(D_MODEL, 3, n_g, DIL_COLS)
    parts = [w[:, 0].reshape(D_MODEL, n_g * DIL_COLS)]
    for gi in range(n_g):
        parts.append(jnp.concatenate([w[:, 1, gi], w[:, 2, gi]], axis=1))
    return [p.astype(BF16) for p in parts]


def _dil_prompt(tok, x, g, nb, seq, w_in_parts, w_out, name="dil"):
    q3, kv0, kv1, kv2, kv0_t, kv1_t, kv2_t = _mod_proj(tok, x, g, w_in_parts, [F32] * 4, name + "_proj",
                                                       rows_minor=(1, 2, 3))
    kvs = (kv0, kv1, kv2)
    outs, lses = [], []
    for gi in range(len(DIL_PAIRS)):
        group = _dil_prompt_group_rows16 if DIL_PAIRS[gi][1] == 16 else _dil_prompt_group
        o, lse = group(q3, kvs[gi], gi, nb, seq)
        outs.append(o)
        lses.append(lse)
    x_new = _out_proj(tok, x, outs + lses, w_out, "dil", name + "_out")
    return x_new, (kv0_t, kv1_t, kv2_t)


def _dil_sample_kernel(q_ref, st_ref, new_ref, o_ref, l_ref, *, window, dil, n_valid):
    lb = st_ref.shape[4]
    t_idx = lax.broadcasted_iota(jnp.int32, (T_PAD, 1), 0)
    d_old = lb + t_idx - lax.broadcasted_iota(jnp.int32, (T_PAD, lb), 1)
    old_mask = ((d_old & (dil - 1)) == 0) & (d_old <= window)
    i_new = lax.broadcasted_iota(jnp.int32, (T_PAD, T_PAD), 1)
    d_new = t_idx - i_new
    new_mask = (d_new >= 0) & ((d_new & (dil - 1)) == 0) & (i_new < n_valid)
    outs, lses = [], []
    for h in range(DIL_SLOTS):
        c0 = h * HD
        qh = (q_ref[0, :, c0:c0 + HD] * (HD ** -0.5)).astype(BF16)
        k_old = st_ref[0, 0, h].astype(BF16)
        v_old = st_ref[0, 1, h].astype(BF16)
        k_new = new_ref[0, :, c0:c0 + HD].astype(BF16)
        v_new = new_ref[0, :, DIL_COLS + c0:DIL_COLS + c0 + HD].astype(BF16)
        o, m, den = _softmax_two(_bdot(qh, k_old), old_mask, v_old, _dot_nt(qh, k_new), new_mask, v_new)
        outs.append(o)
        lses.append(jnp.broadcast_to(m + jnp.log(den), (T_PAD, HD)))
    o_ref[0] = jnp.concatenate(outs, axis=1)
    l_ref[0] = jnp.concatenate(lses, axis=1)


def _dil_sample_group(q3, kv_new, state, gi, n_valid):
    window, dil = DIL_PAIRS[gi]
    nb, lb = state.shape[:2]
    state_t = _rows_minor(state)
    out_sd = jax.ShapeDtypeStruct((nb, T_PAD, DIL_COLS), F32)
    return pl.pallas_call(
        functools.partial(_dil_sample_kernel, window=window, dil=dil, n_valid=n_valid),
        grid=(nb,),
        in_specs=[
            pl.BlockSpec((1, T_PAD, DIL_COLS), lambda b: (b, 0, gi)),
            pl.BlockSpec((1, 2, DIL_SLOTS, HD, lb), lambda b: (b, 0, 0, 0, 0)),
            pl.BlockSpec((1, T_PAD, 2 * DIL_COLS), lambda b: (b, 0, 0)),
        ],
        out_specs=[pl.BlockSpec((1, T_PAD, DIL_COLS), lambda b: (b, 0, 0))] * 2,
        out_shape=[out_sd, out_sd],
        compiler_params=_params(("arbitrary",), 40),
        name=f"dil_sample_g{gi}",
    )(q3, state_t, kv_new)


def _dil_sample(tok, x, g, nb, states, n_valid, w_in_parts, w_out, name="dil_s"):
    q3, kv0, kv1, kv2 = _mod_proj(tok, x, g, w_in_parts, [F32] * 4, name + "_proj")
    r3 = lambda a: a.reshape(nb, T_PAD, a.shape[1])
    kvs = (r3(kv0), r3(kv1), r3(kv2))
    outs, lses = [], []
    for gi in range(len(DIL_PAIRS)):
        o, lse = _dil_sample_group(r3(q3), kvs[gi], states[gi], gi, n_valid)
        outs.append(o.reshape(nb * T_PAD, DIL_COLS))
        lses.append(lse.reshape(nb * T_PAD, DIL_COLS))
    x_new = _out_proj(tok, x, outs + lses, w_out, "dil", name + "_out")
    return x_new, kvs


GLA_H, GLA_DK, GLA_DV, GLA_RANK, GLA_TAU = 4, 128, 256, 16, 16.0
HG_H, HG_DK, HG_DV = 8, 128, 128
SCAN_SUB = 8
SCAN_HEADS_PER_STEP = 4
SCAN_UNROLL = 2


def _log_sigmoid(x):
    return jnp.minimum(x, 0.0) - jnp.log1p(jnp.exp(-jnp.abs(x)))


def _cumsum_rows(g):
    n = g.shape[0]
    tri = (lax.broadcasted_iota(jnp.int32, (n, n), 0) >= lax.broadcasted_iota(jnp.int32, (n, n), 1)).astype(BF16)
    hi = g.astype(BF16)
    r1 = g - hi.astype(F32)
    mid = r1.astype(BF16)
    lo = (r1 - mid.astype(F32)).astype(BF16)
    return _bdot(tri, hi) + _bdot(tri, mid) + _bdot(tri, lo)


def _scan_chunk(q, k, g, v, st):
    n, dk = q.shape
    sub = min(SCAN_SUB, n)
    b = _cumsum_rows(g)
    b_end = b[n - 1:n]
    o = _dot_nt((q * jnp.exp(b)).astype(BF16), st.astype(BF16))

    lane = lax.broadcasted_iota(jnp.int32, (sub, n), 1)
    row = lax.broadcasted_iota(jnp.int32, (sub, n), 0)
    ones = jnp.ones((dk, n), BF16)
    a_rows = []
    for i in range(n // sub):
        lo = i * sub
        qi, ki, bi = q[lo:lo + sub], k[lo:lo + sub], b[lo:lo + sub]
        prods = []
        for s in range(sub):
            e = jnp.exp(jnp.minimum(bi - bi[s:s + 1], 0.0))
            prods.append((qi * ki[s:s + 1] * e).astype(BF16))
        sums = _bdot(jnp.concatenate(prods, axis=0), ones)
        a_i = jnp.zeros((sub, n), F32)
        for s in range(sub):
            a_i = a_i + jnp.where((lane == lo + s) & (row >= s), sums[s * sub:(s + 1) * sub], 0.0)
        if i > 0:
            b_ref = b[lo - 1:lo]
            qd = (qi * jnp.exp(bi - b_ref)).astype(BF16)
            kd = (k[0:lo] * jnp.exp(b_ref - b[0:lo])).astype(BF16)
            if lo < n:
                kd = jnp.concatenate([kd, jnp.zeros((n - lo, dk), BF16)], axis=0)
            a_i = a_i + _dot_nt(qd, kd)
        a_rows.append(a_i)
    a = jnp.concatenate(a_rows, axis=0) if len(a_rows) > 1 else a_rows[0]
    o = o + _bdot(a.astype(BF16), v.astype(BF16))
    kd_end = (k * jnp.exp(b_end - b)).astype(BF16)
    st_new = st * jnp.exp(b_end) + _dot_tn(v.astype(BF16), kd_end)
    return o, st_new


def _scan_kernel(*refs, kind, chunk, n_chunks, n_valid, hpb, dk, dv):
    if kind == "gla":
        q_ref, k_ref, v_ref, a_ref, wa_ref, ba_ref, s0_ref, o_ref, sf_ref, st_sc = refs
    else:
        f_ref, v_ref, q_ref, lb_ref, s0_ref, o_ref, sf_ref, st_sc = refs
    ci = pl.program_id(2)

    @pl.when(ci == 0)
    def _():
        for hh in range(hpb):
            st_sc[hh] = s0_ref[0, hh].T

    def body(j, carry):
        r0 = pl.multiple_of(j * chunk, chunk)
        rows = pl.ds(r0, chunk)
        for hh in range(hpb):
            kc = slice(hh * dk, (hh + 1) * dk)
            vc = slice(hh * dv, (hh + 1) * dv)
            if kind == "gla":
                q = q_ref[rows, kc] * (GLA_DK ** -0.5)
                k = k_ref[rows, kc]
                pre = _bdot(a_ref[rows, :].astype(BF16), wa_ref[:, kc]) + ba_ref[:, kc]
                g = _log_sigmoid(pre) * (1.0 / GLA_TAU)
            else:
                fz = f_ref[rows, kc]
                lb = lb_ref[:, kc]
                q = _silu(q_ref[rows, kc])
                la = jnp.log(lb)
                lc = jnp.log1p(-lb) + _log_sigmoid(fz)
                g = jnp.maximum(la, lc) + jnp.log1p(jnp.exp(-jnp.abs(la - lc)))
                k = (1.0 - lb) * jax.nn.sigmoid(-fz)
            if n_valid < chunk:
                live = lax.broadcasted_iota(jnp.int32, k.shape, 0) < n_valid
                k = jnp.where(live, k, 0.0)
                g = jnp.where(live, g, 0.0)
            o, st = _scan_chunk(q, k, g, v_ref[rows, vc], st_sc[hh])
            st_sc[hh] = st
            o_ref[rows, vc] = o
        return carry

    lax.fori_loop(0, n_chunks, body, 0, unroll=min(SCAN_UNROLL, n_chunks))

    @pl.when(ci == pl.num_programs(2) - 1)
    def _():
        for hh in range(hpb):
            sf_ref[0, hh] = st_sc[hh].T


def _scan(kind, ins, s0_t, nb, rows_per_b, n_valid, name):
    if kind == "gla":
        heads, dk, dv = GLA_H, GLA_DK, GLA_DV
    else:
        heads, dk, dv = HG_H, HG_DK, HG_DV
    chunk = min(64, rows_per_b)
    blk = min(512, rows_per_b)
    nblk = rows_per_b // blk
    hpb = SCAN_HEADS_PER_STEP
    rspec = lambda w: pl.BlockSpec((blk, hpb * w), lambda b, h, i: (b * nblk + i, h))
    if kind == "gla":
        q, k, v, a, wa, ba = ins
        args = [q, k, v, a, wa, ba]
        in_specs = [rspec(dk), rspec(dk), rspec(dv),
                    pl.BlockSpec((blk, a.shape[1]), lambda b, h, i: (b * nblk + i, 0)),
                    pl.BlockSpec((wa.shape[0], hpb * dk), lambda b, h, i: (0, h)),
                    pl.BlockSpec((1, hpb * dk), lambda b, h, i: (0, h))]
    else:
        f, v, q, lb = ins
        args = [f, v, q, lb]
        in_specs = [rspec(dk), rspec(dv), rspec(dk), pl.BlockSpec((1, hpb * dk), lambda b, h, i: (0, h))]
    m = nb * rows_per_b
    st_spec = pl.BlockSpec((1, hpb, dk, dv), lambda b, h, i: (b, h, 0, 0))
    return pl.pallas_call(
        functools.partial(_scan_kernel, kind=kind, chunk=chunk, n_chunks=blk // chunk, n_valid=n_valid,
                          hpb=hpb, dk=dk, dv=dv),
        grid=(nb, heads // hpb, nblk),
        in_specs=in_specs + [st_spec],
        out_specs=[rspec(dv), st_spec],
        out_shape=[jax.ShapeDtypeStruct((m, heads * dv), F32), jax.ShapeDtypeStruct((nb, heads, dk, dv), F32)],
        scratch_shapes=[pltpu.VMEM((hpb, dv, dk), F32)],
        compiler_params=_params(("arbitrary", "arbitrary", "arbitrary"), 40),
        name=name,
    )(*args, s0_t)


def _gla_weights(w_in, w_a2, b_a2):
    nk, nv = GLA_H * GLA_DK, GLA_H * GLA_DV
    wa = jnp.pad(w_in[:, 2 * nk + 2 * nv:], ((0, 0), (0, 128 - GLA_RANK)))
    parts = [w_in[:, :nk], w_in[:, nk:2 * nk], w_in[:, 2 * nk:2 * nk + nv], w_in[:, 2 * nk + nv:2 * nk + 2 * nv], wa]
    wa2 = jnp.pad(w_a2, ((0, 128 - GLA_RANK), (0, 0))).astype(BF16)
    return [p.astype(BF16) for p in parts], wa2, b_a2.reshape(1, nk)


def _gla(tok, x, g, s0_t, nb, rows_per_b, n_valid, wts, norm_g, w_out, name="gla"):
    parts, wa2, ba2 = wts
    q, k, v, r, a = _mod_proj(tok, x, g, parts, [F32] * 5, name + "_proj")
    o, s_t = _scan("gla", [q, k, v, a, wa2, ba2], s0_t, nb, rows_per_b, n_valid, name + "_scan")
    x_new = _out_proj(tok, x, [o, r, norm_g.reshape(1, GLA_DV)], w_out, "heads", name + "_out", heads=GLA_H)
    return x_new, s_t


def _hgrn_weights(w_in, lb_logits, layer):
    nk, nv = HG_H * HG_DK, HG_H * HG_DV
    parts = [w_in[:, :nk], w_in[:, nk:nk + nv], w_in[:, nk + nv:2 * nk + nv], w_in[:, 2 * nk + nv:]]
    sm = jax.nn.softmax(lb_logits.astype(F32), axis=0)
    lb = jnp.sum(sm[1:layer + 1], axis=0).reshape(1, nk)
    return [p.astype(BF16) for p in parts], lb


def _hgrn(tok, x, g, s0_t, nb, rows_per_b, n_valid, wts, norm_g, w_out, name="hgrn"):
    parts, lb = wts
    f, i_in, q, og = _mod_proj(tok, x, g, parts, [F32] * 4, name + "_proj")
    o, s_t = _scan("hgrn", [f, i_in, q, lb], s0_t, nb, rows_per_b, n_valid, name + "_scan")
    x_new = _out_proj(tok, x, [o, og, norm_g.reshape(1, HG_DV)], w_out, "heads", name + "_out", heads=HG_H)
    return x_new, s_t


PROMPT_TM = 256
PROMPT_MOE_TM = 512


def _moe_weights(w_rg, b_rg, w_re, b_re, w_up, w_down):
    n_e = MOE_GROUPS * MOE_PER_GROUP
    wr = jnp.zeros((D_MODEL, ROUTER_LANES), F32).at[:, :MOE_GROUPS].set(w_rg).at[:, MOE_GROUPS:MOE_GROUPS + n_e].set(w_re)
    br = jnp.zeros((1, ROUTER_LANES), F32).at[0, :MOE_GROUPS].set(b_rg).at[0, MOE_GROUPS:MOE_GROUPS + n_e].set(b_re)
    wr_hi, wr_lo = _split_hi_lo(wr)
    up = w_up.reshape(MOE_GROUPS, MOE_PER_GROUP, D_MODEL, 2 * MOE_FF).transpose(0, 2, 1, 3)
    up = up.reshape(MOE_GROUPS, D_MODEL, MOE_PER_GROUP * 2 * MOE_FF).astype(BF16)
    dn = w_down.reshape(MOE_GROUPS, MOE_PER_GROUP * MOE_FF, D_MODEL).astype(BF16)
    return wr_hi, wr_lo, br, up, dn


def kernel(x_prompt, x_sample, cache_nsa_kv, state_nsa_win, state_dil_0, state_dil_1, state_dil_2, state_gla, state_hgrn, page_table, c_prompt, c_sample, nsa_w_in, nsa_cmp_pe, nsa_cmp_w1, nsa_cmp_w2, nsa_w_out, dil_w_in, dil_w_out, gla_w_in, gla_w_a2, gla_b_a2, gla_norm_g, gla_w_out, hg_w_in, hg_lb_logits, hg_norm_g, hg_w_out, norm_g, ada_w, ada_b, moe_w_rg, moe_b_rg, moe_w_re, moe_b_re, moe_w_up, moe_w_down, final_norm_g):
    nb, seq, _ = x_prompt.shape
    ndb, n_t, _ = x_sample.shape
    depth = ada_w.shape[0]
    ms_rows = ndb * T_PAD

    c_rows = -(-(nb + ndb) // 16) * 16
    c_all = jnp.zeros((c_rows, D_MODEL), F32).at[:nb].set(c_prompt).at[nb:nb + ndb].set(c_sample)
    mod = _adaln(c_all, ada_w, ada_b)

    xp = x_prompt.reshape(nb * seq, D_MODEL)
    xs = jnp.pad(x_sample, ((0, 0), (0, T_PAD - n_t), (0, 0))).reshape(ms_rows, D_MODEL)

    outs = {}
    for i in range(depth):
        mp = mod[i, :nb].reshape(nb, ADA_N, D_MODEL).transpose(1, 0, 2).reshape(ADA_N * nb, 1, D_MODEL)
        ms = mod[i, nb:nb + ndb].reshape(ndb, ADA_N, D_MODEL).transpose(1, 0, 2)
        ms = jnp.repeat(ms, T_PAD, axis=1)
        tok_p = _Tokens(mp, False, nb, seq, PROMPT_TM)
        tok_pm = _Tokens(mp, False, nb, seq, PROMPT_MOE_TM)
        tok_s = _Tokens(ms, True, 1, ms_rows, ms_rows)
        g_mix, g_moe = norm_g[i, 0], norm_g[i, 1]
        kind = i % 4
        if kind == 0:
            parts = _nsa_weights(nsa_w_in)
            cw = _compress_weights(nsa_cmp_pe, nsa_cmp_w1, nsa_cmp_w2)
            w_out = nsa_w_out.astype(BF16)
            xp, kv4, kvw = _nsa_prompt(tok_p, xp, g_mix, nb, seq, parts, cw, w_out)
            xs, kv4_s, kvw_s = _nsa_sample(tok_s, xs, g_mix, ndb, cache_nsa_kv, state_nsa_win, page_table, n_t,
                                           parts, cw, w_out)
            keep = min(NSA_WIN, seq)
            outs["nsa_kv_p"] = _rows_major(kv4.reshape(nb, 4, NSA_KVH, HD, seq))
            outs["nsa_kv_s"] = kv4_s[:, :n_t].reshape(ndb, n_t, 4, NSA_KVH, HD).astype(cache_nsa_kv.dtype)
            outs["nsa_win_p"] = _rows_major(kvw.reshape(nb, 2, NSA_KVH, HD, seq)[..., seq - keep:])
            win_all = jnp.concatenate(
                [state_nsa_win, kvw_s[:, :n_t].reshape(ndb, n_t, 2, NSA_KVH, HD).astype(state_nsa_win.dtype)], axis=1)
            outs["nsa_win_s"] = win_all[:, win_all.shape[1] - min(NSA_WIN, win_all.shape[1]):]
        elif kind == 1:
            parts = _dil_weights(dil_w_in)
            w_out = dil_w_out.astype(BF16)
            states = (state_dil_0, state_dil_1, state_dil_2)
            xp, kvs = _dil_prompt(tok_p, xp, g_mix, nb, seq, parts, w_out)
            xs, kvs_s = _dil_sample(tok_s, xs, g_mix, ndb, states, n_t, parts, w_out)
            for gi, (window, _) in enumerate(DIL_PAIRS):
                buf = kvs[gi].reshape(nb, 2, DIL_SLOTS, HD, seq)
                outs[f"dil_p{gi}"] = _rows_major(buf[..., seq - min(window, seq):])
                new = kvs_s[gi][:, :n_t].reshape(ndb, n_t, 2, DIL_SLOTS, HD).astype(states[gi].dtype)
                kv_all = jnp.concatenate([states[gi], new], axis=1)
                outs[f"dil_s{gi}"] = kv_all[:, kv_all.shape[1] - min(window, kv_all.shape[1]):]
        elif kind == 2:
            wts = _gla_weights(gla_w_in, gla_w_a2, gla_b_a2)
            w_out = gla_w_out.astype(BF16)
            zero = jnp.zeros((nb, GLA_H, GLA_DK, GLA_DV), F32)
            xp, s_p = _gla(tok_p, xp, g_mix, zero, nb, seq, seq, wts, gla_norm_g, w_out)
            xs, s_s = _gla(tok_s, xs, g_mix, state_gla.astype(F32), ndb, T_PAD, n_t, wts,
                           gla_norm_g, w_out, name="gla_s")
            outs["gla_p"] = s_p.astype(state_gla.dtype)
            outs["gla_s"] = s_s.astype(state_gla.dtype)
        else:
            wts = _hgrn_weights(hg_w_in, hg_lb_logits, i)
            w_out = hg_w_out.astype(BF16)
            zero = jnp.zeros((nb, HG_H, HG_DK, HG_DV), F32)
            xp, s_p = _hgrn(tok_p, xp, g_mix, zero, nb, seq, seq, wts, hg_norm_g, w_out)
            xs, s_s = _hgrn(tok_s, xs, g_mix, state_hgrn.astype(F32), ndb, T_PAD, n_t, wts,
                            hg_norm_g, w_out, name="hgrn_s")
            outs["hg_p"] = s_p.astype(state_hgrn.dtype)
            outs["hg_s"] = s_s.astype(state_hgrn.dtype)
        mw = _moe_weights(moe_w_rg[i], moe_b_rg[i], moe_w_re[i], moe_b_re[i], moe_w_up[i], moe_w_down[i])
        fg = final_norm_g if i == depth - 1 else None
        xp = _moe(tok_pm, xp, g_moe, *mw, fg, f"moe_p{i}")
        xs = _moe(tok_s, xs, g_moe, *mw, fg, f"moe_s{i}")

    y_prompt = xp.reshape(nb, seq, D_MODEL)
    y_sample = xs.reshape(ndb, T_PAD, D_MODEL)[:, :n_t]
    return (y_prompt, y_sample, outs["nsa_kv_p"], outs["nsa_kv_s"], outs["nsa_win_p"], outs["nsa_win_s"],
            outs["dil_p0"], outs["dil_s0"], outs["dil_p1"], outs["dil_s1"], outs["dil_p2"], outs["dil_s2"],
            outs["gla_p"], outs["gla_s"], outs["hg_p"], outs["hg_s"])
```

```python
import functools

import jax
import jax.numpy as jnp
from jax import lax
from jax.experimental import pallas as pl
from jax.experimental.pallas import tpu as pltpu

F32 = jnp.float32
BF16 = jnp.bfloat16
EPS = 1e-6
D_MODEL = 1024
HD = 64
ADA_N = 6
MIB = 1024 * 1024
NEG_BIG = -1e30
MASK_BIAS = -2e30


def _params(sem, vmem_mib):
    return pltpu.CompilerParams(dimension_semantics=sem, vmem_limit_bytes=vmem_mib * MIB)


def _silu(x):
    return x * jax.nn.sigmoid(x)


def _bdot(a, b):
    return jnp.dot(a, b, preferred_element_type=F32)


def _dot_nt(a, b):
    return lax.dot_general(a, b, (((1,), (1,)), ((), ())), preferred_element_type=F32)


def _dot_tn(a, b):
    return lax.dot_general(a, b, (((0,), (0,)), ((), ())), preferred_element_type=F32)


def _split_hi_lo(x):
    hi = x.astype(BF16)
    lo = (x - hi.astype(F32)).astype(BF16)
    return hi, lo


def _rms(x):
    return x * lax.rsqrt(jnp.mean(x * x, axis=-1, keepdims=True) + EPS)


class _Tokens:
    def __init__(self, mod, per_row, nb, rows_per_b, tm):
        self.mod = mod
        self.per_row = per_row
        self.nb = nb
        self.rows_per_b = rows_per_b
        self.tm = tm

    def mod_spec(self, k):
        tm = self.tm
        if self.per_row:
            return pl.BlockSpec((1, tm, D_MODEL), lambda i, *_: (k, i, 0))
        nb, rpb = self.nb, self.rows_per_b
        return pl.BlockSpec((1, 1, D_MODEL), lambda i, *_: (k * nb + (i * tm) // rpb, 0, 0))


def _row_spec(tm, n):
    return pl.BlockSpec((tm, n), lambda i, *_: (i, 0))


def _const_spec(shape):
    nd = len(shape)
    return pl.BlockSpec(shape, lambda *_: (0,) * nd)


def _adaln_kernel(c_ref, w_ref, b_ref, o_ref):
    x = _silu(c_ref[...]).astype(BF16)
    o_ref[0] = _bdot(x, w_ref[0].astype(BF16)) + b_ref[0]


def _adaln(c_all, ada_w, ada_b):
    depth, _, n = ada_w.shape
    rows = c_all.shape[0]
    tn = 1536
    return pl.pallas_call(
        _adaln_kernel,
        grid=(depth, n // tn),
        in_specs=[
            pl.BlockSpec((rows, D_MODEL), lambda l, j: (0, 0)),
            pl.BlockSpec((1, D_MODEL, tn), lambda l, j: (l, 0, j)),
            pl.BlockSpec((1, 1, tn), lambda l, j: (l, 0, j)),
        ],
        out_specs=pl.BlockSpec((1, rows, tn), lambda l, j: (l, 0, j)),
        out_shape=jax.ShapeDtypeStruct((depth, rows, n), F32),
        compiler_params=_params(("arbitrary", "arbitrary"), 40),
        name="adaln",
    )(c_all, ada_w, ada_b.reshape(depth, 1, n))


def _modulate(x, g, shift, scale):
    return _rms(x) * g * (1.0 + scale) + shift


def _mod_proj_kernel(x_ref, g_ref, sh_ref, sc_ref, *refs, n_out, rows_minor):
    w_refs, o_refs, t_refs = refs[:n_out], refs[n_out:2 * n_out], list(refs[2 * n_out:])
    h = _modulate(x_ref[...], g_ref[...], sh_ref[0], sc_ref[0]).astype(BF16)
    for k, (w_ref, o_ref) in enumerate(zip(w_refs, o_refs)):
        z = _bdot(h, w_ref[...])
        o_ref[...] = z.astype(o_ref.dtype)
        if k in rows_minor:
            t_refs.pop(0)[0] = z.T


def _mod_proj(tok, x, g, ws, dtypes, name, rows_minor=()):
    m = x.shape[0]
    tm = tok.tm
    out_specs = [_row_spec(tm, w.shape[1]) for w in ws]
    out_shape = [jax.ShapeDtypeStruct((m, w.shape[1]), dt) for w, dt in zip(ws, dtypes)]
    tiles = tok.rows_per_b // tm
    for k in rows_minor:
        cols = ws[k].shape[1]
        out_specs.append(pl.BlockSpec((1, cols, tm), lambda i: (i // tiles, 0, i % tiles)))
        out_shape.append(jax.ShapeDtypeStruct((tok.nb, cols, tok.rows_per_b), F32))
    return pl.pallas_call(
        functools.partial(_mod_proj_kernel, n_out=len(ws), rows_minor=tuple(rows_minor)),
        grid=(m // tm,),
        in_specs=[_row_spec(tm, D_MODEL), _const_spec((1, D_MODEL)), tok.mod_spec(0), tok.mod_spec(1)]
        + [_const_spec(w.shape) for w in ws],
        out_specs=out_specs,
        out_shape=out_shape,
        compiler_params=_params(("arbitrary",), 56),
        name=name,
    )(x, g.reshape(1, D_MODEL), tok.mod, tok.mod, *ws)


def _out_proj_kernel(x_ref, gt_ref, *refs, mode, heads):
    if mode == "plain":
        a_ref, w_ref, o_ref = refs
        a = a_ref[...].astype(BF16)
    elif mode == "dil":
        o0, o1, o2, l0, l1, l2, w_ref, o_ref = refs
        la, lb, lc = l0[...], l1[...], l2[...]
        mx = jnp.maximum(jnp.maximum(la, lb), lc)
        wa, wb, wc = jnp.exp(la - mx), jnp.exp(lb - mx), jnp.exp(lc - mx)
        a = ((wa * o0[...] + wb * o1[...] + wc * o2[...]) / (wa + wb + wc)).astype(BF16)
    else:
        s_ref, r_ref, ng_ref, w_ref, o_ref = refs
        dv = s_ref.shape[1] // heads
        parts = []
        for h in range(heads):
            parts.append(_rms(s_ref[:, h * dv:(h + 1) * dv]) * ng_ref[...])
        a = (jnp.concatenate(parts, axis=1) * _silu(r_ref[...])).astype(BF16)
    o_ref[...] = x_ref[...] + gt_ref[0] * _bdot(a, w_ref[...])


def _out_proj(tok, x, ins, w, mode, name, heads=1, col_blocks=None):
    m = x.shape[0]
    tm = tok.tm
    in_specs = [_row_spec(tm, D_MODEL), tok.mod_spec(2)]
    for k, a in enumerate(ins):
        if a.shape[0] != m:
            in_specs.append(_const_spec(a.shape))
        elif col_blocks and col_blocks[k] is not None:
            width, blk = col_blocks[k]
            in_specs.append(pl.BlockSpec((tm, width), lambda i, blk=blk: (i, blk)))
        else:
            in_specs.append(_row_spec(tm, a.shape[1]))
    in_specs.append(_const_spec(w.shape))
    return pl.pallas_call(
        functools.partial(_out_proj_kernel, mode=mode, heads=heads),
        grid=(m // tm,),
        in_specs=in_specs,
        out_specs=_row_spec(tm, D_MODEL),
        out_shape=jax.ShapeDtypeStruct((m, D_MODEL), F32),
        compiler_params=_params(("arbitrary",), 48),
        name=name,
    )(x, tok.mod, *ins, w)


MOE_GROUPS = 4
MOE_PER_GROUP = 4
MOE_FF = 256
ROUTER_LANES = 128


def _router_gates(logits):
    lane = lax.broadcasted_iota(jnp.int32, logits.shape, 1)
    lane_f = lane.astype(F32)
    neg = -jnp.inf
    glog = jnp.where(lane < MOE_GROUPS, logits, neg)
    gmax = jnp.max(glog, axis=-1, keepdims=True)
    grp = jnp.min(jnp.where(glog == gmax, lane_f, 1e9), axis=-1, keepdims=True)
    p_grp = 1.0 / jnp.sum(jnp.exp(glog - gmax), axis=-1, keepdims=True)
    e_grp = ((lane - MOE_GROUPS) >> 2).astype(F32)
    n_e = MOE_GROUPS * MOE_PER_GROUP
    in_grp = (lane >= MOE_GROUPS) & (lane < MOE_GROUPS + n_e) & (e_grp == grp)
    e_in = jnp.where(in_grp, logits, neg)
    v1 = jnp.max(e_in, axis=-1, keepdims=True)
    i1 = jnp.min(jnp.where(e_in == v1, lane_f, 1e9), axis=-1, keepdims=True)
    e2 = jnp.where(lane_f == i1, neg, e_in)
    v2 = jnp.max(e2, axis=-1, keepdims=True)
    i2 = jnp.min(jnp.where(e2 == v2, lane_f, 1e9), axis=-1, keepdims=True)
    t = jnp.exp(v2 - v1)
    w1 = p_grp / (1.0 + t)
    w2 = p_grp * t / (1.0 + t)
    return jnp.where(lane_f == i1, w1, 0.0) + jnp.where(lane_f == i2, w2, 0.0)


def _moe_kernel(x_ref, g_ref, sh_ref, sc_ref, gt_ref, wrh_ref, wrl_ref, br_ref, up_ref, dn_ref, *rest, final):
    if final:
        fg_ref, o_ref, h_sc, gate_sc, acc_sc = rest
    else:
        o_ref, h_sc, gate_sc, acc_sc = rest
    grp = pl.program_id(1)

    @pl.when(grp == 0)
    def _():
        h = _modulate(x_ref[...], g_ref[...], sh_ref[0], sc_ref[0])
        hh, hl = _split_hi_lo(h)
        h_sc[...] = hh
        logits = _bdot(hh, wrh_ref[...]) + _bdot(hl, wrh_ref[...]) + _bdot(hh, wrl_ref[...]) + br_ref[...]
        gate_sc[...] = _router_gates(logits)
        acc_sc[...] = jnp.zeros_like(acc_sc)

    h = h_sc[...]
    gates = gate_sc[...]
    lane = lax.broadcasted_iota(jnp.int32, gates.shape, 1)
    acts = []
    for e in range(MOE_PER_GROUP):
        col = jnp.sum(jnp.where(lane == MOE_GROUPS + MOE_PER_GROUP * grp + e, gates, 0.0), axis=-1, keepdims=True)
        hid = _bdot(h, up_ref[e])
        acts.append((_silu(hid[:, :MOE_FF]) * hid[:, MOE_FF:] * col).astype(BF16))
    acc_sc[...] += _bdot(jnp.concatenate(acts, axis=1), dn_ref[0])

    @pl.when(grp == MOE_GROUPS - 1)
    def _():
        y = x_ref[...] + gt_ref[0] * acc_sc[...]
        if final:
            y = _rms(y) * fg_ref[...]
        o_ref[...] = y


def _moe(tok, x, g, wr_hi, wr_lo, br, up, dn, final_g, name):
    m = x.shape[0]
    tm = tok.tm
    final = final_g is not None
    ins = [x, g.reshape(1, D_MODEL), tok.mod, tok.mod, tok.mod, wr_hi, wr_lo, br, up, dn]
    in_specs = [
        _row_spec(tm, D_MODEL), _const_spec((1, D_MODEL)), tok.mod_spec(3), tok.mod_spec(4), tok.mod_spec(5),
        _const_spec(wr_hi.shape), _const_spec(wr_lo.shape), _const_spec(br.shape),
        pl.BlockSpec((MOE_PER_GROUP,) + up.shape[1:], lambda i, e: (e, 0, 0)),
        pl.BlockSpec((1,) + dn.shape[1:], lambda i, e: (e, 0, 0)),
    ]
    if final:
        ins.append(final_g.reshape(1, D_MODEL))
        in_specs.append(_const_spec((1, D_MODEL)))
    return pl.pallas_call(
        functools.partial(_moe_kernel, final=final),
        grid=(m // tm, MOE_GROUPS),
        in_specs=in_specs,
        out_specs=_row_spec(tm, D_MODEL),
        out_shape=jax.ShapeDtypeStruct((m, D_MODEL), F32),
        scratch_shapes=[
            pltpu.VMEM((tm, D_MODEL), BF16),
            pltpu.VMEM((tm, ROUTER_LANES), F32),
            pltpu.VMEM((tm, D_MODEL), F32),
        ],
        compiler_params=_params(("arbitrary", "arbitrary"), 48),
        name=name,
    )(*ins)


NSA_KVH = 4
NSA_G = 4
NSA_CMP_LEN = 32
NSA_CMP_STRIDE = 16
NSA_CMP_HID = 128
NSA_SEL_LEN = 64
NSA_TOPN = 16
NSA_WIN = 512
NSA_FORCE_BONUS = 1000.0
KV_COLS = NSA_KVH * HD
CMP_TAIL = 8 * NSA_CMP_STRIDE


def _compress_rows(src_ref, n_ch, w1_ref, pe_ref, w2_ref):
    hid = [jnp.zeros((n_ch, 2 * NSA_CMP_HID), F32) for _ in range(2)]
    for r in range(NSA_CMP_STRIDE):
        for h in range(2):
            rows = src_ref[h, pl.ds(r, n_ch + 8, stride=NSA_CMP_STRIDE), :]
            lo = (rows[0:n_ch] + pe_ref[r]).astype(BF16)
            hi = (rows[1:n_ch + 1] + pe_ref[r + NSA_CMP_STRIDE]).astype(BF16)
            hid[h] = hid[h] + _bdot(lo, w1_ref[r]) + _bdot(hi, w1_ref[r + NSA_CMP_STRIDE])
    outs = [_bdot(_silu(h).astype(BF16), w2_ref[...]) for h in hid]
    return jnp.concatenate(outs, axis=1)


def _masked_softmax_rows(s, mask):
    sm = jnp.where(mask, s, NEG_BIG)
    m = jnp.max(sm, axis=-1, keepdims=True)
    e = jnp.where(mask, jnp.exp(sm - m), 0.0)
    den = jnp.sum(e, axis=-1, keepdims=True)
    return e / jnp.where(den > 0, den, 1.0), m, den


def _top_blocks(score, n_valid, topn):
    lane = lax.broadcasted_iota(jnp.int32, score.shape, 1)
    rank = jnp.zeros(score.shape, F32)
    for i in range(n_valid):
        si = score[:, i:i + 1]
        ahead = (si > score) | ((si == score) & (lane > i))
        rank = rank + jnp.where(ahead, 1.0, 0.0)
    return jnp.where((rank < topn) & (lane < n_valid), 1.0, 0.0)


def _nsa_prompt_kernel(q_ref, kc_ref, vc_ref, ks_ref, vs_ref, kw_ref, vw_ref, gt_ref, ov_ref, ex_ref, o_ref,
                       ks_sc, vs_sc, kw_sc, vw_sc, *, tq, seq, chunk):
    qi = pl.program_id(1)
    t0 = qi * tq

    @pl.when(qi == 0)
    def _():
        ks_sc[...] = ks_ref[...].astype(BF16)
        vs_sc[...] = vs_ref[...].astype(BF16)
        kw_sc[...] = kw_ref[...].astype(BF16)
        vw_sc[...] = vw_ref[...].astype(BF16)

    n_cmp = kc_ref.shape[0]
    n_blk = seq // NSA_SEL_LEN
    jpad = ov_ref.shape[0]
    wkeys = NSA_WIN + tq
    rows = NSA_G * tq
    gates = jax.nn.sigmoid(gt_ref[...])
    tpos = t0 + lax.broadcasted_iota(jnp.int32, (tq, 1), 0)

    def heads3(x):
        return x.reshape(NSA_G, tq, x.shape[-1])

    c_end = lax.broadcasted_iota(jnp.int32, (tq, n_cmp), 1) * NSA_CMP_STRIDE + NSA_CMP_LEN - 1
    cmp_ok = (c_end <= tpos) & (c_end < (n_cmp - 1) * NSA_CMP_STRIDE + NSA_CMP_LEN - 1)
    cmp_bias = jnp.where(cmp_ok, 0.0, MASK_BIAS)
    has_cmp = jnp.where(tpos >= NSA_CMP_LEN - 1, 1.0, 0.0)
    w0 = pl.multiple_of(jnp.maximum(t0 - NSA_WIN, 0), tq)
    wdist = tpos - (w0 + lax.broadcasted_iota(jnp.int32, (tq, wkeys), 1))
    win_bias = jnp.where((wdist >= 0) & (wdist <= NSA_WIN), 0.0, MASK_BIAS)
    n_sel_chunks = (t0 + tq + chunk - 1) // chunk
    k_last = pl.multiple_of((n_sel_chunks - 1) * chunk, chunk)
    causal_bias = jnp.where(k_last + lax.broadcasted_iota(jnp.int32, (tq, chunk), 1) <= tpos, 0.0, MASK_BIAS)
    t_lane = t0 + lax.broadcasted_iota(jnp.int32, (jpad, tq), 1)
    j_idx = lax.broadcasted_iota(jnp.int32, (jpad, tq), 0)
    tb = t_lane // NSA_SEL_LEN
    forced = (j_idx == 0) | (j_idx == tb) | (j_idx == tb - 1)

    for kvh in range(NSA_KVH):
        c0 = kvh * HD
        qs = jnp.concatenate(
            [q_ref[:, (kvh * NSA_G + g) * HD:(kvh * NSA_G + g + 1) * HD] for g in range(NSA_G)], axis=0)
        qs = (qs * (HD ** -0.5)).astype(BF16)

        s = heads3(_dot_nt(qs, kc_ref[:, c0:c0 + HD].astype(BF16))) + cmp_bias[None]
        e = jnp.exp(s - jnp.max(s, axis=-1, keepdims=True))
        p_c = e * (has_cmp[None] / jnp.sum(e, axis=-1, keepdims=True))
        o_cmp = _bdot(p_c.reshape(rows, n_cmp).astype(BF16), vc_ref[:, c0:c0 + HD].astype(BF16))
        ph, plo = _split_hi_lo(jnp.sum(p_c, axis=0))
        imp = _dot_nt(ov_ref[...], ph) + _dot_nt(ov_ref[...], plo)
        score = jnp.where(j_idx <= tb, imp + jnp.where(forced, NSA_FORCE_BONUS, 0.0), -jnp.inf)
        rank = jnp.zeros((jpad, tq), F32)
        for i in range(n_blk):
            si = score[i:i + 1, :]
            rank = rank + jnp.where((si > score) | ((si == score) & (j_idx > i)), 1.0, 0.0)
        sel_bias = jnp.where((rank < NSA_TOPN) & (j_idx < n_blk), 0.0, MASK_BIAS).astype(BF16)

        def sel_chunk(ci, carry, extra_bias):
            m, l, acc = carry
            k0 = pl.multiple_of(ci * chunk, chunk)
            kch = ks_sc[pl.ds(k0, chunk), c0:c0 + HD]
            vch = vs_sc[pl.ds(k0, chunk), c0:c0 + HD]
            bias = _dot_tn(sel_bias, ex_ref[ci])
            if extra_bias is not None:
                bias = bias + extra_bias
            sc = heads3(_dot_nt(qs, kch)) + bias[None]
            m_new = jnp.maximum(m, jnp.max(sc, axis=-1, keepdims=True))
            p = jnp.exp(sc - m_new)
            alpha = jnp.exp(m - m_new)
            l = alpha * l + jnp.sum(p, axis=-1, keepdims=True)
            pv = _bdot(p.reshape(rows, chunk).astype(BF16), vch)
            acc = alpha * acc + heads3(pv)
            return m_new, l, acc

        init = (jnp.full((NSA_G, tq, 1), NEG_BIG, F32), jnp.zeros((NSA_G, tq, 1), F32),
                jnp.zeros((NSA_G, tq, HD), F32))
        carry = lax.fori_loop(0, n_sel_chunks - 1, lambda ci, c: sel_chunk(ci, c, None), init)
        _, l_s, acc_s = sel_chunk(n_sel_chunks - 1, carry, causal_bias)
        o_sel = acc_s / l_s

        kwin = kw_sc[pl.ds(w0, wkeys), c0:c0 + HD]
        vwin = vw_sc[pl.ds(w0, wkeys), c0:c0 + HD]
        sw = heads3(_dot_nt(qs, kwin)) + win_bias[None]
        ew = jnp.exp(sw - jnp.max(sw, axis=-1, keepdims=True))
        o_win = heads3(_bdot(ew.reshape(rows, wkeys).astype(BF16), vwin)) / jnp.sum(ew, axis=-1, keepdims=True)
        o_cmp = heads3(o_cmp)

        outs = []
        for g in range(NSA_G):
            h = kvh * NSA_G + g
            outs.append(gates[:, 3 * h:3 * h + 1] * o_cmp[g] + gates[:, 3 * h + 1:3 * h + 2] * o_sel[g]
                        + gates[:, 3 * h + 2:3 * h + 3] * o_win[g])
        o_ref[:, kvh * NSA_G * HD:(kvh + 1) * NSA_G * HD] = jnp.concatenate(outs, axis=1).astype(o_ref.dtype)


def _compress_weights(pe, w1, w2):
    eye = jnp.eye(2, dtype=F32)
    w1r = w1.reshape(2, NSA_CMP_LEN, HD, NSA_CMP_HID)
    w1bd = jnp.einsum("hg,srdj->srhdgj", eye, w1r).reshape(2, NSA_CMP_LEN, 2 * HD, 2 * NSA_CMP_HID)
    w2bd = jnp.einsum("hg,sjd->shjgd", eye, w2).reshape(2, 2 * NSA_CMP_HID, 2 * HD)
    pe2 = jnp.tile(pe, (1, 1, 2)).reshape(2, NSA_CMP_LEN, 1, 2 * HD)
    return w1bd.astype(BF16), pe2, w2bd.astype(BF16)


def _compress_prompt_kernel(src_ref, w1_ref, pe_ref, w2_ref, o_ref, stage_sc, *, n_ch):
    rows = NSA_CMP_STRIDE * n_ch
    for h in range(2):
        stage_sc[h, 0:rows, :] = src_ref[:, 128 * h:128 * (h + 1)]
        stage_sc[h, rows:rows + CMP_TAIL, :] = jnp.zeros((CMP_TAIL, 128), F32)
    o_ref[0] = _compress_rows(stage_sc, n_ch, w1_ref.at[0], pe_ref.at[0], w2_ref[0])


def _compress_prompt(kv4, nb, seq, cw):
    w1r, pe2, w2 = cw
    n_ch = seq // NSA_CMP_STRIDE
    return pl.pallas_call(
        functools.partial(_compress_prompt_kernel, n_ch=n_ch),
        grid=(2, nb),
        in_specs=[
            pl.BlockSpec((seq, KV_COLS), lambda s, b: (b, s)),
            pl.BlockSpec((1,) + w1r.shape[1:], lambda s, b: (s, 0, 0, 0)),
            pl.BlockSpec((1,) + pe2.shape[1:], lambda s, b: (s, 0, 0, 0)),
            pl.BlockSpec((1,) + w2.shape[1:], lambda s, b: (s, 0, 0)),
        ],
        out_specs=pl.BlockSpec((1, n_ch, KV_COLS), lambda s, b: (s, b, 0)),
        out_shape=jax.ShapeDtypeStruct((2, nb * n_ch, KV_COLS), F32),
        scratch_shapes=[pltpu.VMEM((2, seq + CMP_TAIL, 128), F32)],
        compiler_params=_params(("arbitrary", "arbitrary"), 40),
        name="nsa_compress_prompt",
    )(kv4, w1r, pe2, w2)


def _overlap_matrix(n_cmp_pad, n_cmp, n_blk, lanes=128):
    c = jnp.arange(n_cmp_pad)[:, None] * NSA_CMP_STRIDE
    j = jnp.arange(lanes)[None, :] * NSA_SEL_LEN
    ok = (c < j + NSA_SEL_LEN) & (c + NSA_CMP_LEN > j)
    ok = ok & (jnp.arange(n_cmp_pad)[:, None] < n_cmp) & (jnp.arange(lanes)[None, :] < n_blk)
    return ok.astype(BF16)


def _expand_matrix(n_keys, chunk, lanes=128):
    key = jnp.arange(n_keys).reshape(n_keys // chunk, 1, chunk)
    j = jnp.arange(lanes).reshape(1, lanes, 1)
    return (key // NSA_SEL_LEN == j).astype(BF16)


def _nsa_prompt_attention(q, kc, vc, kv4, kvw, gates, nb, seq, tq=128, chunk=512):
    nq = seq // tq
    n_cmp = seq // NSA_CMP_STRIDE
    n_blk = seq // NSA_SEL_LEN
    jpad = -(-n_blk // 8) * 8
    ov = _overlap_matrix(n_cmp, n_cmp - 1, n_blk, jpad).T
    ex = _expand_matrix(seq, chunk, jpad)
    return pl.pallas_call(
        functools.partial(_nsa_prompt_kernel, tq=tq, seq=seq, chunk=chunk),
        grid=(nb, nq),
        in_specs=[
            pl.BlockSpec((tq, q.shape[1]), lambda b, i: (b * nq + i, 0)),
            pl.BlockSpec((n_cmp, KV_COLS), lambda b, i: (b, 0)),
            pl.BlockSpec((n_cmp, KV_COLS), lambda b, i: (b, 0)),
            pl.BlockSpec((seq, KV_COLS), lambda b, i: (b, 2)),
            pl.BlockSpec((seq, KV_COLS), lambda b, i: (b, 3)),
            pl.BlockSpec((seq, KV_COLS), lambda b, i: (b, 0)),
            pl.BlockSpec((seq, KV_COLS), lambda b, i: (b, 1)),
            pl.BlockSpec((tq, gates.shape[1]), lambda b, i: (b * nq + i, 0)),
            _const_spec(ov.shape),
            _const_spec(ex.shape),
        ],
        out_specs=pl.BlockSpec((tq, q.shape[1]), lambda b, i: (b * nq + i, 0)),
        out_shape=jax.ShapeDtypeStruct(q.shape, BF16),
        scratch_shapes=[pltpu.VMEM((seq, KV_COLS), BF16)] * 4,
        compiler_params=_params(("arbitrary", "arbitrary"), 48),
        name="nsa_prompt_attention",
    )(q, kc, vc, kv4, kv4, kvw, kvw, gates, ov, ex)


def _nsa_weights(w_in):
    nq = NSA_KVH * NSA_G * HD
    wg = jnp.pad(w_in[:, nq + 6 * KV_COLS:], ((0, 0), (0, 128 - 3 * NSA_KVH * NSA_G)))
    return [w_in[:, :nq].astype(BF16), w_in[:, nq:nq + 4 * KV_COLS].astype(BF16),
            w_in[:, nq + 4 * KV_COLS:nq + 6 * KV_COLS].astype(BF16), wg.astype(BF16)]


def _nsa_prompt(tok, x, g, nb, seq, w_in_parts, cw, w_out, name="nsa"):
    q, kv4, kvw, gates, kv4_t, kvw_t = _mod_proj(tok, x, g, w_in_parts, [F32, F32, F32, F32], name + "_proj",
                                                 rows_minor=(1, 2))
    cmp_rows = _compress_prompt(kv4, nb, seq, cw)
    o = _nsa_prompt_attention(q, cmp_rows[0], cmp_rows[1], kv4, kvw, gates, nb, seq)
    x_new = _out_proj(tok, x, [o], w_out, "plain", name + "_out")
    return x_new, kv4_t, kvw_t


T_PAD = 8
PAGE_SIZE = 128
PAGES_PER_STEP = 16


def _softmax_two(s1, mask1, v1t, s2, mask2, v2):
    m = jnp.maximum(jnp.max(jnp.where(mask1, s1, NEG_BIG), axis=-1, keepdims=True),
                    jnp.max(jnp.where(mask2, s2, NEG_BIG), axis=-1, keepdims=True))
    p1 = jnp.where(mask1, jnp.exp(jnp.where(mask1, s1, NEG_BIG) - m), 0.0)
    p2 = jnp.where(mask2, jnp.exp(jnp.where(mask2, s2, NEG_BIG) - m), 0.0)
    den = jnp.sum(p1, axis=-1, keepdims=True) + jnp.sum(p2, axis=-1, keepdims=True)
    acc = _dot_nt(p1.astype(BF16), v1t) + _bdot(p2.astype(BF16), v2)
    return acc / jnp.where(den > 0, den, 1.0), m, den


def _stack_heads(q_ref, kvh):
    qs = jnp.concatenate(
        [q_ref[0, :, (kvh * NSA_G + g) * HD:(kvh * NSA_G + g + 1) * HD] for g in range(NSA_G)], axis=0)
    return (qs * (HD ** -0.5)).astype(BF16)


def _nsa_sample_cmp_kernel(pt_ref, *refs, n_pages, n_valid):
    del pt_ref
    pages = refs[:PAGES_PER_STEP]
    q_ref, w1_ref, pe_ref, w2_ref, ov_ref, ocmp_ref, sel_ref, stage_sc = refs[PAGES_PER_STEP:]
    step = pl.program_id(1)
    n_rows = n_pages * PAGE_SIZE
    n_ch = n_rows // NSA_CMP_STRIDE

    @pl.when(step == 0)
    def _():
        for h in range(4):
            stage_sc[h, n_rows:n_rows + CMP_TAIL, :] = jnp.zeros((CMP_TAIL, 128), F32)

    for k, page in enumerate(pages):
        r0 = pl.multiple_of((step * PAGES_PER_STEP + k) * PAGE_SIZE, PAGE_SIZE)
        for h in range(4):
            pair = [page[0, h // 2, 2 * (h % 2) + j].T for j in range(2)]
            stage_sc[h, pl.ds(r0, PAGE_SIZE), :] = jnp.concatenate(pair, axis=1)

    @pl.when(step == pl.num_programs(1) - 1)
    def _():
        kc = _compress_rows(stage_sc.at[0:2], n_ch, w1_ref.at[0], pe_ref.at[0], w2_ref[0]).astype(BF16)
        vc = _compress_rows(stage_sc.at[2:4], n_ch, w1_ref.at[1], pe_ref.at[1], w2_ref[1]).astype(BF16)
        tpos = n_rows + lax.broadcasted_iota(jnp.int32, (T_PAD, 1), 0)
        c_idx = lax.broadcasted_iota(jnp.int32, (T_PAD, n_ch), 1)
        cmp_mask = (c_idx * NSA_CMP_STRIDE + NSA_CMP_LEN - 1 <= tpos) & (c_idx < n_ch - 1)
        cmp_mask = jnp.concatenate([cmp_mask] * NSA_G, axis=0)
        lanes = sel_ref.shape[3]
        j_idx = lax.broadcasted_iota(jnp.int32, (T_PAD, lanes), 1)
        tb = tpos // NSA_SEL_LEN
        forced = (j_idx == 0) | (j_idx == tb) | (j_idx == tb - 1)
        n_blk = (n_rows + n_valid + NSA_SEL_LEN - 1) // NSA_SEL_LEN
        for kvh in range(NSA_KVH):
            c0 = kvh * HD
            qs = _stack_heads(q_ref, kvh)
            p_c, _, _ = _masked_softmax_rows(_dot_nt(qs, kc[:, c0:c0 + HD]), cmp_mask)
            o_cmp = _bdot(p_c.astype(BF16), vc[:, c0:c0 + HD])
            p_sum = p_c[0:T_PAD]
            for g in range(1, NSA_G):
                p_sum = p_sum + p_c[g * T_PAD:(g + 1) * T_PAD]
            ph, plo = _split_hi_lo(p_sum)
            imp = _bdot(ph, ov_ref[...]) + _bdot(plo, ov_ref[...])
            score = jnp.where(j_idx <= tb, imp + jnp.where(forced, NSA_FORCE_BONUS, 0.0), -jnp.inf)
            sel_ref[0, kvh] = _top_blocks(score, n_blk, NSA_TOPN)
            ocmp_ref[0, :, kvh * NSA_G * HD:(kvh + 1) * NSA_G * HD] = jnp.concatenate(
                [o_cmp[g * T_PAD:(g + 1) * T_PAD] for g in range(NSA_G)], axis=1)


def _rows_minor(a):
    nd = a.ndim
    return a.transpose((0,) + tuple(range(2, nd)) + (1,))


def _rows_major(a):
    nd = a.ndim
    return a.transpose((0, nd - 1) + tuple(range(1, nd - 1)))


def _page_specs(slot_pair):
    return [pl.BlockSpec((1, 2, NSA_KVH, HD, PAGE_SIZE),
                         lambda b, s, pt, k=k: (pt[b, s * PAGES_PER_STEP + k], slot_pair, 0, 0, 0))
            for k in range(PAGES_PER_STEP)]


def _nsa_sample_cmp(q, cache, page_table, cw, n_valid):
    w1r, pe2, w2 = cw
    nb, n_pages = page_table.shape
    n_rows = n_pages * PAGE_SIZE
    n_ch = n_rows // NSA_CMP_STRIDE
    n_blk = (n_rows + n_valid + NSA_SEL_LEN - 1) // NSA_SEL_LEN
    lanes = -(-n_blk // 128) * 128
    ov = _overlap_matrix(n_ch, n_ch - 1, n_blk, lanes)
    bspec = lambda shape: pl.BlockSpec(shape, lambda b, s, pt: (b,) + (0,) * (len(shape) - 1))
    cspec = lambda shape: pl.BlockSpec(shape, lambda b, s, pt: (0,) * len(shape))
    grid_spec = pltpu.PrefetchScalarGridSpec(
        num_scalar_prefetch=1,
        grid=(nb, n_pages // PAGES_PER_STEP),
        in_specs=_page_specs(0) + [bspec((1, T_PAD, q.shape[2])), cspec(w1r.shape), cspec(pe2.shape),
                                    cspec(w2.shape), cspec(ov.shape)],
        out_specs=[bspec((1, T_PAD, q.shape[2])), bspec((1, NSA_KVH, T_PAD, lanes))],
        scratch_shapes=[pltpu.VMEM((4, n_rows + CMP_TAIL, 128), F32)],
    )
    return pl.pallas_call(
        functools.partial(_nsa_sample_cmp_kernel, n_pages=n_pages, n_valid=n_valid),
        grid_spec=grid_spec,
        out_shape=[jax.ShapeDtypeStruct(q.shape, F32), jax.ShapeDtypeStruct((nb, NSA_KVH, T_PAD, lanes), F32)],
        compiler_params=_params(("arbitrary", "arbitrary"), 56),
        name="nsa_sample_cmp",
    )(page_table, *([cache] * PAGES_PER_STEP), q, w1r, pe2, w2, ov)


def _nsa_sample_attend_kernel(pt_ref, *refs, n_pages, n_valid):
    del pt_ref
    pages = refs[:PAGES_PER_STEP]
    (q_ref, sel_ref, ocmp_ref, new4_ref, neww_ref, win_ref, gt_ref, ex_ref, o_ref, ks_sc, vs_sc) = refs[PAGES_PER_STEP:]
    step = pl.program_id(1)
    n_rows = n_pages * PAGE_SIZE

    for k, page in enumerate(pages):
        r0 = pl.multiple_of((step * PAGES_PER_STEP + k) * PAGE_SIZE, PAGE_SIZE)
        for slot, dst in enumerate((ks_sc, vs_sc)):
            for h in range(NSA_KVH):
                dst[h, :, pl.ds(r0, PAGE_SIZE)] = page[0, slot, h].astype(BF16)

    @pl.when(step == pl.num_programs(1) - 1)
    def _():
        def tile_g(x):
            return jnp.concatenate([x] * NSA_G, axis=0)

        lb = win_ref.shape[4]
        t_idx = lax.broadcasted_iota(jnp.int32, (T_PAD, 1), 0)
        i_new = lax.broadcasted_iota(jnp.int32, (T_PAD, T_PAD), 1)
        new_mask = tile_g((i_new <= t_idx) & (i_new < n_valid))
        w_idx = lax.broadcasted_iota(jnp.int32, (T_PAD, lb), 1)
        win_mask = tile_g(w_idx >= lb + t_idx - NSA_WIN)
        gates = jax.nn.sigmoid(gt_ref[0])
        for kvh in range(NSA_KVH):
            c0 = kvh * HD
            qs = _stack_heads(q_ref, kvh)
            chosen = tile_g(_bdot(sel_ref[0, kvh].astype(BF16), ex_ref[...]) > 0.5)
            k_new = new4_ref[0, :, 2 * KV_COLS + c0:2 * KV_COLS + c0 + HD].astype(BF16)
            v_new = new4_ref[0, :, 3 * KV_COLS + c0:3 * KV_COLS + c0 + HD].astype(BF16)
            o_sel, _, _ = _softmax_two(_bdot(qs, ks_sc[kvh]), chosen, vs_sc[kvh], _dot_nt(qs, k_new), new_mask, v_new)
            kw_old = win_ref[0, 0, kvh].astype(BF16)
            vw_old = win_ref[0, 1, kvh].astype(BF16)
            kw_new = neww_ref[0, :, c0:c0 + HD].astype(BF16)
            vw_new = neww_ref[0, :, KV_COLS + c0:KV_COLS + c0 + HD].astype(BF16)
            o_win, _, _ = _softmax_two(_bdot(qs, kw_old), win_mask, vw_old, _dot_nt(qs, kw_new), new_mask, vw_new)
            outs = []
            for g in range(NSA_G):
                h = kvh * NSA_G + g
                r = slice(g * T_PAD, (g + 1) * T_PAD)
                outs.append(gates[:, 3 * h:3 * h + 1] * ocmp_ref[0, :, h * HD:(h + 1) * HD]
                            + gates[:, 3 * h + 1:3 * h + 2] * o_sel[r] + gates[:, 3 * h + 2:3 * h + 3] * o_win[r])
            o_ref[0, :, kvh * NSA_G * HD:(kvh + 1) * NSA_G * HD] = jnp.concatenate(outs, axis=1).astype(o_ref.dtype)


def _nsa_sample_attend(q, sel, ocmp, new4, neww, win, gates, cache, page_table, n_valid):
    nb, n_pages = page_table.shape
    n_rows = n_pages * PAGE_SIZE
    lanes = sel.shape[3]
    ex = _expand_matrix(n_rows, n_rows, lanes)[0]
    bspec = lambda shape: pl.BlockSpec(shape, lambda b, s, pt: (b,) + (0,) * (len(shape) - 1))
    cspec = lambda shape: pl.BlockSpec(shape, lambda b, s, pt: (0,) * len(shape))
    grid_spec = pltpu.PrefetchScalarGridSpec(
        num_scalar_prefetch=1,
        grid=(nb, n_pages // PAGES_PER_STEP),
        in_specs=_page_specs(1) + [
            bspec((1, T_PAD, q.shape[2])), bspec((1,) + sel.shape[1:]), bspec((1, T_PAD, ocmp.shape[2])),
            bspec((1, T_PAD, new4.shape[2])), bspec((1, T_PAD, neww.shape[2])), bspec((1,) + win.shape[1:]),
            bspec((1, T_PAD, gates.shape[2])), cspec(ex.shape)],
        out_specs=bspec((1, T_PAD, q.shape[2])),
        scratch_shapes=[pltpu.VMEM((NSA_KVH, HD, n_rows), BF16)] * 2,
    )
    return pl.pallas_call(
        functools.partial(_nsa_sample_attend_kernel, n_pages=n_pages, n_valid=n_valid),
        grid_spec=grid_spec,
        out_shape=jax.ShapeDtypeStruct(q.shape, BF16),
        compiler_params=_params(("arbitrary", "arbitrary"), 56),
        name="nsa_sample_attend",
    )(page_table, *([cache] * PAGES_PER_STEP), q, sel, ocmp, new4, neww, win, gates, ex)


def _nsa_sample(tok, x, g, nb, cache, win, page_table, n_valid, w_in_parts, cw, w_out, name="nsa_s"):
    q, kv4, kvw, gates = _mod_proj(tok, x, g, w_in_parts, [F32, F32, F32, F32], name + "_proj")
    r3 = lambda a: a.reshape(nb, T_PAD, a.shape[1])
    cache_t, win_t = _rows_minor(cache), _rows_minor(win)
    ocmp, sel = _nsa_sample_cmp(r3(q), cache_t, page_table, cw, n_valid)
    o = _nsa_sample_attend(r3(q), sel, ocmp, r3(kv4), r3(kvw), win_t, r3(gates), cache_t, page_table, n_valid)
    x_new = _out_proj(tok, x, [o.reshape(nb * T_PAD, o.shape[2])], w_out, "plain", name + "_out")
    return x_new, r3(kv4), r3(kvw)


DIL_PAIRS = ((128, 1), (512, 4), (2048, 16))
DIL_SLOTS = 8
DIL_COLS = DIL_SLOTS * HD


def _dil_prompt_kernel(q_ref, k_ref, v_ref, o_ref, l_ref, *, tq, ls, nback):
    t0 = pl.program_id(2) * tq
    wkeys = min(nback + tq, ls)
    w0 = pl.multiple_of(jnp.maximum(t0 - nback, 0), tq)
    tpos = t0 + lax.broadcasted_iota(jnp.int32, (tq, 1), 0)
    dist = tpos - (w0 + lax.broadcasted_iota(jnp.int32, (tq, wkeys), 1))
    bias = jnp.where((dist >= 0) & (dist <= nback), 0.0, MASK_BIAS)
    outs, lses = [], []
    for h in range(DIL_SLOTS):
        c0 = h * HD
        qh = (q_ref[:, c0:c0 + HD] * (HD ** -0.5)).astype(BF16)
        kh = k_ref[pl.ds(w0, wkeys), c0:c0 + HD].astype(BF16)
        vh = v_ref[pl.ds(w0, wkeys), c0:c0 + HD].astype(BF16)
        o, lse = _biased_attention(qh, kh, vh, bias)
        outs.append(o)
        lses.append(jnp.broadcast_to(lse, (tq, HD)))
    o_ref[...] = jnp.concatenate(outs, axis=1)
    l_ref[...] = jnp.concatenate(lses, axis=1)


def _biased_attention(q, k, v, bias):
    s = _dot_nt(q, k) + bias
    m = jnp.max(s, axis=-1, keepdims=True)
    e = jnp.exp(s - m)
    den = jnp.sum(e, axis=-1, keepdims=True)
    return _bdot(e.astype(BF16), v) / den, m + jnp.log(den)


def _dil_prompt_rows16_kernel(q_ref, kv_ref, o_ref, l_ref, *, dil, nback):
    n_a = q_ref.shape[0]
    sets = 16 // dil
    a_q = lax.broadcasted_iota(jnp.int32, (n_a, n_a), 0)
    a_k = lax.broadcasted_iota(jnp.int32, (n_a, n_a), 1)
    for r in range(dil):
        for h in range(DIL_SLOTS):
            c0 = h * HD
            ks = [kv_ref[:, r + dil * e, c0:c0 + HD].astype(BF16) for e in range(sets)]
            vs = [kv_ref[:, r + dil * e, DIL_COLS + c0:DIL_COLS + c0 + HD].astype(BF16) for e in range(sets)]
            k_all = jnp.concatenate(ks, axis=0) if sets > 1 else ks[0]
            v_all = jnp.concatenate(vs, axis=0) if sets > 1 else vs[0]
            for e in range(sets):
                dist = jnp.concatenate([(sets * a_q + e) - (sets * a_k + e2) for e2 in range(sets)], axis=1) \
                    if sets > 1 else a_q - a_k
                bias = jnp.where((dist >= 0) & (dist <= nback), 0.0, MASK_BIAS)
                qh = (q_ref[:, r + dil * e, c0:c0 + HD] * (HD ** -0.5)).astype(BF16)
                o, lse = _biased_attention(qh, k_all, v_all, bias)
                o_ref[:, r + dil * e, c0:c0 + HD] = o
                l_ref[:, r + dil * e, c0:c0 + HD] = jnp.broadcast_to(lse, (n_a, HD))


def _dil_prompt_group_rows16(q3, kv, gi, nb, seq):
    window, dil = DIL_PAIRS[gi]
    n_a = seq // 16
    qv = q3.reshape(nb * n_a, 16, 3 * DIL_COLS)
    kvv = kv.reshape(nb * n_a, 16, 2 * DIL_COLS)
    out_sd = jax.ShapeDtypeStruct((nb * n_a, 16, DIL_COLS), F32)
    o, lse = pl.pallas_call(
        functools.partial(_dil_prompt_rows16_kernel, dil=dil, nback=window // dil),
        grid=(nb,),
        in_specs=[
            pl.BlockSpec((n_a, 16, DIL_COLS), lambda b: (b, 0, gi)),
            pl.BlockSpec((n_a, 16, 2 * DIL_COLS), lambda b: (b, 0, 0)),
        ],
        out_specs=[pl.BlockSpec((n_a, 16, DIL_COLS), lambda b: (b, 0, 0))] * 2,
        out_shape=[out_sd, out_sd],
        compiler_params=_params(("arbitrary",), 52),
        name=f"dil_prompt_g{gi}",
    )(qv, kvv)
    return o.reshape(nb * seq, DIL_COLS), lse.reshape(nb * seq, DIL_COLS)


def _dil_prompt_group(q3, kv, gi, nb, seq, tq=128):
    window, dil = DIL_PAIRS[gi]
    ls = seq // dil
    nq = ls // tq
    qv = q3.reshape(nb * ls, dil * 3 * DIL_COLS)
    kvv = kv.reshape(nb * ls, dil * 2 * DIL_COLS)
    out_sd = jax.ShapeDtypeStruct((nb * ls, dil * DIL_COLS), F32)
    o, lse = pl.pallas_call(
        functools.partial(_dil_prompt_kernel, tq=tq, ls=ls, nback=window // dil),
        grid=(nb, dil, nq),
        in_specs=[
            pl.BlockSpec((tq, DIL_COLS), lambda b, r, i: (b * nq + i, 3 * r + gi)),
            pl.BlockSpec((ls, DIL_COLS), lambda b, r, i: (b, 2 * r)),
            pl.BlockSpec((ls, DIL_COLS), lambda b, r, i: (b, 2 * r + 1)),
        ],
        out_specs=[pl.BlockSpec((tq, DIL_COLS), lambda b, r, i: (b * nq + i, r))] * 2,
        out_shape=[out_sd, out_sd],
        compiler_params=_params(("arbitrary", "arbitrary", "arbitrary"), 40),
        name=f"dil_prompt_g{gi}",
    )(qv, kvv, kvv)
    return o.reshape(nb * seq, DIL_COLS), lse.reshape(nb * seq, DIL_COLS)


def _dil_weights(w_in):
    n_g = len(DIL_PAIRS)
    w = w_in.reshape(D_MODEL, 3, n_g, DIL_COLS)
    parts = [w[:, 0].reshape(D_MODEL, n_g * DIL_COLS)]
    for gi in range(n_g):
        parts.append(jnp.concatenate([w[:, 1, gi], w[:, 2, gi]], axis=1))
    return [p.astype(BF16) for p in parts]


def _dil_prompt(tok, x, g, nb, seq, w_in_parts, w_out, name="dil"):
    q3, kv0, kv1, kv2, kv0_t, kv1_t, kv2_t = _mod_proj(tok, x, g, w_in_parts, [F32] * 4, name + "_proj",
                                                       rows_minor=(1, 2, 3))
    kvs = (kv0, kv1, kv2)
    outs, lses = [], []
    for gi in range(len(DIL_PAIRS)):
        group = _dil_prompt_group_rows16 if DIL_PAIRS[gi][1] == 16 else _dil_prompt_group
        o, lse = group(q3, kvs[gi], gi, nb, seq)
        outs.append(o)
        lses.append(lse)
    x_new = _out_proj(tok, x, outs + lses, w_out, "dil", name + "_out")
    return x_new, (kv0_t, kv1_t, kv2_t)


def _shifted_state(old_t, new_rows, n_valid):
    hd, lb = old_t.shape
    rolled = pltpu.roll(old_t, lb - n_valid, 1)
    tail = jnp.concatenate([new_rows.T, jnp.zeros((hd, 128 - T_PAD), F32)], axis=1)
    tail = pltpu.roll(tail, 128 - n_valid, 1)
    lane = lax.broadcasted_iota(jnp.int32, (hd, 128), 1)
    last = jnp.where(lane >= 128 - n_valid, tail, rolled[:, lb - 128:])
    return rolled, last


def _dil_sample_kernel(q_ref, st_ref, new_ref, o_ref, l_ref, ns_ref, *, window, dil, n_valid):
    lb = st_ref.shape[4]
    for kv in range(2):
        for h in range(DIL_SLOTS):
            c0 = kv * DIL_COLS + h * HD
            rolled, last = _shifted_state(st_ref[0, kv, h], new_ref[0, :, c0:c0 + HD], n_valid)
            if lb > 128:
                ns_ref[0, kv, h, :, 0:lb - 128] = rolled[:, 0:lb - 128]
            ns_ref[0, kv, h, :, lb - 128:lb] = last
    t_idx = lax.broadcasted_iota(jnp.int32, (T_PAD, 1), 0)
    d_old = lb + t_idx - lax.broadcasted_iota(jnp.int32, (T_PAD, lb), 1)
    old_mask = ((d_old & (dil - 1)) == 0) & (d_old <= window)
    i_new = lax.broadcasted_iota(jnp.int32, (T_PAD, T_PAD), 1)
    d_new = t_idx - i_new
    new_mask = (d_new >= 0) & ((d_new & (dil - 1)) == 0) & (i_new < n_valid)
    outs, lses = [], []
    for h in range(DIL_SLOTS):
        c0 = h * HD
        qh = (q_ref[0, :, c0:c0 + HD] * (HD ** -0.5)).astype(BF16)
        k_old = st_ref[0, 0, h].astype(BF16)
        v_old = st_ref[0, 1, h].astype(BF16)
        k_new = new_ref[0, :, c0:c0 + HD].astype(BF16)
        v_new = new_ref[0, :, DIL_COLS + c0:DIL_COLS + c0 + HD].astype(BF16)
        o, m, den = _softmax_two(_bdot(qh, k_old), old_mask, v_old, _dot_nt(qh, k_new), new_mask, v_new)
        outs.append(o)
        lses.append(jnp.broadcast_to(m + jnp.log(den), (T_PAD, HD)))
    o_ref[0] = jnp.concatenate(outs, axis=1)
    l_ref[0] = jnp.concatenate(lses, axis=1)


def _dil_sample_group(q3, kv_new, state, gi, n_valid):
    window, dil = DIL_PAIRS[gi]
    nb, lb = state.shape[:2]
    assert lb == window and lb % 128 == 0, "the next state is the shifted buffer only for a full window"
    state_t = _rows_minor(state)
    out_sd = jax.ShapeDtypeStruct((nb, T_PAD, DIL_COLS), F32)
    return pl.pallas_call(
        functools.partial(_dil_sample_kernel, window=window, dil=dil, n_valid=n_valid),
        grid=(nb,),
        in_specs=[
            pl.BlockSpec((1, T_PAD, DIL_COLS), lambda b: (b, 0, gi)),
            pl.BlockSpec((1, 2, DIL_SLOTS, HD, lb), lambda b: (b, 0, 0, 0, 0)),
            pl.BlockSpec((1, T_PAD, 2 * DIL_COLS), lambda b: (b, 0, 0)),
        ],
        out_specs=[pl.BlockSpec((1, T_PAD, DIL_COLS), lambda b: (b, 0, 0))] * 2
        + [pl.BlockSpec((1, 2, DIL_SLOTS, HD, lb), lambda b: (b, 0, 0, 0, 0))],
        out_shape=[out_sd, out_sd, jax.ShapeDtypeStruct(state_t.shape, F32)],
        compiler_params=_params(("arbitrary",), 56),
        name=f"dil_sample_g{gi}",
    )(q3, state_t, kv_new)


def _dil_sample(tok, x, g, nb, states, n_valid, w_in_parts, w_out, name="dil_s"):
    q3, kv0, kv1, kv2 = _mod_proj(tok, x, g, w_in_parts, [F32] * 4, name + "_proj")
    r3 = lambda a: a.reshape(nb, T_PAD, a.shape[1])
    kvs = (r3(kv0), r3(kv1), r3(kv2))
    outs, lses, new_states = [], [], []
    for gi in range(len(DIL_PAIRS)):
        o, lse, ns = _dil_sample_group(r3(q3), kvs[gi], states[gi], gi, n_valid)
        outs.append(o.reshape(nb * T_PAD, DIL_COLS))
        lses.append(lse.reshape(nb * T_PAD, DIL_COLS))
        new_states.append(_rows_major(ns))
    x_new = _out_proj(tok, x, outs + lses, w_out, "dil", name + "_out")
    return x_new, new_states


GLA_H, GLA_DK, GLA_DV, GLA_RANK, GLA_TAU = 4, 128, 256, 16, 16.0
HG_H, HG_DK, HG_DV = 8, 128, 128
SCAN_SUB = 8
SCAN_HEADS_PER_STEP = 4
SCAN_UNROLL = 2


def _log_sigmoid(x):
    return jnp.minimum(x, 0.0) - jnp.log1p(jnp.exp(-jnp.abs(x)))


def _cumsum_rows(g):
    n = g.shape[0]
    tri = (lax.broadcasted_iota(jnp.int32, (n, n), 0) >= lax.broadcasted_iota(jnp.int32, (n, n), 1)).astype(BF16)
    hi = g.astype(BF16)
    r1 = g - hi.astype(F32)
    mid = r1.astype(BF16)
    lo = (r1 - mid.astype(F32)).astype(BF16)
    return _bdot(tri, hi) + _bdot(tri, mid) + _bdot(tri, lo)


def _scan_chunk(q, k, g, v, st):
    n, dk = q.shape
    sub = min(SCAN_SUB, n)
    b = _cumsum_rows(g)
    b_end = b[n - 1:n]
    o = _dot_nt((q * jnp.exp(b)).astype(BF16), st.astype(BF16))

    lane = lax.broadcasted_iota(jnp.int32, (sub, n), 1)
    row = lax.broadcasted_iota(jnp.int32, (sub, n), 0)
    ones = jnp.ones((dk, n), BF16)
    a_rows = []
    for i in range(n // sub):
        lo = i * sub
        qi, ki, bi = q[lo:lo + sub], k[lo:lo + sub], b[lo:lo + sub]
        prods = []
        for s in range(sub):
            e = jnp.exp(jnp.minimum(bi - bi[s:s + 1], 0.0))
            prods.append((qi * ki[s:s + 1] * e).astype(BF16))
        sums = _bdot(jnp.concatenate(prods, axis=0), ones)
        a_i = jnp.zeros((sub, n), F32)
        for s in range(sub):
            a_i = a_i + jnp.where((lane == lo + s) & (row >= s), sums[s * sub:(s + 1) * sub], 0.0)
        if i > 0:
            b_ref = b[lo - 1:lo]
            qd = (qi * jnp.exp(bi - b_ref)).astype(BF16)
            kd = (k[0:lo] * jnp.exp(b_ref - b[0:lo])).astype(BF16)
            if lo < n:
                kd = jnp.concatenate([kd, jnp.zeros((n - lo, dk), BF16)], axis=0)
            a_i = a_i + _dot_nt(qd, kd)
        a_rows.append(a_i)
    a = jnp.concatenate(a_rows, axis=0) if len(a_rows) > 1 else a_rows[0]
    o = o + _bdot(a.astype(BF16), v.astype(BF16))
    kd_end = (k * jnp.exp(b_end - b)).astype(BF16)
    st_new = st * jnp.exp(b_end) + _dot_tn(v.astype(BF16), kd_end)
    return o, st_new


def _scan_kernel(*refs, kind, chunk, n_chunks, n_valid, hpb, dk, dv):
    if kind == "gla":
        q_ref, k_ref, v_ref, a_ref, wa_ref, ba_ref, s0_ref, o_ref, sf_ref, st_sc = refs
    else:
        f_ref, v_ref, q_ref, lb_ref, s0_ref, o_ref, sf_ref, st_sc = refs
    ci = pl.program_id(2)

    @pl.when(ci == 0)
    def _():
        for hh in range(hpb):
            st_sc[hh] = s0_ref[0, hh].T

    def body(j, carry):
        r0 = pl.multiple_of(j * chunk, chunk)
        rows = pl.ds(r0, chunk)
        for hh in range(hpb):
            kc = slice(hh * dk, (hh + 1) * dk)
            vc = slice(hh * dv, (hh + 1) * dv)
            if kind == "gla":
                q = q_ref[rows, kc] * (GLA_DK ** -0.5)
                k = k_ref[rows, kc]
                pre = _bdot(a_ref[rows, :].astype(BF16), wa_ref[:, kc]) + ba_ref[:, kc]
                g = _log_sigmoid(pre) * (1.0 / GLA_TAU)
            else:
                fz = f_ref[rows, kc]
                lb = lb_ref[:, kc]
                q = _silu(q_ref[rows, kc])
                la = jnp.log(lb)
                lc = jnp.log1p(-lb) + _log_sigmoid(fz)
                g = jnp.maximum(la, lc) + jnp.log1p(jnp.exp(-jnp.abs(la - lc)))
                k = (1.0 - lb) * jax.nn.sigmoid(-fz)
            if n_valid < chunk:
                live = lax.broadcasted_iota(jnp.int32, k.shape, 0) < n_valid
                k = jnp.where(live, k, 0.0)
                g = jnp.where(live, g, 0.0)
            o, st = _scan_chunk(q, k, g, v_ref[rows, vc], st_sc[hh])
            st_sc[hh] = st
            o_ref[rows, vc] = o
        return carry

    lax.fori_loop(0, n_chunks, body, 0, unroll=min(SCAN_UNROLL, n_chunks))

    @pl.when(ci == pl.num_programs(2) - 1)
    def _():
        for hh in range(hpb):
            sf_ref[0, hh] = st_sc[hh].T


def _scan(kind, ins, s0_t, nb, rows_per_b, n_valid, name):
    if kind == "gla":
        heads, dk, dv = GLA_H, GLA_DK, GLA_DV
    else:
        heads, dk, dv = HG_H, HG_DK, HG_DV
    chunk = min(64, rows_per_b)
    blk = min(512, rows_per_b)
    nblk = rows_per_b // blk
    hpb = SCAN_HEADS_PER_STEP
    rspec = lambda w: pl.BlockSpec((blk, hpb * w), lambda b, h, i: (b * nblk + i, h))
    if kind == "gla":
        q, k, v, a, wa, ba = ins
        args = [q, k, v, a, wa, ba]
        in_specs = [rspec(dk), rspec(dk), rspec(dv),
                    pl.BlockSpec((blk, a.shape[1]), lambda b, h, i: (b * nblk + i, 0)),
                    pl.BlockSpec((wa.shape[0], hpb * dk), lambda b, h, i: (0, h)),
                    pl.BlockSpec((1, hpb * dk), lambda b, h, i: (0, h))]
    else:
        f, v, q, lb = ins
        args = [f, v, q, lb]
        in_specs = [rspec(dk), rspec(dv), rspec(dk), pl.BlockSpec((1, hpb * dk), lambda b, h, i: (0, h))]
    m = nb * rows_per_b
    st_spec = pl.BlockSpec((1, hpb, dk, dv), lambda b, h, i: (b, h, 0, 0))
    return pl.pallas_call(
        functools.partial(_scan_kernel, kind=kind, chunk=chunk, n_chunks=blk // chunk, n_valid=n_valid,
                          hpb=hpb, dk=dk, dv=dv),
        grid=(nb, heads // hpb, nblk),
        in_specs=in_specs + [st_spec],
        out_specs=[rspec(dv), st_spec],
        out_shape=[jax.ShapeDtypeStruct((m, heads * dv), F32), jax.ShapeDtypeStruct((nb, heads, dk, dv), F32)],
        scratch_shapes=[pltpu.VMEM((hpb, dv, dk), F32)],
        compiler_params=_params(("arbitrary", "arbitrary", "arbitrary"), 40),
        name=name,
    )(*args, s0_t)


def _gla_weights(w_in, w_a2, b_a2):
    nk, nv = GLA_H * GLA_DK, GLA_H * GLA_DV
    wa = jnp.pad(w_in[:, 2 * nk + 2 * nv:], ((0, 0), (0, 128 - GLA_RANK)))
    parts = [w_in[:, :nk], w_in[:, nk:2 * nk], w_in[:, 2 * nk:2 * nk + nv], w_in[:, 2 * nk + nv:2 * nk + 2 * nv], wa]
    wa2 = jnp.pad(w_a2, ((0, 128 - GLA_RANK), (0, 0))).astype(BF16)
    return [p.astype(BF16) for p in parts], wa2, b_a2.reshape(1, nk)


def _gla(tok, x, g, s0_t, nb, rows_per_b, n_valid, wts, norm_g, w_out, name="gla"):
    parts, wa2, ba2 = wts
    q, k, v, r, a = _mod_proj(tok, x, g, parts, [F32] * 5, name + "_proj")
    o, s_t = _scan("gla", [q, k, v, a, wa2, ba2], s0_t, nb, rows_per_b, n_valid, name + "_scan")
    x_new = _out_proj(tok, x, [o, r, norm_g.reshape(1, GLA_DV)], w_out, "heads", name + "_out", heads=GLA_H)
    return x_new, s_t


def _hgrn_weights(w_in, lb_logits, layer):
    nk, nv = HG_H * HG_DK, HG_H * HG_DV
    parts = [w_in[:, :nk], w_in[:, nk:nk + nv], w_in[:, nk + nv:2 * nk + nv], w_in[:, 2 * nk + nv:]]
    sm = jax.nn.softmax(lb_logits.astype(F32), axis=0)
    lb = jnp.sum(sm[1:layer + 1], axis=0).reshape(1, nk)
    return [p.astype(BF16) for p in parts], lb


def _hgrn(tok, x, g, s0_t, nb, rows_per_b, n_valid, wts, norm_g, w_out, name="hgrn"):
    parts, lb = wts
    f, i_in, q, og = _mod_proj(tok, x, g, parts, [F32] * 4, name + "_proj")
    o, s_t = _scan("hgrn", [f, i_in, q, lb], s0_t, nb, rows_per_b, n_valid, name + "_scan")
    x_new = _out_proj(tok, x, [o, og, norm_g.reshape(1, HG_DV)], w_out, "heads", name + "_out", heads=HG_H)
    return x_new, s_t


PROMPT_TM = 256
PROMPT_MOE_TM = 512


def _moe_weights(w_rg, b_rg, w_re, b_re, w_up, w_down):
    n_e = MOE_GROUPS * MOE_PER_GROUP
    wr = jnp.zeros((D_MODEL, ROUTER_LANES), F32).at[:, :MOE_GROUPS].set(w_rg).at[:, MOE_GROUPS:MOE_GROUPS + n_e].set(w_re)
    br = jnp.zeros((1, ROUTER_LANES), F32).at[0, :MOE_GROUPS].set(b_rg).at[0, MOE_GROUPS:MOE_GROUPS + n_e].set(b_re)
    wr_hi, wr_lo = _split_hi_lo(wr)
    dn = w_down.reshape(MOE_GROUPS, MOE_PER_GROUP * MOE_FF, D_MODEL).astype(BF16)
    return wr_hi, wr_lo, br, w_up.astype(BF16), dn


def kernel(x_prompt, x_sample, cache_nsa_kv, state_nsa_win, state_dil_0, state_dil_1, state_dil_2, state_gla, state_hgrn, page_table, c_prompt, c_sample, nsa_w_in, nsa_cmp_pe, nsa_cmp_w1, nsa_cmp_w2, nsa_w_out, dil_w_in, dil_w_out, gla_w_in, gla_w_a2, gla_b_a2, gla_norm_g, gla_w_out, hg_w_in, hg_lb_logits, hg_norm_g, hg_w_out, norm_g, ada_w, ada_b, moe_w_rg, moe_b_rg, moe_w_re, moe_b_re, moe_w_up, moe_w_down, final_norm_g):
    nb, seq, _ = x_prompt.shape
    ndb, n_t, _ = x_sample.shape
    depth = ada_w.shape[0]
    ms_rows = ndb * T_PAD

    c_rows = -(-(nb + ndb) // 16) * 16
    c_all = jnp.zeros((c_rows, D_MODEL), F32).at[:nb].set(c_prompt).at[nb:nb + ndb].set(c_sample)
    mod = _adaln(c_all, ada_w, ada_b)

    xp = x_prompt.reshape(nb * seq, D_MODEL)
    xs = jnp.pad(x_sample, ((0, 0), (0, T_PAD - n_t), (0, 0))).reshape(ms_rows, D_MODEL)

    outs = {}
    for i in range(depth):
        mp = mod[i, :nb].reshape(nb, ADA_N, D_MODEL).transpose(1, 0, 2).reshape(ADA_N * nb, 1, D_MODEL)
        ms = mod[i, nb:nb + ndb].reshape(ndb, ADA_N, D_MODEL).transpose(1, 0, 2)
        ms = jnp.repeat(ms, T_PAD, axis=1)
        tok_p = _Tokens(mp, False, nb, seq, PROMPT_TM)
        tok_pm = _Tokens(mp, False, nb, seq, PROMPT_MOE_TM)
        tok_s = _Tokens(ms, True, 1, ms_rows, ms_rows)
        g_mix, g_moe = norm_g[i, 0], norm_g[i, 1]
        kind = i % 4
        if kind == 0:
            parts = _nsa_weights(nsa_w_in)
            cw = _compress_weights(nsa_cmp_pe, nsa_cmp_w1, nsa_cmp_w2)
            w_out = nsa_w_out.astype(BF16)
            xp, kv4, kvw = _nsa_prompt(tok_p, xp, g_mix, nb, seq, parts, cw, w_out)
            xs, kv4_s, kvw_s = _nsa_sample(tok_s, xs, g_mix, ndb, cache_nsa_kv, state_nsa_win, page_table, n_t,
                                           parts, cw, w_out)
            keep = min(NSA_WIN, seq)
            outs["nsa_kv_p"] = _rows_major(kv4.reshape(nb, 4, NSA_KVH, HD, seq))
            outs["nsa_kv_s"] = kv4_s[:, :n_t].reshape(ndb, n_t, 4, NSA_KVH, HD).astype(cache_nsa_kv.dtype)
            outs["nsa_win_p"] = _rows_major(kvw.reshape(nb, 2, NSA_KVH, HD, seq)[..., seq - keep:])
            win_all = jnp.concatenate(
                [state_nsa_win, kvw_s[:, :n_t].reshape(ndb, n_t, 2, NSA_KVH, HD).astype(state_nsa_win.dtype)], axis=1)
            outs["nsa_win_s"] = win_all[:, win_all.shape[1] - min(NSA_WIN, win_all.shape[1]):]
        elif kind == 1:
            parts = _dil_weights(dil_w_in)
            w_out = dil_w_out.astype(BF16)
            states = (state_dil_0, state_dil_1, state_dil_2)
            xp, kvs = _dil_prompt(tok_p, xp, g_mix, nb, seq, parts, w_out)
            xs, kvs_s = _dil_sample(tok_s, xs, g_mix, ndb, states, n_t, parts, w_out)
            for gi, (window, _) in enumerate(DIL_PAIRS):
                buf = kvs[gi].reshape(nb, 2, DIL_SLOTS, HD, seq)
                outs[f"dil_p{gi}"] = _rows_major(buf[..., seq - min(window, seq):])
                outs[f"dil_s{gi}"] = kvs_s[gi].astype(states[gi].dtype)
        elif kind == 2:
            wts = _gla_weights(gla_w_in, gla_w_a2, gla_b_a2)
            w_out = gla_w_out.astype(BF16)
            zero = jnp.zeros((nb, GLA_H, GLA_DK, GLA_DV), F32)
            xp, s_p = _gla(tok_p, xp, g_mix, zero, nb, seq, seq, wts, gla_norm_g, w_out)
            xs, s_s = _gla(tok_s, xs, g_mix, state_gla.astype(F32), ndb, T_PAD, n_t, wts,
                           gla_norm_g, w_out, name="gla_s")
            outs["gla_p"] = s_p.astype(state_gla.dtype)
            outs["gla_s"] = s_s.astype(state_gla.dtype)
        else:
            wts = _hgrn_weights(hg_w_in, hg_lb_logits, i)
            w_out = hg_w_out.astype(BF16)
            zero = jnp.zeros((nb, HG_H, HG_DK, HG_DV), F32)
            xp, s_p = _hgrn(tok_p, xp, g_mix, zero, nb, seq, seq, wts, hg_norm_g, w_out)
            xs, s_s = _hgrn(tok_s, xs, g_mix, state_hgrn.astype(F32), ndb, T_PAD, n_t, wts,
                            hg_norm_g, w_out, name="hgrn_s")
            outs["hg_p"] = s_p.astype(state_hgrn.dtype)
            outs["hg_s"] = s_s.astype(state_hgrn.dtype)
        mw = _moe_weights(moe_w_rg[i], moe_b_rg[i], moe_w_re[i], moe_b_re[i], moe_w_up[i], moe_w_down[i])
        fg = final_norm_g if i == depth - 1 else None
        xp = _moe(tok_pm, xp, g_moe, *mw, fg, f"moe_p{i}")
        xs = _moe(tok_s, xs, g_moe, *mw, fg, f"moe_s{i}")

    y_prompt = xp.reshape(nb, seq, D_MODEL)
    y_sample = xs.reshape(ndb, T_PAD, D_MODEL)[:, :n_t]
    return (y_prompt, y_sample, outs["nsa_kv_p"], outs["nsa_kv_s"], outs["nsa_win_p"], outs["nsa_win_s"],
            outs["dil_p0"], outs["dil_s0"], outs["dil_p1"], outs["dil_s1"], outs["dil_p2"], outs["dil_s2"],
            outs["gla_p"], outs["gla_s"], outs["hg_p"], outs["hg_s"])
```

```python
import functools

import jax
import jax.numpy as jnp
from jax import lax
from jax.experimental import pallas as pl
from jax.experimental.pallas import tpu as pltpu

F32 = jnp.float32
BF16 = jnp.bfloat16
EPS = 1e-6
D_MODEL = 1024
HD = 64
ADA_N = 6
MIB = 1024 * 1024
NEG_BIG = -1e30
MASK_BIAS = -2e30


def _params(sem, vmem_mib):
    return pltpu.CompilerParams(dimension_semantics=sem, vmem_limit_bytes=vmem_mib * MIB)


def _silu(x):
    return x * jax.nn.sigmoid(x)


def _bdot(a, b):
    return jnp.dot(a, b, preferred_element_type=F32)


def _dot_nt(a, b):
    return lax.dot_general(a, b, (((1,), (1,)), ((), ())), preferred_element_type=F32)


def _dot_tn(a, b):
    return lax.dot_general(a, b, (((0,), (0,)), ((), ())), preferred_element_type=F32)


def _split_hi_lo(x):
    hi = x.astype(BF16)
    lo = (x - hi.astype(F32)).astype(BF16)
    return hi, lo


def _rms(x):
    return x * lax.rsqrt(jnp.mean(x * x, axis=-1, keepdims=True) + EPS)


class _Tokens:
    def __init__(self, mod, per_row, nb, rows_per_b, tm):
        self.mod = mod
        self.per_row = per_row
        self.nb = nb
        self.rows_per_b = rows_per_b
        self.tm = tm

    def mod_spec(self, k):
        tm = self.tm
        if self.per_row:
            return pl.BlockSpec((1, tm, D_MODEL), lambda i, *_: (k, i, 0))
        nb, rpb = self.nb, self.rows_per_b
        return pl.BlockSpec((1, 1, D_MODEL), lambda i, *_: (k * nb + (i * tm) // rpb, 0, 0))


def _row_spec(tm, n):
    return pl.BlockSpec((tm, n), lambda i, *_: (i, 0))


def _const_spec(shape):
    nd = len(shape)
    return pl.BlockSpec(shape, lambda *_: (0,) * nd)


def _adaln_kernel(c_ref, w_ref, b_ref, o_ref):
    x = _silu(c_ref[...]).astype(BF16)
    o_ref[0] = _bdot(x, w_ref[0].astype(BF16)) + b_ref[0]


def _adaln(c_all, ada_w, ada_b):
    depth, _, n = ada_w.shape
    rows = c_all.shape[0]
    tn = 1536
    return pl.pallas_call(
        _adaln_kernel,
        grid=(depth, n // tn),
        in_specs=[
            pl.BlockSpec((rows, D_MODEL), lambda l, j: (0, 0)),
            pl.BlockSpec((1, D_MODEL, tn), lambda l, j: (l, 0, j)),
            pl.BlockSpec((1, 1, tn), lambda l, j: (l, 0, j)),
        ],
        out_specs=pl.BlockSpec((1, rows, tn), lambda l, j: (l, 0, j)),
        out_shape=jax.ShapeDtypeStruct((depth, rows, n), F32),
        compiler_params=_params(("arbitrary", "arbitrary"), 40),
        name="adaln",
    )(c_all, ada_w, ada_b.reshape(depth, 1, n))


def _modulate(x, g, shift, scale):
    return _rms(x) * g * (1.0 + scale) + shift


def _mod_proj_kernel(x_ref, g_ref, sh_ref, sc_ref, *refs, n_out, rows_minor):
    w_refs, o_refs, t_refs = refs[:n_out], refs[n_out:2 * n_out], list(refs[2 * n_out:])
    h = _modulate(x_ref[...], g_ref[...], sh_ref[0], sc_ref[0]).astype(BF16)
    for k, (w_ref, o_ref) in enumerate(zip(w_refs, o_refs)):
        z = _bdot(h, w_ref[...])
        o_ref[...] = z.astype(o_ref.dtype)
        if k in rows_minor:
            t_refs.pop(0)[0] = z.T


def _mod_proj(tok, x, g, ws, dtypes, name, rows_minor=()):
    m = x.shape[0]
    tm = tok.tm
    out_specs = [_row_spec(tm, w.shape[1]) for w in ws]
    out_shape = [jax.ShapeDtypeStruct((m, w.shape[1]), dt) for w, dt in zip(ws, dtypes)]
    tiles = tok.rows_per_b // tm
    for k in rows_minor:
        cols = ws[k].shape[1]
        out_specs.append(pl.BlockSpec((1, cols, tm), lambda i: (i // tiles, 0, i % tiles)))
        out_shape.append(jax.ShapeDtypeStruct((tok.nb, cols, tok.rows_per_b), F32))
    return pl.pallas_call(
        functools.partial(_mod_proj_kernel, n_out=len(ws), rows_minor=tuple(rows_minor)),
        grid=(m // tm,),
        in_specs=[_row_spec(tm, D_MODEL), _const_spec((1, D_MODEL)), tok.mod_spec(0), tok.mod_spec(1)]
        + [_const_spec(w.shape) for w in ws],
        out_specs=out_specs,
        out_shape=out_shape,
        compiler_params=_params(("arbitrary",), 56),
        name=name,
    )(x, g.reshape(1, D_MODEL), tok.mod, tok.mod, *ws)


def _out_proj_kernel(x_ref, gt_ref, *refs, mode, heads):
    if mode == "plain":
        a_ref, w_ref, o_ref = refs
        a = a_ref[...].astype(BF16)
    elif mode == "dil":
        o0, o1, o2, l0, l1, l2, w_ref, o_ref = refs
        la, lb, lc = l0[...], l1[...], l2[...]
        mx = jnp.maximum(jnp.maximum(la, lb), lc)
        wa, wb, wc = jnp.exp(la - mx), jnp.exp(lb - mx), jnp.exp(lc - mx)
        a = ((wa * o0[...] + wb * o1[...] + wc * o2[...]) / (wa + wb + wc)).astype(BF16)
    else:
        s_ref, r_ref, ng_ref, w_ref, o_ref = refs
        dv = s_ref.shape[1] // heads
        parts = []
        for h in range(heads):
            parts.append(_rms(s_ref[:, h * dv:(h + 1) * dv]) * ng_ref[...])
        a = (jnp.concatenate(parts, axis=1) * _silu(r_ref[...])).astype(BF16)
    o_ref[...] = x_ref[...] + gt_ref[0] * _bdot(a, w_ref[...])


def _out_proj(tok, x, ins, w, mode, name, heads=1, col_blocks=None):
    m = x.shape[0]
    tm = tok.tm
    in_specs = [_row_spec(tm, D_MODEL), tok.mod_spec(2)]
    for k, a in enumerate(ins):
        if a.shape[0] != m:
            in_specs.append(_const_spec(a.shape))
        elif col_blocks and col_blocks[k] is not None:
            width, blk = col_blocks[k]
            in_specs.append(pl.BlockSpec((tm, width), lambda i, blk=blk: (i, blk)))
        else:
            in_specs.append(_row_spec(tm, a.shape[1]))
    in_specs.append(_const_spec(w.shape))
    return pl.pallas_call(
        functools.partial(_out_proj_kernel, mode=mode, heads=heads),
        grid=(m // tm,),
        in_specs=in_specs,
        out_specs=_row_spec(tm, D_MODEL),
        out_shape=jax.ShapeDtypeStruct((m, D_MODEL), F32),
        compiler_params=_params(("arbitrary",), 48),
        name=name,
    )(x, tok.mod, *ins, w)


MOE_GROUPS = 4
MOE_PER_GROUP = 4
MOE_FF = 256
ROUTER_LANES = 128


def _router_gates(logits):
    lane = lax.broadcasted_iota(jnp.int32, logits.shape, 1)
    lane_f = lane.astype(F32)
    neg = -jnp.inf
    glog = jnp.where(lane < MOE_GROUPS, logits, neg)
    gmax = jnp.max(glog, axis=-1, keepdims=True)
    grp = jnp.min(jnp.where(glog == gmax, lane_f, 1e9), axis=-1, keepdims=True)
    p_grp = 1.0 / jnp.sum(jnp.exp(glog - gmax), axis=-1, keepdims=True)
    e_grp = ((lane - MOE_GROUPS) >> 2).astype(F32)
    n_e = MOE_GROUPS * MOE_PER_GROUP
    in_grp = (lane >= MOE_GROUPS) & (lane < MOE_GROUPS + n_e) & (e_grp == grp)
    e_in = jnp.where(in_grp, logits, neg)
    v1 = jnp.max(e_in, axis=-1, keepdims=True)
    i1 = jnp.min(jnp.where(e_in == v1, lane_f, 1e9), axis=-1, keepdims=True)
    e2 = jnp.where(lane_f == i1, neg, e_in)
    v2 = jnp.max(e2, axis=-1, keepdims=True)
    i2 = jnp.min(jnp.where(e2 == v2, lane_f, 1e9), axis=-1, keepdims=True)
    t = jnp.exp(v2 - v1)
    w1 = p_grp / (1.0 + t)
    w2 = p_grp * t / (1.0 + t)
    return jnp.where(lane_f == i1, w1, 0.0) + jnp.where(lane_f == i2, w2, 0.0)


def _moe_kernel(x_ref, g_ref, sh_ref, sc_ref, gt_ref, wrh_ref, wrl_ref, br_ref, up_ref, dn_ref, *rest, final):
    if final:
        fg_ref, o_ref, h_sc, gate_sc, acc_sc = rest
    else:
        o_ref, h_sc, gate_sc, acc_sc = rest
    grp = pl.program_id(1)

    @pl.when(grp == 0)
    def _():
        h = _modulate(x_ref[...], g_ref[...], sh_ref[0], sc_ref[0])
        hh, hl = _split_hi_lo(h)
        h_sc[...] = hh
        logits = _bdot(hh, wrh_ref[...]) + _bdot(hl, wrh_ref[...]) + _bdot(hh, wrl_ref[...]) + br_ref[...]
        gate_sc[...] = _router_gates(logits)
        acc_sc[...] = jnp.zeros_like(acc_sc)

    h = h_sc[...]
    gates = gate_sc[...]
    lane = lax.broadcasted_iota(jnp.int32, gates.shape, 1)
    acts = []
    for e in range(MOE_PER_GROUP):
        col = jnp.sum(jnp.where(lane == MOE_GROUPS + MOE_PER_GROUP * grp + e, gates, 0.0), axis=-1, keepdims=True)
        hid = _bdot(h, up_ref[e])
        acts.append((_silu(hid[:, :MOE_FF]) * hid[:, MOE_FF:] * col).astype(BF16))
    acc_sc[...] += _bdot(jnp.concatenate(acts, axis=1), dn_ref[0])

    @pl.when(grp == MOE_GROUPS - 1)
    def _():
        y = x_ref[...] + gt_ref[0] * acc_sc[...]
        if final:
            y = _rms(y) * fg_ref[...]
        o_ref[...] = y


def _moe(tok, x, g, wr_hi, wr_lo, br, up, dn, final_g, name):
    m = x.shape[0]
    tm = tok.tm
    final = final_g is not None
    ins = [x, g.reshape(1, D_MODEL), tok.mod, tok.mod, tok.mod, wr_hi, wr_lo, br, up, dn]
    in_specs = [
        _row_spec(tm, D_MODEL), _const_spec((1, D_MODEL)), tok.mod_spec(3), tok.mod_spec(4), tok.mod_spec(5),
        _const_spec(wr_hi.shape), _const_spec(wr_lo.shape), _const_spec(br.shape),
        pl.BlockSpec((MOE_PER_GROUP,) + up.shape[1:], lambda i, e: (e, 0, 0)),
        pl.BlockSpec((1,) + dn.shape[1:], lambda i, e: (e, 0, 0)),
    ]
    if final:
        ins.append(final_g.reshape(1, D_MODEL))
        in_specs.append(_const_spec((1, D_MODEL)))
    return pl.pallas_call(
        functools.partial(_moe_kernel, final=final),
        grid=(m // tm, MOE_GROUPS),
        in_specs=in_specs,
        out_specs=_row_spec(tm, D_MODEL),
        out_shape=jax.ShapeDtypeStruct((m, D_MODEL), F32),
        scratch_shapes=[
            pltpu.VMEM((tm, D_MODEL), BF16),
            pltpu.VMEM((tm, ROUTER_LANES), F32),
            pltpu.VMEM((tm, D_MODEL), F32),
        ],
        compiler_params=_params(("arbitrary", "arbitrary"), 48),
        name=name,
    )(*ins)


NSA_KVH = 4
NSA_G = 4
NSA_CMP_LEN = 32
NSA_CMP_STRIDE = 16
NSA_CMP_HID = 128
NSA_SEL_LEN = 64
NSA_TOPN = 16
NSA_WIN = 512
NSA_FORCE_BONUS = 1000.0
KV_COLS = NSA_KVH * HD
CMP_TAIL = 8 * NSA_CMP_STRIDE


def _compress_rows(src_ref, n_ch, w1_ref, pe_ref, w2_ref):
    hid = [jnp.zeros((n_ch, 2 * NSA_CMP_HID), F32) for _ in range(2)]
    for r in range(NSA_CMP_STRIDE):
        for h in range(2):
            rows = src_ref[h, pl.ds(r, n_ch + 8, stride=NSA_CMP_STRIDE), :]
            lo = (rows[0:n_ch] + pe_ref[r]).astype(BF16)
            hi = (rows[1:n_ch + 1] + pe_ref[r + NSA_CMP_STRIDE]).astype(BF16)
            hid[h] = hid[h] + _bdot(lo, w1_ref[r]) + _bdot(hi, w1_ref[r + NSA_CMP_STRIDE])
    outs = [_bdot(_silu(h).astype(BF16), w2_ref[...]) for h in hid]
    return jnp.concatenate(outs, axis=1)


def _masked_softmax_rows(s, mask):
    sm = jnp.where(mask, s, NEG_BIG)
    m = jnp.max(sm, axis=-1, keepdims=True)
    e = jnp.where(mask, jnp.exp(sm - m), 0.0)
    den = jnp.sum(e, axis=-1, keepdims=True)
    return e / jnp.where(den > 0, den, 1.0), m, den


def _top_blocks(score, n_valid, topn):
    lane = lax.broadcasted_iota(jnp.int32, score.shape, 1)
    rank = jnp.zeros(score.shape, F32)
    for i in range(n_valid):
        si = score[:, i:i + 1]
        ahead = (si > score) | ((si == score) & (lane > i))
        rank = rank + jnp.where(ahead, 1.0, 0.0)
    return jnp.where((rank < topn) & (lane < n_valid), 1.0, 0.0)


def _nsa_prompt_kernel(q_ref, kc_ref, vc_ref, ks_ref, vs_ref, kw_ref, vw_ref, gt_ref, ov_ref, ex_ref, o_ref,
                       ks_sc, vs_sc, kw_sc, vw_sc, *, tq, seq, chunk):
    qi = pl.program_id(1)
    t0 = qi * tq

    @pl.when(qi == 0)
    def _():
        ks_sc[...] = ks_ref[...].astype(BF16)
        vs_sc[...] = vs_ref[...].astype(BF16)
        kw_sc[...] = kw_ref[...].astype(BF16)
        vw_sc[...] = vw_ref[...].astype(BF16)

    n_cmp = kc_ref.shape[0]
    n_blk = seq // NSA_SEL_LEN
    jpad = ov_ref.shape[0]
    wkeys = NSA_WIN + tq
    rows = NSA_G * tq
    gates = jax.nn.sigmoid(gt_ref[...])
    tpos = t0 + lax.broadcasted_iota(jnp.int32, (tq, 1), 0)

    def heads3(x):
        return x.reshape(NSA_G, tq, x.shape[-1])

    c_end = lax.broadcasted_iota(jnp.int32, (tq, n_cmp), 1) * NSA_CMP_STRIDE + NSA_CMP_LEN - 1
    cmp_ok = (c_end <= tpos) & (c_end < (n_cmp - 1) * NSA_CMP_STRIDE + NSA_CMP_LEN - 1)
    cmp_bias = jnp.where(cmp_ok, 0.0, MASK_BIAS)
    has_cmp = jnp.where(tpos >= NSA_CMP_LEN - 1, 1.0, 0.0)
    w0 = pl.multiple_of(jnp.maximum(t0 - NSA_WIN, 0), tq)
    wdist = tpos - (w0 + lax.broadcasted_iota(jnp.int32, (tq, wkeys), 1))
    win_bias = jnp.where((wdist >= 0) & (wdist <= NSA_WIN), 0.0, MASK_BIAS)
    n_sel_chunks = (t0 + tq + chunk - 1) // chunk
    k_last = pl.multiple_of((n_sel_chunks - 1) * chunk, chunk)
    causal_bias = jnp.where(k_last + lax.broadcasted_iota(jnp.int32, (tq, chunk), 1) <= tpos, 0.0, MASK_BIAS)
    t_lane = t0 + lax.broadcasted_iota(jnp.int32, (jpad, tq), 1)
    j_idx = lax.broadcasted_iota(jnp.int32, (jpad, tq), 0)
    tb = t_lane // NSA_SEL_LEN
    forced = (j_idx == 0) | (j_idx == tb) | (j_idx == tb - 1)

    for kvh in range(NSA_KVH):
        c0 = kvh * HD
        qs = jnp.concatenate(
            [q_ref[:, (kvh * NSA_G + g) * HD:(kvh * NSA_G + g + 1) * HD] for g in range(NSA_G)], axis=0)
        qs = (qs * (HD ** -0.5)).astype(BF16)

        s = heads3(_dot_nt(qs, kc_ref[:, c0:c0 + HD].astype(BF16))) + cmp_bias[None]
        e = jnp.exp(s - jnp.max(s, axis=-1, keepdims=True))
        p_c = e * (has_cmp[None] / jnp.sum(e, axis=-1, keepdims=True))
        o_cmp = _bdot(p_c.reshape(rows, n_cmp).astype(BF16), vc_ref[:, c0:c0 + HD].astype(BF16))
        ph, plo = _split_hi_lo(jnp.sum(p_c, axis=0))
        imp = _dot_nt(ov_ref[...], ph) + _dot_nt(ov_ref[...], plo)
        score = jnp.where(j_idx <= tb, imp + jnp.where(forced, NSA_FORCE_BONUS, 0.0), -jnp.inf)
        rank = jnp.zeros((jpad, tq), F32)
        for i in range(n_blk):
            si = score[i:i + 1, :]
            rank = rank + jnp.where((si > score) | ((si == score) & (j_idx > i)), 1.0, 0.0)
        sel_bias = jnp.where((rank < NSA_TOPN) & (j_idx < n_blk), 0.0, MASK_BIAS).astype(BF16)

        def sel_chunk(ci, carry, extra_bias):
            m, l, acc = carry
            k0 = pl.multiple_of(ci * chunk, chunk)
            kch = ks_sc[pl.ds(k0, chunk), c0:c0 + HD]
            vch = vs_sc[pl.ds(k0, chunk), c0:c0 + HD]
            bias = _dot_tn(sel_bias, ex_ref[ci])
            if extra_bias is not None:
                bias = bias + extra_bias
            sc = heads3(_dot_nt(qs, kch)) + bias[None]
            m_new = jnp.maximum(m, jnp.max(sc, axis=-1, keepdims=True))
            p = jnp.exp(sc - m_new)
            alpha = jnp.exp(m - m_new)
            l = alpha * l + jnp.sum(p, axis=-1, keepdims=True)
            pv = _bdot(p.reshape(rows, chunk).astype(BF16), vch)
            acc = alpha * acc + heads3(pv)
            return m_new, l, acc

        init = (jnp.full((NSA_G, tq, 1), NEG_BIG, F32), jnp.zeros((NSA_G, tq, 1), F32),
                jnp.zeros((NSA_G, tq, HD), F32))
        carry = lax.fori_loop(0, n_sel_chunks - 1, lambda ci, c: sel_chunk(ci, c, None), init)
        _, l_s, acc_s = sel_chunk(n_sel_chunks - 1, carry, causal_bias)
        o_sel = acc_s / l_s

        kwin = kw_sc[pl.ds(w0, wkeys), c0:c0 + HD]
        vwin = vw_sc[pl.ds(w0, wkeys), c0:c0 + HD]
        sw = heads3(_dot_nt(qs, kwin)) + win_bias[None]
        ew = jnp.exp(sw - jnp.max(sw, axis=-1, keepdims=True))
        o_win = heads3(_bdot(ew.reshape(rows, wkeys).astype(BF16), vwin)) / jnp.sum(ew, axis=-1, keepdims=True)
        o_cmp = heads3(o_cmp)

        outs = []
        for g in range(NSA_G):
            h = kvh * NSA_G + g
            outs.append(gates[:, 3 * h:3 * h + 1] * o_cmp[g] + gates[:, 3 * h + 1:3 * h + 2] * o_sel[g]
                        + gates[:, 3 * h + 2:3 * h + 3] * o_win[g])
        o_ref[:, kvh * NSA_G * HD:(kvh + 1) * NSA_G * HD] = jnp.concatenate(outs, axis=1).astype(o_ref.dtype)


def _compress_weights(pe, w1, w2):
    eye = jnp.eye(2, dtype=F32)
    w1r = w1.reshape(2, NSA_CMP_LEN, HD, NSA_CMP_HID)
    w1bd = jnp.einsum("hg,srdj->srhdgj", eye, w1r).reshape(2, NSA_CMP_LEN, 2 * HD, 2 * NSA_CMP_HID)
    w2bd = jnp.einsum("hg,sjd->shjgd", eye, w2).reshape(2, 2 * NSA_CMP_HID, 2 * HD)
    pe2 = jnp.tile(pe, (1, 1, 2)).reshape(2, NSA_CMP_LEN, 1, 2 * HD)
    return w1bd.astype(BF16), pe2, w2bd.astype(BF16)


def _compress_prompt_kernel(src_ref, w1_ref, pe_ref, w2_ref, o_ref, stage_sc, *, n_ch):
    rows = NSA_CMP_STRIDE * n_ch
    for h in range(2):
        stage_sc[h, 0:rows, :] = src_ref[:, 128 * h:128 * (h + 1)]
        stage_sc[h, rows:rows + CMP_TAIL, :] = jnp.zeros((CMP_TAIL, 128), F32)
    o_ref[0] = _compress_rows(stage_sc, n_ch, w1_ref.at[0], pe_ref.at[0], w2_ref[0])


def _compress_prompt(kv4, nb, seq, cw):
    w1r, pe2, w2 = cw
    n_ch = seq // NSA_CMP_STRIDE
    return pl.pallas_call(
        functools.partial(_compress_prompt_kernel, n_ch=n_ch),
        grid=(2, nb),
        in_specs=[
            pl.BlockSpec((seq, KV_COLS), lambda s, b: (b, s)),
            pl.BlockSpec((1,) + w1r.shape[1:], lambda s, b: (s, 0, 0, 0)),
            pl.BlockSpec((1,) + pe2.shape[1:], lambda s, b: (s, 0, 0, 0)),
            pl.BlockSpec((1,) + w2.shape[1:], lambda s, b: (s, 0, 0)),
        ],
        out_specs=pl.BlockSpec((1, n_ch, KV_COLS), lambda s, b: (s, b, 0)),
        out_shape=jax.ShapeDtypeStruct((2, nb * n_ch, KV_COLS), F32),
        scratch_shapes=[pltpu.VMEM((2, seq + CMP_TAIL, 128), F32)],
        compiler_params=_params(("arbitrary", "arbitrary"), 40),
        name="nsa_compress_prompt",
    )(kv4, w1r, pe2, w2)


def _overlap_matrix(n_cmp_pad, n_cmp, n_blk, lanes=128):
    c = jnp.arange(n_cmp_pad)[:, None] * NSA_CMP_STRIDE
    j = jnp.arange(lanes)[None, :] * NSA_SEL_LEN
    ok = (c < j + NSA_SEL_LEN) & (c + NSA_CMP_LEN > j)
    ok = ok & (jnp.arange(n_cmp_pad)[:, None] < n_cmp) & (jnp.arange(lanes)[None, :] < n_blk)
    return ok.astype(BF16)


def _expand_matrix(n_keys, chunk, lanes=128):
    key = jnp.arange(n_keys).reshape(n_keys // chunk, 1, chunk)
    j = jnp.arange(lanes).reshape(1, lanes, 1)
    return (key // NSA_SEL_LEN == j).astype(BF16)


def _nsa_prompt_attention(q, kc, vc, kv4, kvw, gates, nb, seq, tq=256, chunk=512):
    nq = seq // tq
    n_cmp = seq // NSA_CMP_STRIDE
    n_blk = seq // NSA_SEL_LEN
    jpad = -(-n_blk // 8) * 8
    ov = _overlap_matrix(n_cmp, n_cmp - 1, n_blk, jpad).T
    ex = _expand_matrix(seq, chunk, jpad)
    return pl.pallas_call(
        functools.partial(_nsa_prompt_kernel, tq=tq, seq=seq, chunk=chunk),
        grid=(nb, nq),
        in_specs=[
            pl.BlockSpec((tq, q.shape[1]), lambda b, i: (b * nq + i, 0)),
            pl.BlockSpec((n_cmp, KV_COLS), lambda b, i: (b, 0)),
            pl.BlockSpec((n_cmp, KV_COLS), lambda b, i: (b, 0)),
            pl.BlockSpec((seq, KV_COLS), lambda b, i: (b, 2)),
            pl.BlockSpec((seq, KV_COLS), lambda b, i: (b, 3)),
            pl.BlockSpec((seq, KV_COLS), lambda b, i: (b, 0)),
            pl.BlockSpec((seq, KV_COLS), lambda b, i: (b, 1)),
            pl.BlockSpec((tq, gates.shape[1]), lambda b, i: (b * nq + i, 0)),
            _const_spec(ov.shape),
            _const_spec(ex.shape),
        ],
        out_specs=pl.BlockSpec((tq, q.shape[1]), lambda b, i: (b * nq + i, 0)),
        out_shape=jax.ShapeDtypeStruct(q.shape, BF16),
        scratch_shapes=[pltpu.VMEM((seq, KV_COLS), BF16)] * 4,
        compiler_params=_params(("arbitrary", "arbitrary"), 48),
        name="nsa_prompt_attention",
    )(q, kc, vc, kv4, kv4, kvw, kvw, gates, ov, ex)


def _nsa_weights(w_in):
    nq = NSA_KVH * NSA_G * HD
    wg = jnp.pad(w_in[:, nq + 6 * KV_COLS:], ((0, 0), (0, 128 - 3 * NSA_KVH * NSA_G)))
    return [w_in[:, :nq].astype(BF16), w_in[:, nq:nq + 4 * KV_COLS].astype(BF16),
            w_in[:, nq + 4 * KV_COLS:nq + 6 * KV_COLS].astype(BF16), wg.astype(BF16)]


def _nsa_prompt(tok, x, g, nb, seq, w_in_parts, cw, w_out, name="nsa"):
    q, kv4, kvw, gates, kv4_t, kvw_t = _mod_proj(tok, x, g, w_in_parts, [F32, F32, F32, F32], name + "_proj",
                                                 rows_minor=(1, 2))
    cmp_rows = _compress_prompt(kv4, nb, seq, cw)
    o = _nsa_prompt_attention(q, cmp_rows[0], cmp_rows[1], kv4, kvw, gates, nb, seq)
    x_new = _out_proj(tok, x, [o], w_out, "plain", name + "_out")
    return x_new, kv4_t, kvw_t


T_PAD = 8
PAGE_SIZE = 128
PAGES_PER_STEP = 16


def _softmax_two(s1, mask1, v1t, s2, mask2, v2):
    m = jnp.maximum(jnp.max(jnp.where(mask1, s1, NEG_BIG), axis=-1, keepdims=True),
                    jnp.max(jnp.where(mask2, s2, NEG_BIG), axis=-1, keepdims=True))
    p1 = jnp.where(mask1, jnp.exp(jnp.where(mask1, s1, NEG_BIG) - m), 0.0)
    p2 = jnp.where(mask2, jnp.exp(jnp.where(mask2, s2, NEG_BIG) - m), 0.0)
    den = jnp.sum(p1, axis=-1, keepdims=True) + jnp.sum(p2, axis=-1, keepdims=True)
    acc = _dot_nt(p1.astype(BF16), v1t) + _bdot(p2.astype(BF16), v2)
    return acc / jnp.where(den > 0, den, 1.0), m, den


def _stack_heads(q_ref, kvh):
    qs = jnp.concatenate(
        [q_ref[0, :, (kvh * NSA_G + g) * HD:(kvh * NSA_G + g + 1) * HD] for g in range(NSA_G)], axis=0)
    return (qs * (HD ** -0.5)).astype(BF16)


def _nsa_sample_cmp_kernel(pt_ref, *refs, n_pages, n_valid):
    del pt_ref
    pages = refs[:PAGES_PER_STEP]
    q_ref, w1_ref, pe_ref, w2_ref, ov_ref, ocmp_ref, sel_ref, stage_sc = refs[PAGES_PER_STEP:]
    step = pl.program_id(1)
    n_rows = n_pages * PAGE_SIZE
    n_ch = n_rows // NSA_CMP_STRIDE

    @pl.when(step == 0)
    def _():
        for h in range(4):
            stage_sc[h, n_rows:n_rows + CMP_TAIL, :] = jnp.zeros((CMP_TAIL, 128), F32)

    for k, page in enumerate(pages):
        r0 = pl.multiple_of((step * PAGES_PER_STEP + k) * PAGE_SIZE, PAGE_SIZE)
        for h in range(4):
            pair = page[0, h // 2, 2 * (h % 2):2 * (h % 2) + 2].reshape(2 * HD, PAGE_SIZE)
            stage_sc[h, pl.ds(r0, PAGE_SIZE), :] = pair.T

    @pl.when(step == pl.num_programs(1) - 1)
    def _():
        kc = _compress_rows(stage_sc.at[0:2], n_ch, w1_ref.at[0], pe_ref.at[0], w2_ref[0]).astype(BF16)
        vc = _compress_rows(stage_sc.at[2:4], n_ch, w1_ref.at[1], pe_ref.at[1], w2_ref[1]).astype(BF16)
        tpos = n_rows + lax.broadcasted_iota(jnp.int32, (T_PAD, 1), 0)
        c_idx = lax.broadcasted_iota(jnp.int32, (T_PAD, n_ch), 1)
        cmp_mask = (c_idx * NSA_CMP_STRIDE + NSA_CMP_LEN - 1 <= tpos) & (c_idx < n_ch - 1)
        cmp_mask = jnp.concatenate([cmp_mask] * NSA_G, axis=0)
        lanes = sel_ref.shape[3]
        j_idx = lax.broadcasted_iota(jnp.int32, (T_PAD, lanes), 1)
        tb = tpos // NSA_SEL_LEN
        forced = (j_idx == 0) | (j_idx == tb) | (j_idx == tb - 1)
        n_blk = (n_rows + n_valid + NSA_SEL_LEN - 1) // NSA_SEL_LEN
        for kvh in range(NSA_KVH):
            c0 = kvh * HD
            qs = _stack_heads(q_ref, kvh)
            p_c, _, _ = _masked_softmax_rows(_dot_nt(qs, kc[:, c0:c0 + HD]), cmp_mask)
            o_cmp = _bdot(p_c.astype(BF16), vc[:, c0:c0 + HD])
            p_sum = p_c[0:T_PAD]
            for g in range(1, NSA_G):
                p_sum = p_sum + p_c[g * T_PAD:(g + 1) * T_PAD]
            ph, plo = _split_hi_lo(p_sum)
            imp = _bdot(ph, ov_ref[...]) + _bdot(plo, ov_ref[...])
            score = jnp.where(j_idx <= tb, imp + jnp.where(forced, NSA_FORCE_BONUS, 0.0), -jnp.inf)
            sel_ref[0, kvh] = _top_blocks(score, n_blk, NSA_TOPN)
            ocmp_ref[0, :, kvh * NSA_G * HD:(kvh + 1) * NSA_G * HD] = jnp.concatenate(
                [o_cmp[g * T_PAD:(g + 1) * T_PAD] for g in range(NSA_G)], axis=1)


def _rows_minor(a):
    nd = a.ndim
    return a.transpose((0,) + tuple(range(2, nd)) + (1,))


def _rows_major(a):
    nd = a.ndim
    return a.transpose((0, nd - 1) + tuple(range(1, nd - 1)))


def _page_specs(slot_pair):
    return [pl.BlockSpec((1, 2, NSA_KVH, HD, PAGE_SIZE),
                         lambda b, s, pt, k=k: (pt[b, s * PAGES_PER_STEP + k], slot_pair, 0, 0, 0))
            for k in range(PAGES_PER_STEP)]


def _nsa_sample_cmp(q, cache, page_table, cw, n_valid):
    w1r, pe2, w2 = cw
    nb, n_pages = page_table.shape
    n_rows = n_pages * PAGE_SIZE
    n_ch = n_rows // NSA_CMP_STRIDE
    n_blk = (n_rows + n_valid + NSA_SEL_LEN - 1) // NSA_SEL_LEN
    lanes = -(-n_blk // 128) * 128
    ov = _overlap_matrix(n_ch, n_ch - 1, n_blk, lanes)
    bspec = lambda shape: pl.BlockSpec(shape, lambda b, s, pt: (b,) + (0,) * (len(shape) - 1))
    cspec = lambda shape: pl.BlockSpec(shape, lambda b, s, pt: (0,) * len(shape))
    grid_spec = pltpu.PrefetchScalarGridSpec(
        num_scalar_prefetch=1,
        grid=(nb, n_pages // PAGES_PER_STEP),
        in_specs=_page_specs(0) + [bspec((1, T_PAD, q.shape[2])), cspec(w1r.shape), cspec(pe2.shape),
                                    cspec(w2.shape), cspec(ov.shape)],
        out_specs=[bspec((1, T_PAD, q.shape[2])), bspec((1, NSA_KVH, T_PAD, lanes))],
        scratch_shapes=[pltpu.VMEM((4, n_rows + CMP_TAIL, 128), F32)],
    )
    return pl.pallas_call(
        functools.partial(_nsa_sample_cmp_kernel, n_pages=n_pages, n_valid=n_valid),
        grid_spec=grid_spec,
        out_shape=[jax.ShapeDtypeStruct(q.shape, F32), jax.ShapeDtypeStruct((nb, NSA_KVH, T_PAD, lanes), F32)],
        compiler_params=_params(("arbitrary", "arbitrary"), 56),
        name="nsa_sample_cmp",
    )(page_table, *([cache] * PAGES_PER_STEP), q, w1r, pe2, w2, ov)


def _nsa_sample_attend_kernel(pt_ref, *refs, n_pages, n_valid):
    del pt_ref
    pages = refs[:PAGES_PER_STEP]
    (q_ref, sel_ref, ocmp_ref, new4_ref, neww_ref, win_ref, gt_ref, ex_ref, o_ref, ks_sc, vs_sc) = refs[PAGES_PER_STEP:]
    step = pl.program_id(1)
    n_rows = n_pages * PAGE_SIZE

    for k, page in enumerate(pages):
        r0 = pl.multiple_of((step * PAGES_PER_STEP + k) * PAGE_SIZE, PAGE_SIZE)
        for slot, dst in enumerate((ks_sc, vs_sc)):
            for h in range(NSA_KVH):
                dst[h, :, pl.ds(r0, PAGE_SIZE)] = page[0, slot, h].astype(BF16)

    @pl.when(step == pl.num_programs(1) - 1)
    def _():
        def tile_g(x):
            return jnp.concatenate([x] * NSA_G, axis=0)

        lb = win_ref.shape[4]
        t_idx = lax.broadcasted_iota(jnp.int32, (T_PAD, 1), 0)
        i_new = lax.broadcasted_iota(jnp.int32, (T_PAD, T_PAD), 1)
        new_mask = tile_g((i_new <= t_idx) & (i_new < n_valid))
        w_idx = lax.broadcasted_iota(jnp.int32, (T_PAD, lb), 1)
        win_mask = tile_g(w_idx >= lb + t_idx - NSA_WIN)
        gates = jax.nn.sigmoid(gt_ref[0])
        for kvh in range(NSA_KVH):
            c0 = kvh * HD
            qs = _stack_heads(q_ref, kvh)
            chosen = tile_g(_bdot(sel_ref[0, kvh].astype(BF16), ex_ref[...]) > 0.5)
            k_new = new4_ref[0, :, 2 * KV_COLS + c0:2 * KV_COLS + c0 + HD].astype(BF16)
            v_new = new4_ref[0, :, 3 * KV_COLS + c0:3 * KV_COLS + c0 + HD].astype(BF16)
            o_sel, _, _ = _softmax_two(_bdot(qs, ks_sc[kvh]), chosen, vs_sc[kvh], _dot_nt(qs, k_new), new_mask, v_new)
            kw_old = win_ref[0, 0, kvh].astype(BF16)
            vw_old = win_ref[0, 1, kvh].astype(BF16)
            kw_new = neww_ref[0, :, c0:c0 + HD].astype(BF16)
            vw_new = neww_ref[0, :, KV_COLS + c0:KV_COLS + c0 + HD].astype(BF16)
            o_win, _, _ = _softmax_two(_bdot(qs, kw_old), win_mask, vw_old, _dot_nt(qs, kw_new), new_mask, vw_new)
            outs = []
            for g in range(NSA_G):
                h = kvh * NSA_G + g
                r = slice(g * T_PAD, (g + 1) * T_PAD)
                outs.append(gates[:, 3 * h:3 * h + 1] * ocmp_ref[0, :, h * HD:(h + 1) * HD]
                            + gates[:, 3 * h + 1:3 * h + 2] * o_sel[r] + gates[:, 3 * h + 2:3 * h + 3] * o_win[r])
            o_ref[0, :, kvh * NSA_G * HD:(kvh + 1) * NSA_G * HD] = jnp.concatenate(outs, axis=1).astype(o_ref.dtype)


def _nsa_sample_attend(q, sel, ocmp, new4, neww, win, gates, cache, page_table, n_valid):
    nb, n_pages = page_table.shape
    n_rows = n_pages * PAGE_SIZE
    lanes = sel.shape[3]
    ex = _expand_matrix(n_rows, n_rows, lanes)[0]
    bspec = lambda shape: pl.BlockSpec(shape, lambda b, s, pt: (b,) + (0,) * (len(shape) - 1))
    cspec = lambda shape: pl.BlockSpec(shape, lambda b, s, pt: (0,) * len(shape))
    grid_spec = pltpu.PrefetchScalarGridSpec(
        num_scalar_prefetch=1,
        grid=(nb, n_pages // PAGES_PER_STEP),
        in_specs=_page_specs(1) + [
            bspec((1, T_PAD, q.shape[2])), bspec((1,) + sel.shape[1:]), bspec((1, T_PAD, ocmp.shape[2])),
            bspec((1, T_PAD, new4.shape[2])), bspec((1, T_PAD, neww.shape[2])), bspec((1,) + win.shape[1:]),
            bspec((1, T_PAD, gates.shape[2])), cspec(ex.shape)],
        out_specs=bspec((1, T_PAD, q.shape[2])),
        scratch_shapes=[pltpu.VMEM((NSA_KVH, HD, n_rows), BF16)] * 2,
    )
    return pl.pallas_call(
        functools.partial(_nsa_sample_attend_kernel, n_pages=n_pages, n_valid=n_valid),
        grid_spec=grid_spec,
        out_shape=jax.ShapeDtypeStruct(q.shape, BF16),
        compiler_params=_params(("arbitrary", "arbitrary"), 56),
        name="nsa_sample_attend",
    )(page_table, *([cache] * PAGES_PER_STEP), q, sel, ocmp, new4, neww, win, gates, ex)


def _nsa_sample(tok, x, g, nb, cache, win, page_table, n_valid, w_in_parts, cw, w_out, name="nsa_s"):
    q, kv4, kvw, gates = _mod_proj(tok, x, g, w_in_parts, [F32, F32, F32, F32], name + "_proj")
    r3 = lambda a: a.reshape(nb, T_PAD, a.shape[1])
    cache_t, win_t = _rows_minor(cache), _rows_minor(win)
    ocmp, sel = _nsa_sample_cmp(r3(q), cache_t, page_table, cw, n_valid)
    o = _nsa_sample_attend(r3(q), sel, ocmp, r3(kv4), r3(kvw), win_t, r3(gates), cache_t, page_table, n_valid)
    x_new = _out_proj(tok, x, [o.reshape(nb * T_PAD, o.shape[2])], w_out, "plain", name + "_out")
    return x_new, r3(kv4), r3(kvw)


DIL_PAIRS = ((128, 1), (512, 4), (2048, 16))
DIL_SLOTS = 8
DIL_COLS = DIL_SLOTS * HD


def _dil_prompt_kernel(q_ref, k_ref, v_ref, o_ref, l_ref, *, tq, ls, nback):
    t0 = pl.program_id(2) * tq
    wkeys = min(nback + tq, ls)
    w0 = pl.multiple_of(jnp.maximum(t0 - nback, 0), tq)
    tpos = t0 + lax.broadcasted_iota(jnp.int32, (tq, 1), 0)
    dist = tpos - (w0 + lax.broadcasted_iota(jnp.int32, (tq, wkeys), 1))
    bias = jnp.where((dist >= 0) & (dist <= nback), 0.0, MASK_BIAS)
    outs, lses = [], []
    for h in range(DIL_SLOTS):
        c0 = h * HD
        qh = (q_ref[:, c0:c0 + HD] * (HD ** -0.5)).astype(BF16)
        kh = k_ref[pl.ds(w0, wkeys), c0:c0 + HD].astype(BF16)
        vh = v_ref[pl.ds(w0, wkeys), c0:c0 + HD].astype(BF16)
        o, lse = _biased_attention(qh, kh, vh, bias)
        outs.append(o)
        lses.append(jnp.broadcast_to(lse, (tq, HD)))
    o_ref[...] = jnp.concatenate(outs, axis=1)
    l_ref[...] = jnp.concatenate(lses, axis=1)


def _biased_attention(q, k, v, bias):
    s = _dot_nt(q, k) + bias
    m = jnp.max(s, axis=-1, keepdims=True)
    e = jnp.exp(s - m)
    den = jnp.sum(e, axis=-1, keepdims=True)
    return _bdot(e.astype(BF16), v) / den, m + jnp.log(den)


def _dil_prompt_rows16_kernel(q_ref, kv_ref, o_ref, l_ref, *, dil, nback):
    n_a = q_ref.shape[0]
    sets = 16 // dil
    a_q = lax.broadcasted_iota(jnp.int32, (n_a, n_a), 0)
    a_k = lax.broadcasted_iota(jnp.int32, (n_a, n_a), 1)
    for r in range(dil):
        for h in range(DIL_SLOTS):
            c0 = h * HD
            ks = [kv_ref[:, r + dil * e, c0:c0 + HD].astype(BF16) for e in range(sets)]
            vs = [kv_ref[:, r + dil * e, DIL_COLS + c0:DIL_COLS + c0 + HD].astype(BF16) for e in range(sets)]
            k_all = jnp.concatenate(ks, axis=0) if sets > 1 else ks[0]
            v_all = jnp.concatenate(vs, axis=0) if sets > 1 else vs[0]
            for e in range(sets):
                dist = jnp.concatenate([(sets * a_q + e) - (sets * a_k + e2) for e2 in range(sets)], axis=1) \
                    if sets > 1 else a_q - a_k
                bias = jnp.where((dist >= 0) & (dist <= nback), 0.0, MASK_BIAS)
                qh = (q_ref[:, r + dil * e, c0:c0 + HD] * (HD ** -0.5)).astype(BF16)
                o, lse = _biased_attention(qh, k_all, v_all, bias)
                o_ref[:, r + dil * e, c0:c0 + HD] = o
                l_ref[:, r + dil * e, c0:c0 + HD] = jnp.broadcast_to(lse, (n_a, HD))


def _dil_prompt_group_rows16(q3, kv, gi, nb, seq):
    window, dil = DIL_PAIRS[gi]
    n_a = seq // 16
    qv = q3.reshape(nb * n_a, 16, 3 * DIL_COLS)
    kvv = kv.reshape(nb * n_a, 16, 2 * DIL_COLS)
    out_sd = jax.ShapeDtypeStruct((nb * n_a, 16, DIL_COLS), F32)
    o, lse = pl.pallas_call(
        functools.partial(_dil_prompt_rows16_kernel, dil=dil, nback=window // dil),
        grid=(nb,),
        in_specs=[
            pl.BlockSpec((n_a, 16, DIL_COLS), lambda b: (b, 0, gi)),
            pl.BlockSpec((n_a, 16, 2 * DIL_COLS), lambda b: (b, 0, 0)),
        ],
        out_specs=[pl.BlockSpec((n_a, 16, DIL_COLS), lambda b: (b, 0, 0))] * 2,
        out_shape=[out_sd, out_sd],
        compiler_params=_params(("arbitrary",), 52),
        name=f"dil_prompt_g{gi}",
    )(qv, kvv)
    return o.reshape(nb * seq, DIL_COLS), lse.reshape(nb * seq, DIL_COLS)


def _dil_prompt_group(q3, kv, gi, nb, seq, tq=128):
    window, dil = DIL_PAIRS[gi]
    ls = seq // dil
    nq = ls // tq
    qv = q3.reshape(nb * ls, dil * 3 * DIL_COLS)
    kvv = kv.reshape(nb * ls, dil * 2 * DIL_COLS)
    out_sd = jax.ShapeDtypeStruct((nb * ls, dil * DIL_COLS), F32)
    o, lse = pl.pallas_call(
        functools.partial(_dil_prompt_kernel, tq=tq, ls=ls, nback=window // dil),
        grid=(nb, dil, nq),
        in_specs=[
            pl.BlockSpec((tq, DIL_COLS), lambda b, r, i: (b * nq + i, 3 * r + gi)),
            pl.BlockSpec((ls, DIL_COLS), lambda b, r, i: (b, 2 * r)),
            pl.BlockSpec((ls, DIL_COLS), lambda b, r, i: (b, 2 * r + 1)),
        ],
        out_specs=[pl.BlockSpec((tq, DIL_COLS), lambda b, r, i: (b * nq + i, r))] * 2,
        out_shape=[out_sd, out_sd],
        compiler_params=_params(("arbitrary", "arbitrary", "arbitrary"), 40),
        name=f"dil_prompt_g{gi}",
    )(qv, kvv, kvv)
    return o.reshape(nb * seq, DIL_COLS), lse.reshape(nb * seq, DIL_COLS)


def _dil_weights(w_in):
    n_g = len(DIL_PAIRS)
    w = w_in.reshape(D_MODEL, 3, n_g, DIL_COLS)
    parts = [w[:, 0].reshape(D_MODEL, n_g * DIL_COLS)]
    for gi in range(n_g):
        parts.append(jnp.concatenate([w[:, 1, gi], w[:, 2, gi]], axis=1))
    return [p.astype(BF16) for p in parts]


def _dil_prompt(tok, x, g, nb, seq, w_in_parts, w_out, name="dil"):
    q3, kv0, kv1, kv2, kv0_t, kv1_t, kv2_t = _mod_proj(tok, x, g, w_in_parts, [F32] * 4, name + "_proj",
                                                       rows_minor=(1, 2, 3))
    kvs = (kv0, kv1, kv2)
    outs, lses = [], []
    for gi in range(len(DIL_PAIRS)):
        group = _dil_prompt_group_rows16 if DIL_PAIRS[gi][1] == 16 else _dil_prompt_group
        o, lse = group(q3, kvs[gi], gi, nb, seq)
        outs.append(o)
        lses.append(lse)
    x_new = _out_proj(tok, x, outs + lses, w_out, "dil", name + "_out")
    return x_new, (kv0_t, kv1_t, kv2_t)


def _shifted_state(old_t, new_rows, n_valid):
    hd, lb = old_t.shape
    rolled = pltpu.roll(old_t, lb - n_valid, 1)
    tail = jnp.concatenate([new_rows.T, jnp.zeros((hd, 128 - T_PAD), F32)], axis=1)
    tail = pltpu.roll(tail, 128 - n_valid, 1)
    lane = lax.broadcasted_iota(jnp.int32, (hd, 128), 1)
    last = jnp.where(lane >= 128 - n_valid, tail, rolled[:, lb - 128:])
    return rolled, last


def _dil_sample_kernel(q_ref, st_ref, new_ref, o_ref, l_ref, ns_ref, *, window, dil, n_valid):
    lb = st_ref.shape[4]
    for kv in range(2):
        for h in range(DIL_SLOTS):
            c0 = kv * DIL_COLS + h * HD
            rolled, last = _shifted_state(st_ref[0, kv, h], new_ref[0, :, c0:c0 + HD], n_valid)
            if lb > 128:
                ns_ref[0, kv, h, :, 0:lb - 128] = rolled[:, 0:lb - 128]
            ns_ref[0, kv, h, :, lb - 128:lb] = last
    t_idx = lax.broadcasted_iota(jnp.int32, (T_PAD, 1), 0)
    d_old = lb + t_idx - lax.broadcasted_iota(jnp.int32, (T_PAD, lb), 1)
    old_mask = ((d_old & (dil - 1)) == 0) & (d_old <= window)
    i_new = lax.broadcasted_iota(jnp.int32, (T_PAD, T_PAD), 1)
    d_new = t_idx - i_new
    new_mask = (d_new >= 0) & ((d_new & (dil - 1)) == 0) & (i_new < n_valid)
    outs, lses = [], []
    for h in range(DIL_SLOTS):
        c0 = h * HD
        qh = (q_ref[0, :, c0:c0 + HD] * (HD ** -0.5)).astype(BF16)
        k_old = st_ref[0, 0, h].astype(BF16)
        v_old = st_ref[0, 1, h].astype(BF16)
        k_new = new_ref[0, :, c0:c0 + HD].astype(BF16)
        v_new = new_ref[0, :, DIL_COLS + c0:DIL_COLS + c0 + HD].astype(BF16)
        o, m, den = _softmax_two(_bdot(qh, k_old), old_mask, v_old, _dot_nt(qh, k_new), new_mask, v_new)
        outs.append(o)
        lses.append(jnp.broadcast_to(m + jnp.log(den), (T_PAD, HD)))
    o_ref[0] = jnp.concatenate(outs, axis=1)
    l_ref[0] = jnp.concatenate(lses, axis=1)


def _dil_sample_group(q3, kv_new, state, gi, n_valid):
    window, dil = DIL_PAIRS[gi]
    nb, lb = state.shape[:2]
    assert lb == window and lb % 128 == 0, "the next state is the shifted buffer only for a full window"
    state_t = _rows_minor(state)
    out_sd = jax.ShapeDtypeStruct((nb, T_PAD, DIL_COLS), F32)
    return pl.pallas_call(
        functools.partial(_dil_sample_kernel, window=window, dil=dil, n_valid=n_valid),
        grid=(nb,),
        in_specs=[
            pl.BlockSpec((1, T_PAD, DIL_COLS), lambda b: (b, 0, gi)),
            pl.BlockSpec((1, 2, DIL_SLOTS, HD, lb), lambda b: (b, 0, 0, 0, 0)),
            pl.BlockSpec((1, T_PAD, 2 * DIL_COLS), lambda b: (b, 0, 0)),
        ],
        out_specs=[pl.BlockSpec((1, T_PAD, DIL_COLS), lambda b: (b, 0, 0))] * 2
        + [pl.BlockSpec((1, 2, DIL_SLOTS, HD, lb), lambda b: (b, 0, 0, 0, 0))],
        out_shape=[out_sd, out_sd, jax.ShapeDtypeStruct(state_t.shape, F32)],
        compiler_params=_params(("arbitrary",), 56),
        name=f"dil_sample_g{gi}",
    )(q3, state_t, kv_new)


def _dil_sample(tok, x, g, nb, states, n_valid, w_in_parts, w_out, name="dil_s"):
    q3, kv0, kv1, kv2 = _mod_proj(tok, x, g, w_in_parts, [F32] * 4, name + "_proj")
    r3 = lambda a: a.reshape(nb, T_PAD, a.shape[1])
    kvs = (r3(kv0), r3(kv1), r3(kv2))
    outs, lses, new_states = [], [], []
    for gi in range(len(DIL_PAIRS)):
        o, lse, ns = _dil_sample_group(r3(q3), kvs[gi], states[gi], gi, n_valid)
        outs.append(o.reshape(nb * T_PAD, DIL_COLS))
        lses.append(lse.reshape(nb * T_PAD, DIL_COLS))
        new_states.append(_rows_major(ns))
    x_new = _out_proj(tok, x, outs + lses, w_out, "dil", name + "_out")
    return x_new, new_states


GLA_H, GLA_DK, GLA_DV, GLA_RANK, GLA_TAU = 4, 128, 256, 16, 16.0
HG_H, HG_DK, HG_DV = 8, 128, 128
SCAN_SUB = 8
SCAN_HEADS_PER_STEP = 4
SCAN_UNROLL = 2


def _log_sigmoid(x):
    return jnp.minimum(x, 0.0) - jnp.log1p(jnp.exp(-jnp.abs(x)))


def _cumsum_rows(g):
    n = g.shape[0]
    tri = (lax.broadcasted_iota(jnp.int32, (n, n), 0) >= lax.broadcasted_iota(jnp.int32, (n, n), 1)).astype(BF16)
    hi = g.astype(BF16)
    r1 = g - hi.astype(F32)
    mid = r1.astype(BF16)
    lo = (r1 - mid.astype(F32)).astype(BF16)
    return _bdot(tri, hi) + _bdot(tri, mid) + _bdot(tri, lo)


def _scan_chunk(q, k, g, v, st):
    n, dk = q.shape
    sub = min(SCAN_SUB, n)
    b = _cumsum_rows(g)
    b_end = b[n - 1:n]
    o = _dot_nt((q * jnp.exp(b)).astype(BF16), st.astype(BF16))

    lane = lax.broadcasted_iota(jnp.int32, (sub, n), 1)
    row = lax.broadcasted_iota(jnp.int32, (sub, n), 0)
    ones = jnp.ones((dk, n), BF16)
    a_rows = []
    for i in range(n // sub):
        lo = i * sub
        qi, ki, bi = q[lo:lo + sub], k[lo:lo + sub], b[lo:lo + sub]
        prods = []
        for s in range(sub):
            e = jnp.exp(jnp.minimum(bi - bi[s:s + 1], 0.0))
            prods.append((qi * ki[s:s + 1] * e).astype(BF16))
        sums = _bdot(jnp.concatenate(prods, axis=0), ones)
        a_i = jnp.zeros((sub, n), F32)
        for s in range(sub):
            a_i = a_i + jnp.where((lane == lo + s) & (row >= s), sums[s * sub:(s + 1) * sub], 0.0)
        if i > 0:
            b_ref = b[lo - 1:lo]
            qd = (qi * jnp.exp(bi - b_ref)).astype(BF16)
            kd = (k[0:lo] * jnp.exp(b_ref - b[0:lo])).astype(BF16)
            if lo < n:
                kd = jnp.concatenate([kd, jnp.zeros((n - lo, dk), BF16)], axis=0)
            a_i = a_i + _dot_nt(qd, kd)
        a_rows.append(a_i)
    a = jnp.concatenate(a_rows, axis=0) if len(a_rows) > 1 else a_rows[0]
    o = o + _bdot(a.astype(BF16), v.astype(BF16))
    kd_end = (k * jnp.exp(b_end - b)).astype(BF16)
    st_new = st * jnp.exp(b_end) + _dot_tn(v.astype(BF16), kd_end)
    return o, st_new


def _scan_kernel(*refs, kind, chunk, n_chunks, n_valid, hpb, dk, dv):
    if kind == "gla":
        q_ref, k_ref, v_ref, a_ref, wa_ref, ba_ref, s0_ref, o_ref, sf_ref, st_sc = refs
    else:
        f_ref, v_ref, q_ref, lb_ref, s0_ref, o_ref, sf_ref, st_sc = refs
    ci = pl.program_id(2)

    @pl.when(ci == 0)
    def _():
        for hh in range(hpb):
            st_sc[hh] = s0_ref[0, hh].T

    def body(j, carry):
        r0 = pl.multiple_of(j * chunk, chunk)
        rows = pl.ds(r0, chunk)
        for hh in range(hpb):
            kc = slice(hh * dk, (hh + 1) * dk)
            vc = slice(hh * dv, (hh + 1) * dv)
            if kind == "gla":
                q = q_ref[rows, kc] * (GLA_DK ** -0.5)
                k = k_ref[rows, kc]
                pre = _bdot(a_ref[rows, :].astype(BF16), wa_ref[:, kc]) + ba_ref[:, kc]
                g = _log_sigmoid(pre) * (1.0 / GLA_TAU)
            else:
                fz = f_ref[rows, kc]
                lb = lb_ref[:, kc]
                q = _silu(q_ref[rows, kc])
                la = jnp.log(lb)
                lc = jnp.log1p(-lb) + _log_sigmoid(fz)
                g = jnp.maximum(la, lc) + jnp.log1p(jnp.exp(-jnp.abs(la - lc)))
                k = (1.0 - lb) * jax.nn.sigmoid(-fz)
            if n_valid < chunk:
                live = lax.broadcasted_iota(jnp.int32, k.shape, 0) < n_valid
                k = jnp.where(live, k, 0.0)
                g = jnp.where(live, g, 0.0)
            o, st = _scan_chunk(q, k, g, v_ref[rows, vc], st_sc[hh])
            st_sc[hh] = st
            o_ref[rows, vc] = o
        return carry

    lax.fori_loop(0, n_chunks, body, 0, unroll=min(SCAN_UNROLL, n_chunks))

    @pl.when(ci == pl.num_programs(2) - 1)
    def _():
        for hh in range(hpb):
            sf_ref[0, hh] = st_sc[hh].T


def _scan(kind, ins, s0_t, nb, rows_per_b, n_valid, name):
    if kind == "gla":
        heads, dk, dv = GLA_H, GLA_DK, GLA_DV
    else:
        heads, dk, dv = HG_H, HG_DK, HG_DV
    chunk = min(64, rows_per_b)
    blk = min(512, rows_per_b)
    nblk = rows_per_b // blk
    hpb = SCAN_HEADS_PER_STEP
    rspec = lambda w: pl.BlockSpec((blk, hpb * w), lambda b, h, i: (b * nblk + i, h))
    if kind == "gla":
        q, k, v, a, wa, ba = ins
        args = [q, k, v, a, wa, ba]
        in_specs = [rspec(dk), rspec(dk), rspec(dv),
                    pl.BlockSpec((blk, a.shape[1]), lambda b, h, i: (b * nblk + i, 0)),
                    pl.BlockSpec((wa.shape[0], hpb * dk), lambda b, h, i: (0, h)),
                    pl.BlockSpec((1, hpb * dk), lambda b, h, i: (0, h))]
    else:
        f, v, q, lb = ins
        args = [f, v, q, lb]
        in_specs = [rspec(dk), rspec(dv), rspec(dk), pl.BlockSpec((1, hpb * dk), lambda b, h, i: (0, h))]
    m = nb * rows_per_b
    st_spec = pl.BlockSpec((1, hpb, dk, dv), lambda b, h, i: (b, h, 0, 0))
    return pl.pallas_call(
        functools.partial(_scan_kernel, kind=kind, chunk=chunk, n_chunks=blk // chunk, n_valid=n_valid,
                          hpb=hpb, dk=dk, dv=dv),
        grid=(nb, heads // hpb, nblk),
        in_specs=in_specs + [st_spec],
        out_specs=[rspec(dv), st_spec],
        out_shape=[jax.ShapeDtypeStruct((m, heads * dv), F32), jax.ShapeDtypeStruct((nb, heads, dk, dv), F32)],
        scratch_shapes=[pltpu.VMEM((hpb, dv, dk), F32)],
        compiler_params=_params(("arbitrary", "arbitrary", "arbitrary"), 40),
        name=name,
    )(*args, s0_t)


def _gla_weights(w_in, w_a2, b_a2):
    nk, nv = GLA_H * GLA_DK, GLA_H * GLA_DV
    wa = jnp.pad(w_in[:, 2 * nk + 2 * nv:], ((0, 0), (0, 128 - GLA_RANK)))
    parts = [w_in[:, :nk], w_in[:, nk:2 * nk], w_in[:, 2 * nk:2 * nk + nv], w_in[:, 2 * nk + nv:2 * nk + 2 * nv], wa]
    wa2 = jnp.pad(w_a2, ((0, 128 - GLA_RANK), (0, 0))).astype(BF16)
    return [p.astype(BF16) for p in parts], wa2, b_a2.reshape(1, nk)


def _gla(tok, x, g, s0_t, nb, rows_per_b, n_valid, wts, norm_g, w_out, name="gla"):
    parts, wa2, ba2 = wts
    q, k, v, r, a = _mod_proj(tok, x, g, parts, [F32] * 5, name + "_proj")
    o, s_t = _scan("gla", [q, k, v, a, wa2, ba2], s0_t, nb, rows_per_b, n_valid, name + "_scan")
    x_new = _out_proj(tok, x, [o, r, norm_g.reshape(1, GLA_DV)], w_out, "heads", name + "_out", heads=GLA_H)
    return x_new, s_t


def _hgrn_weights(w_in, lb_logits, layer):
    nk, nv = HG_H * HG_DK, HG_H * HG_DV
    parts = [w_in[:, :nk], w_in[:, nk:nk + nv], w_in[:, nk + nv:2 * nk + nv], w_in[:, 2 * nk + nv:]]
    sm = jax.nn.softmax(lb_logits.astype(F32), axis=0)
    lb = jnp.sum(sm[1:layer + 1], axis=0).reshape(1, nk)
    return [p.astype(BF16) for p in parts], lb


def _hgrn(tok, x, g, s0_t, nb, rows_per_b, n_valid, wts, norm_g, w_out, name="hgrn"):
    parts, lb = wts
    f, i_in, q, og = _mod_proj(tok, x, g, parts, [F32] * 4, name + "_proj")
    o, s_t = _scan("hgrn", [f, i_in, q, lb], s0_t, nb, rows_per_b, n_valid, name + "_scan")
    x_new = _out_proj(tok, x, [o, og, norm_g.reshape(1, HG_DV)], w_out, "heads", name + "_out", heads=HG_H)
    return x_new, s_t


PROMPT_TM = 256
PROMPT_MOE_TM = 512


def _moe_weights(w_rg, b_rg, w_re, b_re, w_up, w_down):
    n_e = MOE_GROUPS * MOE_PER_GROUP
    wr = jnp.zeros((D_MODEL, ROUTER_LANES), F32).at[:, :MOE_GROUPS].set(w_rg).at[:, MOE_GROUPS:MOE_GROUPS + n_e].set(w_re)
    br = jnp.zeros((1, ROUTER_LANES), F32).at[0, :MOE_GROUPS].set(b_rg).at[0, MOE_GROUPS:MOE_GROUPS + n_e].set(b_re)
    wr_hi, wr_lo = _split_hi_lo(wr)
    dn = w_down.reshape(MOE_GROUPS, MOE_PER_GROUP * MOE_FF, D_MODEL).astype(BF16)
    return wr_hi, wr_lo, br, w_up.astype(BF16), dn


def kernel(x_prompt, x_sample, cache_nsa_kv, state_nsa_win, state_dil_0, state_dil_1, state_dil_2, state_gla, state_hgrn, page_table, c_prompt, c_sample, nsa_w_in, nsa_cmp_pe, nsa_cmp_w1, nsa_cmp_w2, nsa_w_out, dil_w_in, dil_w_out, gla_w_in, gla_w_a2, gla_b_a2, gla_norm_g, gla_w_out, hg_w_in, hg_lb_logits, hg_norm_g, hg_w_out, norm_g, ada_w, ada_b, moe_w_rg, moe_b_rg, moe_w_re, moe_b_re, moe_w_up, moe_w_down, final_norm_g):
    nb, seq, _ = x_prompt.shape
    ndb, n_t, _ = x_sample.shape
    depth = ada_w.shape[0]
    ms_rows = ndb * T_PAD

    c_rows = -(-(nb + ndb) // 16) * 16
    c_all = jnp.zeros((c_rows, D_MODEL), F32).at[:nb].set(c_prompt).at[nb:nb + ndb].set(c_sample)
    mod = _adaln(c_all, ada_w, ada_b)

    xp = x_prompt.reshape(nb * seq, D_MODEL)
    xs = jnp.pad(x_sample, ((0, 0), (0, T_PAD - n_t), (0, 0))).reshape(ms_rows, D_MODEL)

    outs = {}
    for i in range(depth):
        mp = mod[i, :nb].reshape(nb, ADA_N, D_MODEL).transpose(1, 0, 2).reshape(ADA_N * nb, 1, D_MODEL)
        ms = mod[i, nb:nb + ndb].reshape(ndb, ADA_N, D_MODEL).transpose(1, 0, 2)
        ms = jnp.repeat(ms, T_PAD, axis=1)
        tok_p = _Tokens(mp, False, nb, seq, PROMPT_TM)
        tok_pm = _Tokens(mp, False, nb, seq, PROMPT_MOE_TM)
        tok_s = _Tokens(ms, True, 1, ms_rows, ms_rows)
        g_mix, g_moe = norm_g[i, 0], norm_g[i, 1]
        kind = i % 4
        if kind == 0:
            parts = _nsa_weights(nsa_w_in)
            cw = _compress_weights(nsa_cmp_pe, nsa_cmp_w1, nsa_cmp_w2)
            w_out = nsa_w_out.astype(BF16)
            xp, kv4, kvw = _nsa_prompt(tok_p, xp, g_mix, nb, seq, parts, cw, w_out)
            xs, kv4_s, kvw_s = _nsa_sample(tok_s, xs, g_mix, ndb, cache_nsa_kv, state_nsa_win, page_table, n_t,
                                           parts, cw, w_out)
            keep = min(NSA_WIN, seq)
            outs["nsa_kv_p"] = _rows_major(kv4.reshape(nb, 4, NSA_KVH, HD, seq))
            outs["nsa_kv_s"] = kv4_s[:, :n_t].reshape(ndb, n_t, 4, NSA_KVH, HD).astype(cache_nsa_kv.dtype)
            outs["nsa_win_p"] = _rows_major(kvw.reshape(nb, 2, NSA_KVH, HD, seq)[..., seq - keep:])
            win_all = jnp.concatenate(
                [state_nsa_win, kvw_s[:, :n_t].reshape(ndb, n_t, 2, NSA_KVH, HD).astype(state_nsa_win.dtype)], axis=1)
            outs["nsa_win_s"] = win_all[:, win_all.shape[1] - min(NSA_WIN, win_all.shape[1]):]
        elif kind == 1:
            parts = _dil_weights(dil_w_in)
            w_out = dil_w_out.astype(BF16)
            states = (state_dil_0, state_dil_1, state_dil_2)
            xp, kvs = _dil_prompt(tok_p, xp, g_mix, nb, seq, parts, w_out)
            xs, kvs_s = _dil_sample(tok_s, xs, g_mix, ndb, states, n_t, parts, w_out)
            for gi, (window, _) in enumerate(DIL_PAIRS):
                buf = kvs[gi].reshape(nb, 2, DIL_SLOTS, HD, seq)
                outs[f"dil_p{gi}"] = _rows_major(buf[..., seq - min(window, seq):])
                outs[f"dil_s{gi}"] = kvs_s[gi].astype(states[gi].dtype)
        elif kind == 2:
            wts = _gla_weights(gla_w_in, gla_w_a2, gla_b_a2)
            w_out = gla_w_out.astype(BF16)
            zero = jnp.zeros((nb, GLA_H, GLA_DK, GLA_DV), F32)
            xp, s_p = _gla(tok_p, xp, g_mix, zero, nb, seq, seq, wts, gla_norm_g, w_out)
            xs, s_s = _gla(tok_s, xs, g_mix, state_gla.astype(F32), ndb, T_PAD, n_t, wts,
                           gla_norm_g, w_out, name="gla_s")
            outs["gla_p"] = s_p.astype(state_gla.dtype)
            outs["gla_s"] = s_s.astype(state_gla.dtype)
        else:
            wts = _hgrn_weights(hg_w_in, hg_lb_logits, i)
            w_out = hg_w_out.astype(BF16)
            zero = jnp.zeros((nb, HG_H, HG_DK, HG_DV), F32)
            xp, s_p = _hgrn(tok_p, xp, g_mix, zero, nb, seq, seq, wts, hg_norm_g, w_out)
            xs, s_s = _hgrn(tok_s, xs, g_mix, state_hgrn.astype(F32), ndb, T_PAD, n_t, wts,
                            hg_norm_g, w_out, name="hgrn_s")
            outs["hg_p"] = s_p.astype(state_hgrn.dtype)
            outs["hg_s"] = s_s.astype(state_hgrn.dtype)
        mw = _moe_weights(moe_w_rg[i], moe_b_rg[i], moe_w_re[i], moe_b_re[i], moe_w_up[i], moe_w_down[i])
        fg = final_norm_g if i == depth - 1 else None
        xp = _moe(tok_pm, xp, g_moe, *mw, fg, f"moe_p{i}")
        xs = _moe(tok_s, xs, g_moe, *mw, fg, f"moe_s{i}")

    y_prompt = xp.reshape(nb, seq, D_MODEL)
    y_sample = xs.reshape(ndb, T_PAD, D_MODEL)[:, :n_t]
    return (y_prompt, y_sample, outs["nsa_kv_p"], outs["nsa_kv_s"], outs["nsa_win_p"], outs["nsa_win_s"],
            outs["dil_p0"], outs["dil_s0"], outs["dil_p1"], outs["dil_s1"], outs["dil_p2"], outs["dil_s2"],
            outs["gla_p"], outs["gla_s"], outs["hg_p"], outs["hg_s"])
```
